```python
import math, functools
import jax, jax.numpy as jnp
from jax import lax
import numpy as np

D_MODEL = 1024
BATCH = 16
SEQ = 2048
DEPTH = 2

CTX_LEN = 256
GRID_W = 64
EPS = 1e-6
N_MOD = 6
MOD_INIT = 0.5

ATT_HEADS = 8
ATT_KV_HEADS = 2
ATT_GROUP = ATT_HEADS // ATT_KV_HEADS
ATT_HEAD_DIM = 64
ATT_WINDOW = 128
ATT_BLOCK = 128
ROPE_BASE = 10000.0
ROPE_FREQS = ATT_HEAD_DIM // 4
ATT_Q = ATT_HEADS * ATT_HEAD_DIM
ATT_KV = ATT_KV_HEADS * ATT_HEAD_DIM

ML_HEADS = 4
ML_HEAD_DIM = 128
ML_CHUNK = 128
ML_INNER = ML_HEADS * ML_HEAD_DIM

SSD_HEADS = 8
SSD_HEAD_DIM = 64
SSD_GROUPS = 2
SSD_HPG = SSD_HEADS // SSD_GROUPS
SSD_STATE = 128
SSD_CONV = 5
SSD_CHUNK = 128
SSD_INNER = SSD_HEADS * SSD_HEAD_DIM
SSD_XBC = SSD_INNER + 2 * SSD_GROUPS * SSD_STATE

D_FF = -((-8 * D_MODEL) // (3 * 256)) * 256

IN_SPLITS = (ATT_Q, ATT_KV, ATT_KV, ML_INNER, ML_INNER, ML_INNER, ML_INNER, 4 * ML_HEADS,
             SSD_INNER, SSD_XBC, 2 * SSD_HEADS, 3 * D_MODEL)
IN_WIDTH = sum(IN_SPLITS)

kernel_name = "hybrid_gqa_mlstm_ssd_prefix_block"


def rmsnorm(x, g):
    xf = x.astype(jnp.float32)
    y = xf * lax.rsqrt(jnp.mean(xf * xf, axis=-1, keepdims=True) + EPS)
    return (y * g.astype(jnp.float32)).astype(x.dtype)


def split_columns(p):
    offs = [int(o) for o in np.cumsum(IN_SPLITS)[:-1]]
    return jnp.split(p, offs, axis=-1)


def axial_rope_tables(n_tokens):
    rows = n_tokens // GRID_W
    row = jnp.repeat(jnp.arange(rows, dtype=jnp.float32), GRID_W)
    col = jnp.tile(jnp.arange(GRID_W, dtype=jnp.float32), rows)
    inv = ROPE_BASE ** (-jnp.arange(ROPE_FREQS, dtype=jnp.float32) / ROPE_FREQS)
    ang = jnp.stack([row[:, None] * inv, col[:, None] * inv], axis=1)
    return jnp.cos(ang), jnp.sin(ang)


def apply_axial_rope(x, cos, sin):
    shp = x.shape
    xr = x.astype(jnp.float32).reshape(*shp[:-1], 2, 2, ROPE_FREQS)
    bshape = (shp[1],) + (1,) * (len(shp) - 3) + (2, ROPE_FREQS)
    cs, sn = cos.reshape(bshape), sin.reshape(bshape)
    x1, x2 = xr[..., 0, :], xr[..., 1, :]
    out = jnp.stack([x1 * cs - x2 * sn, x1 * sn + x2 * cs], axis=-2)
    return out.reshape(shp).astype(x.dtype)


def latent_window_attention(q, k, v, k_ctx, v_ctx, sink):
    f32 = jnp.float32
    B, T = q.shape[:2]
    nb = T // ATT_BLOCK
    pad = ((0, 0), (ATT_BLOCK, ATT_BLOCK), (0, 0), (0, 0))
    kp, vp = jnp.pad(k, pad), jnp.pad(v, pad)
    scale = ATT_HEAD_DIM ** -0.5
    sk = sink.astype(f32).reshape(1, ATT_KV_HEADS, ATT_GROUP, 1, 1)

    def one_block(j):
        start = j * ATT_BLOCK
        qb = lax.dynamic_slice_in_dim(q, start, ATT_BLOCK, axis=1)
        kb = lax.dynamic_slice_in_dim(kp, start, 3 * ATT_BLOCK, axis=1)
        vb = lax.dynamic_slice_in_dim(vp, start, 3 * ATT_BLOCK, axis=1)
        qpos = start + jnp.arange(ATT_BLOCK)
        kpos = start - ATT_BLOCK + jnp.arange(3 * ATT_BLOCK)
        valid = ((jnp.abs(qpos[:, None] - kpos[None, :]) <= ATT_WINDOW)
                 & (kpos[None, :] >= 0) & (kpos[None, :] < T))
        s_loc = jnp.einsum('bqgrd,bkgd->bgrqk', qb, kb).astype(f32) * scale
        s_loc = jnp.where(valid, s_loc, -jnp.inf)
        s_ctx = jnp.einsum('bqgrd,bcgd->bgrqc', qb, k_ctx).astype(f32) * scale
        m = jnp.maximum(jnp.maximum(jnp.max(s_loc, -1, keepdims=True), jnp.max(s_ctx, -1, keepdims=True)), sk)
        p_loc = jnp.exp(s_loc - m)
        p_ctx = jnp.exp(s_ctx - m)
        den = jnp.sum(p_loc, -1, keepdims=True) + jnp.sum(p_ctx, -1, keepdims=True) + jnp.exp(sk - m)
        o = (jnp.einsum('bgrqk,bkgd->bgrqd', p_loc, vb)
             + jnp.einsum('bgrqc,bcgd->bgrqd', p_ctx, v_ctx)) / den
        return jnp.transpose(o, (0, 3, 1, 2, 4)).reshape(B, ATT_BLOCK, ATT_Q)

    out = lax.map(one_block, jnp.arange(nb))
    return jnp.moveaxis(out, 0, 1).reshape(B, T, ATT_Q)


def context_attention(q, k, v, sink):
    f32 = jnp.float32
    B, Lc = q.shape[:2]
    s = jnp.einsum('bqgrd,bkgd->bgrqk', q, k).astype(f32) * ATT_HEAD_DIM ** -0.5
    sk = sink.astype(f32).reshape(1, ATT_KV_HEADS, ATT_GROUP, 1, 1)
    m = jnp.maximum(jnp.max(s, -1, keepdims=True), sk)
    p = jnp.exp(s - m)
    den = jnp.sum(p, -1, keepdims=True) + jnp.exp(sk - m)
    o = jnp.einsum('bgrqk,bkgd->bqgrd', p / den, v)
    return o.reshape(B, Lc, ATT_Q)


def mlstm_chunk_scan(q, k, v, log_i, log_f, state):
    f32 = jnp.float32
    B, T, H, _ = q.shape
    Dv = v.shape[-1]
    nc = T // ML_CHUNK

    def chunks(a):
        a = a.astype(f32).reshape(B, nc, ML_CHUNK, *a.shape[2:])
        return jnp.moveaxis(a, 1, 0)

    qs, ks, vs = chunks(q), chunks(k), chunks(v)
    gi = jnp.moveaxis(chunks(log_i), -1, 2)
    gf = jnp.moveaxis(chunks(log_f), -1, 2)
    causal = jnp.tril(jnp.ones((ML_CHUNK, ML_CHUNK), bool))

    def step(carry, inp):
        C, n, m = carry
        qc, kc, vc, ic, fc = inp
        b = jnp.cumsum(fc, axis=-1)
        log_d = jnp.where(causal, b[..., :, None] - b[..., None, :] + ic[..., None, :], -jnp.inf)
        log_prev = b + m[..., None]
        m_t = jnp.maximum(jnp.max(log_d, axis=-1), log_prev)
        d = jnp.exp(log_d - m_t[..., None])
        prev = jnp.exp(log_prev - m_t)
        s = jnp.einsum('bthd,bshd->bhts', qc, kc) * d
        num = jnp.einsum('bhts,bshv->bhtv', s, vc) + prev[..., None] * jnp.einsum('bthd,bhvd->bhtv', qc, C)
        den = jnp.sum(s, axis=-1) + prev * jnp.einsum('bthd,bhd->bht', qc, n)
        h = num / jnp.maximum(jnp.abs(den), jnp.exp(-m_t))[..., None]
        b_end = b[..., -1]
        log_w = b_end[..., None] - b + ic
        m_new = jnp.maximum(b_end + m, jnp.max(log_w, axis=-1))
        w = jnp.exp(log_w - m_new[..., None])
        decay = jnp.exp(b_end + m - m_new)
        C_new = decay[..., None, None] * C + jnp.einsum('bhs,bshv,bshd->bhvd', w, vc, kc)
        n_new = decay[..., None] * n + jnp.einsum('bhs,bshd->bhd', w, kc)
        return (C_new, n_new, m_new), h

    final, hs = lax.scan(step, tuple(state), (qs, ks, vs, gi, gf))
    h = jnp.transpose(hs, (1, 0, 3, 2, 4)).reshape(B, T, H, Dv)
    return h, final


def ssd_chunk_scan(x, dt, bmat, cmat, state, a):
    f32 = jnp.float32
    B, T, G, R, P = x.shape
    nc = T // SSD_CHUNK

    def chunks(arr):
        arr = arr.astype(f32).reshape(B, nc, SSD_CHUNK, *arr.shape[2:])
        return jnp.moveaxis(arr, 1, 0)

    dt = dt.astype(f32)
    xs = chunks(x.astype(f32) * dt[..., None])
    las = chunks(dt * a.astype(f32))
    bs, cs = chunks(bmat), chunks(cmat)
    causal = jnp.tril(jnp.ones((SSD_CHUNK, SSD_CHUNK), bool))[:, :, None, None]

    def step(S, inp):
        xc, lc, bc, cc = inp
        acs = jnp.cumsum(lc, axis=1)
        seg = jnp.where(causal, acs[:, :, None] - acs[:, None, :], -jnp.inf)
        cb = jnp.einsum('btgn,bsgn->btsg', cc, bc)
        y = jnp.einsum('btsg,btsgr,bsgrp->btgrp', cb, jnp.exp(seg), xc)
        y = y + jnp.exp(acs)[..., None] * jnp.einsum('btgn,bgrpn->btgrp', cc, S)
        a_end = acs[:, -1]
        w = jnp.exp(a_end[:, None] - acs)
        S_new = jnp.exp(a_end)[..., None, None] * S + jnp.einsum('bsgn,bsgr,bsgrp->bgrpn', bc, w, xc)
        return S_new, y

    final, ys = lax.scan(step, state, (xs, las, bs, cs))
    y = jnp.moveaxis(ys, 0, 1).reshape(B, T, G, R, P)
    return y, final


def two_way_scan(scan_fwd, scan_bwd, init, ctx_fwd, lat_fwd, ctx_bwd, lat_bwd):
    rev = lambda arrs: tuple(t[:, ::-1] for t in arrs)
    hc_f, s_f = scan_fwd(*ctx_fwd, init)
    hl_f, _ = scan_fwd(*lat_fwd, s_f)
    hc_b, s_b = scan_bwd(*rev(ctx_bwd), init)
    hl_b, _ = scan_bwd(*rev(lat_bwd), s_b)
    return hc_f + hc_b[:, ::-1], hl_f + hl_b[:, ::-1]


def depthwise_conv(u, w, b):
    out = lax.conv_general_dilated(
        u, w[:, None, :].astype(u.dtype), window_strides=(1,),
        padding=((SSD_CONV // 2, SSD_CONV // 2),),
        dimension_numbers=('NWC', 'WIO', 'NWC'), feature_group_count=u.shape[-1])
    return out + b


def attention_mixer(lat, ctx, sink, rope, need_ctx):
    def heads(q, k, v):
        B, T = q.shape[:2]
        return (q.reshape(B, T, ATT_KV_HEADS, ATT_GROUP, ATT_HEAD_DIM),
                k.reshape(B, T, ATT_KV_HEADS, ATT_HEAD_DIM),
                v.reshape(B, T, ATT_KV_HEADS, ATT_HEAD_DIM))
    ql, kl, vl = heads(*lat)
    qc, kc, vc = heads(*ctx)
    cos, sin = rope
    ql, kl = apply_axial_rope(ql, cos, sin), apply_axial_rope(kl, cos, sin)
    out_l = latent_window_attention(ql, kl, vl, kc, vc, sink)
    out_c = context_attention(qc, kc, vc, sink) if need_ctx else None
    return out_l, out_c


def mlstm_mixer(lat, ctx, i_bias, f_bias, head_gain, need_ctx):
    f32 = jnp.float32
    ib, fb = i_bias.astype(f32), f_bias.astype(f32)

    def prep(q, k, v, o, g):
        B, T = q.shape[:2]
        shp = (B, T, ML_HEADS, ML_HEAD_DIM)
        g = g.astype(f32).reshape(B, T, 4, ML_HEADS)
        q, k, v = q.reshape(shp), k.reshape(shp) * ML_HEAD_DIM ** -0.5, v.reshape(shp)
        fwd = (q, k, v, g[:, :, 0] + ib[0], jax.nn.log_sigmoid(g[:, :, 1] + fb[0]))
        bwd = (q, k, v, g[:, :, 2] + ib[1], jax.nn.log_sigmoid(g[:, :, 3] + fb[1]))
        return fwd, bwd

    lat_f, lat_b = prep(*lat)
    ctx_f, ctx_b = prep(*ctx)
    B = lat[0].shape[0]
    init = (jnp.zeros((B, ML_HEADS, ML_HEAD_DIM, ML_HEAD_DIM), f32),
            jnp.zeros((B, ML_HEADS, ML_HEAD_DIM), f32),
            jnp.zeros((B, ML_HEADS), f32))
    hc, hl = two_way_scan(mlstm_chunk_scan, mlstm_chunk_scan, init, ctx_f, lat_f, ctx_b, lat_b)
    gain = head_gain.reshape(ML_HEADS, ML_HEAD_DIM)

    def finish(h, o):
        B_, T = h.shape[:2]
        return rmsnorm(h, gain).reshape(B_, T, ML_INNER) * jax.nn.sigmoid(o.astype(f32))

    out_l = finish(hl, lat[3])
    out_c = finish(hc, ctx[3]) if need_ctx else None
    return out_l, out_c


def ssd_mixer(lat, ctx, conv_w, conv_b, dt_bias, a_log, d_skip, norm_gain, need_ctx):
    f32 = jnp.float32
    a = -jnp.exp(a_log.astype(f32)).reshape(2, SSD_GROUPS, SSD_HPG)
    dtb = dt_bias.astype(f32).reshape(2 * SSD_HEADS)
    d = d_skip.astype(f32).reshape(SSD_GROUPS, SSD_HPG, 1)

    def prep(z, xbc, dt):
        B, T = z.shape[:2]
        xbc = jax.nn.silu(depthwise_conv(xbc, conv_w, conv_b))
        xs, bm, cm = jnp.split(xbc, [SSD_INNER, SSD_INNER + SSD_GROUPS * SSD_STATE], axis=-1)
        xs = xs.reshape(B, T, SSD_GROUPS, SSD_HPG, SSD_HEAD_DIM)
        bm = bm.reshape(B, T, SSD_GROUPS, SSD_STATE)
        cm = cm.reshape(B, T, SSD_GROUPS, SSD_STATE)
        dt = jax.nn.softplus(dt.astype(f32) + dtb).reshape(B, T, 2, SSD_GROUPS, SSD_HPG)
        return xs, (xs, dt[:, :, 0], bm, cm), (xs, dt[:, :, 1], bm, cm)

    xs_l, lat_f, lat_b = prep(*lat)
    xs_c, ctx_f, ctx_b = prep(*ctx)
    B = lat[0].shape[0]
    init = jnp.zeros((B, SSD_GROUPS, SSD_HPG, SSD_HEAD_DIM, SSD_STATE), f32)
    yc, yl = two_way_scan(functools.partial(ssd_chunk_scan, a=a[0]), functools.partial(ssd_chunk_scan, a=a[1]),
                          init, ctx_f, lat_f, ctx_b, lat_b)

    def finish(y, xs, z):
        B_, T = z.shape[:2]
        y = (y + d * xs.astype(f32)).reshape(B_, T, SSD_INNER)
        return rmsnorm(y * jax.nn.silu(z.astype(f32)), norm_gain)

    out_l = finish(yl, xs_l, lat[0])
    out_c = finish(yc, xs_c, ctx[0]) if need_ctx else None
    return out_l, out_c


def hybrid_layer(xl, xc, mod_l, mod_c, prm, rope, need_ctx):
    def pre(x, mod, g, i):
        return rmsnorm(x, g) * (1 + mod[..., i + 1, :]) + mod[..., i, :]

    pl = split_columns(pre(xl, mod_l, prm['g_norm1'], 0) @ prm['w_in'])
    pc = split_columns(pre(xc, mod_c, prm['g_norm1'], 0) @ prm['w_in'])
    att_l, att_c = attention_mixer(pl[0:3], pc[0:3], prm['att_sink'], rope, need_ctx)
    ml_l, ml_c = mlstm_mixer(pl[3:8], pc[3:8], prm['ml_i_bias'], prm['ml_f_bias'], prm['ml_head_gain'], need_ctx)
    ss_l, ss_c = ssd_mixer(pl[8:11], pc[8:11], prm['ssd_conv_w'], prm['ssd_conv_b'], prm['ssd_dt_bias'],
                           prm['ssd_a_log'], prm['ssd_d'], prm['ssd_norm_gain'], need_ctx)

    def merge_and_ffn(x, mod, att, ml, ss, gates):
        ga, gm, gs = jnp.split(jax.nn.sigmoid(gates.astype(jnp.float32)), 3, axis=-1)
        y = (ga * (att.astype(x.dtype) @ prm['w_att_out'])
             + gm * (ml.astype(x.dtype) @ prm['w_ml_out'])
             + gs * (ss.astype(x.dtype) @ prm['w_ssd_out']))
        x = x + mod[..., 2, :] * (y.astype(x.dtype) @ prm['w_o'])
        h = pre(x, mod, prm['g_norm2'], 3)
        gate, up = jnp.split(h @ prm['w_up'], 2, axis=-1)
        return x + mod[..., 5, :] * ((jax.nn.silu(gate) * up) @ prm['w_down'])

    xl = merge_and_ffn(xl, mod_l, att_l, ml_l, ss_l, pl[11])
    if need_ctx:
        xc = merge_and_ffn(xc, mod_c, att_c, ml_c, ss_c, pc[11])
    return xl, xc


def setup_inputs(seed: int = 0) -> dict:
    key = jax.random.key(seed)
    ks = iter(jax.random.split(key, 40))
    f32 = jnp.float32
    L, D = DEPTH, D_MODEL

    def normal(shape, scale):
        return jax.random.normal(next(ks), shape, f32) * scale

    def gain(shape):
        return 1.0 + normal(shape, 0.02)

    dt0 = jnp.exp(jax.random.uniform(next(ks), (L, 2, SSD_HEADS), f32, math.log(1e-3), math.log(1e-1)))
    a0 = jax.random.uniform(next(ks), (L, 2, SSD_HEADS), f32, 1.0, 16.0)
    return {
        "x": normal((BATCH, SEQ, D), 1.0),
        "c": normal((BATCH, D), 1.0),
        "ctx": normal((BATCH, CTX_LEN, D), 1.0),
        "c_ctx": normal((D,), 1.0),
        "w_mod": normal((L, D, N_MOD * D), MOD_INIT * D ** -0.5),
        "b_mod": normal((L, N_MOD * D), 0.02),
        "g_norm1": gain((L, D)),
        "w_in": normal((L, D, IN_WIDTH), D ** -0.5),
        "att_sink": normal((L, ATT_HEADS), 0.5),
        "ml_i_bias": normal((L, 2, ML_HEADS), 0.1),
        "ml_f_bias": jnp.linspace(3.0, 6.0, ML_HEADS, dtype=f32) + normal((L, 2, ML_HEADS), 0.1),
        "ml_head_gain": gain((L, ML_INNER)),
        "ssd_conv_w": normal((L, SSD_CONV, SSD_XBC), SSD_CONV ** -0.5),
        "ssd_conv_b": normal((L, SSD_XBC), 0.02),
        "ssd_dt_bias": dt0 + jnp.log(-jnp.expm1(-dt0)),
        "ssd_a_log": jnp.log(a0),
        "ssd_d": 1.0 + normal((L, SSD_HEADS), 0.1),
        "ssd_norm_gain": gain((L, SSD_INNER)),
        "w_att_out": normal((L, ATT_Q, D), ATT_Q ** -0.5),
        "w_ml_out": normal((L, ML_INNER, D), ML_INNER ** -0.5),
        "w_ssd_out": normal((L, SSD_INNER, D), SSD_INNER ** -0.5),
        "w_o": normal((L, D, D), D ** -0.5),
        "g_norm2": gain((L, D)),
        "w_up": normal((L, D, 2 * D_FF), D ** -0.5),
        "w_down": normal((L, D_FF, D), D_FF ** -0.5),
        "g_final": gain((D,)),
    }


def reference(x, c, ctx, c_ctx, w_mod, b_mod, g_norm1, w_in, att_sink, ml_i_bias, ml_f_bias, ml_head_gain,
              ssd_conv_w, ssd_conv_b, ssd_dt_bias, ssd_a_log, ssd_d, ssd_norm_gain,
              w_att_out, w_ml_out, w_ssd_out, w_o, g_norm2, w_up, w_down, g_final):
    rope = axial_rope_tables(x.shape[1])
    xl, xc = x, ctx
    for l in range(DEPTH):
        mod_l = (jax.nn.silu(c) @ w_mod[l] + b_mod[l]).reshape(c.shape[0], 1, N_MOD, -1)
        mod_c = (jax.nn.silu(c_ctx) @ w_mod[l] + b_mod[l]).reshape(1, 1, N_MOD, -1)
        prm = dict(g_norm1=g_norm1[l], w_in=w_in[l], att_sink=att_sink[l],
                   ml_i_bias=ml_i_bias[l], ml_f_bias=ml_f_bias[l], ml_head_gain=ml_head_gain[l],
                   ssd_conv_w=ssd_conv_w[l], ssd_conv_b=ssd_conv_b[l], ssd_dt_bias=ssd_dt_bias[l],
                   ssd_a_log=ssd_a_log[l], ssd_d=ssd_d[l], ssd_norm_gain=ssd_norm_gain[l],
                   w_att_out=w_att_out[l], w_ml_out=w_ml_out[l], w_ssd_out=w_ssd_out[l], w_o=w_o[l],
                   g_norm2=g_norm2[l], w_up=w_up[l], w_down=w_down[l])
        xl, xc = hybrid_layer(xl, xc, mod_l, mod_c, prm, rope, need_ctx=(l < DEPTH - 1))
    return rmsnorm(xl, g_final)
```

```python
import functools
import math

import numpy as np
import jax
import jax.numpy as jnp
from jax import lax
from jax.experimental import pallas as pl
from jax.experimental.pallas import tpu as pltpu

F32 = jnp.float32
BF16 = jnp.bfloat16

D_MODEL = 1024
EPS = 1e-6
N_MOD = 6
GRID_W = 64
ROPE_BASE = 10000.0

ATT_HEADS = 8
ATT_KV_HEADS = 2
ATT_GROUP = ATT_HEADS // ATT_KV_HEADS
ATT_HEAD_DIM = 64
ATT_WINDOW = 128
ATT_Q = ATT_HEADS * ATT_HEAD_DIM
ATT_KV = ATT_KV_HEADS * ATT_HEAD_DIM
ROPE_FREQS = ATT_HEAD_DIM // 4

ML_HEADS = 4
ML_HEAD_DIM = 128
ML_INNER = ML_HEADS * ML_HEAD_DIM

SSD_HEADS = 8
SSD_HEAD_DIM = 64
SSD_GROUPS = 2
SSD_HPG = SSD_HEADS // SSD_GROUPS
SSD_STATE = 128
SSD_CONV = 5
SSD_INNER = SSD_HEADS * SSD_HEAD_DIM
SSD_XBC = SSD_INNER + 2 * SSD_GROUPS * SSD_STATE

D_FF = -((-8 * D_MODEL) // (3 * 256)) * 256

CHUNK = 128
LANES = 128
HALO = 8
ROW_TILE = 512
FFN_TILE = 256
MOD_ROWS = 24
VMEM_LIMIT = 56 * 1024 * 1024

W_Q0 = 0
W_KV0 = ATT_Q
W_ML0 = W_KV0 + 2 * ATT_KV
W_Z0 = W_ML0 + 4 * ML_INNER
W_XBC0 = W_Z0 + SSD_INNER
W_G0 = W_XBC0 + SSD_XBC
W_S0 = W_G0 + 3 * D_MODEL
W_TOT = W_S0 + LANES
DT_LANE0 = 4 * ML_HEADS


def _cparams(sem):
    return pltpu.CompilerParams(dimension_semantics=sem, vmem_limit_bytes=VMEM_LIMIT)


def _resident(shape):
    nd = len(shape)
    return pl.BlockSpec(shape, lambda *_: (0,) * nd, pipeline_mode=pl.Buffered(1))


def _sigmoid(x):
    return 1.0 / (1.0 + jnp.exp(-x))


def _softplus(x):
    return jnp.maximum(x, 0.0) + jnp.log1p(jnp.exp(-jnp.abs(x)))


def _dot(a, b):
    return jnp.dot(a, b, preferred_element_type=F32)


def _dot_nt(a, b):
    return lax.dot_general(a, b, (((1,), (1,)), ((), ())), preferred_element_type=F32)


def _dot_exact(a, b):
    return jnp.dot(a, b, preferred_element_type=F32, precision=lax.Precision.HIGHEST)


def _mod_kernel(c_ref, w_ref, b_ref, o_ref):
    c = c_ref[...]
    a = (c * _sigmoid(c)).astype(BF16)
    o_ref[...] = _dot(a, w_ref[...].astype(BF16)) + b_ref[...]


def _modulation(cc, w_mod, b_mod):
    depth, d, n = w_mod.shape
    tn = 1536
    return pl.pallas_call(
        _mod_kernel,
        out_shape=jax.ShapeDtypeStruct((depth, MOD_ROWS, n), F32),
        grid=(depth, n // tn),
        in_specs=[
            pl.BlockSpec((MOD_ROWS, d), lambda l, j: (0, 0)),
            pl.BlockSpec((None, d, tn), lambda l, j: (l, 0, j)),
            pl.BlockSpec((None, 1, tn), lambda l, j: (l, 0, j)),
        ],
        out_specs=pl.BlockSpec((None, MOD_ROWS, tn), lambda l, j: (l, 0, j)),
        compiler_params=_cparams(("arbitrary", "arbitrary")),
        name="modulation",
    )(cc, w_mod, b_mod.reshape(depth, 1, n))


def _rope(x, cos, sin_signed, first_half):
    partner = jnp.where(first_half, pltpu.roll(x, LANES - ROPE_FREQS, 1), pltpu.roll(x, ROPE_FREQS, 1))
    return x * cos + partner * sin_signed


def _inproj_kernel(x_ref, mod_ref, g_ref, cos_ref, sin_ref, w_ref,
                   oq_ref, okv_ref, oml_ref, oz_ref, oxbc_ref, og_ref, os_ref):
    x = x_ref[...]
    y = x * lax.rsqrt(jnp.mean(x * x, axis=-1, keepdims=True) + EPS) * g_ref[...]
    mod = mod_ref[...]
    h = (y * (1.0 + mod[1:2]) + mod[0:1]).astype(BF16)

    cos = cos_ref[...]
    sin = sin_ref[...]
    lane = lax.broadcasted_iota(jnp.int32, (1, LANES), 1)
    first_half = (lane % (2 * ROPE_FREQS)) < ROPE_FREQS

    def proj(c0, width):
        return _dot(h, w_ref[:, c0:c0 + width])

    q = proj(W_Q0, ATT_Q)
    for s in range(ATT_Q // LANES):
        qs = _rope(q[:, s * LANES:(s + 1) * LANES], cos, sin, first_half)
        oq_ref[:, s * LANES:(s + 1) * LANES] = (qs * ATT_HEAD_DIM ** -0.5).astype(oq_ref.dtype)
    kv = proj(W_KV0, 2 * ATT_KV)
    okv_ref[:, 0:ATT_KV] = _rope(kv[:, 0:ATT_KV], cos, sin, first_half).astype(okv_ref.dtype)
    okv_ref[:, ATT_KV:] = kv[:, ATT_KV:].astype(okv_ref.dtype)

    oml_ref[:, 0:ML_INNER] = (proj(W_ML0, ML_INNER) * ML_HEAD_DIM ** -0.5).astype(oml_ref.dtype)
    for s in range(1, 4):
        oml_ref[:, s * ML_INNER:(s + 1) * ML_INNER] = proj(W_ML0 + s * ML_INNER, ML_INNER).astype(oml_ref.dtype)

    oz_ref[...] = proj(W_Z0, SSD_INNER).astype(oz_ref.dtype)
    for s in range(SSD_XBC // 512):
        oxbc_ref[:, s * 512:(s + 1) * 512] = proj(W_XBC0 + s * 512, 512).astype(oxbc_ref.dtype)
    for s in range(3 * D_MODEL // 512):
        og_ref[:, s * 512:(s + 1) * 512] = proj(W_G0 + s * 512, 512).astype(og_ref.dtype)
    os_ref[...] = proj(W_S0, LANES)


def _in_projection(xs, mods, layer, g1, cos_t, sin_t, w_p, dims):
    n_tok = xs.shape[0]
    tm = ROW_TILE
    n_ctx_tiles = dims["ctx_rows"] // tm
    tiles_per_batch = dims["T"] // tm
    n_b = dims["B"]

    def mod_row(i):
        return jnp.where(i < n_ctx_tiles, n_b, (i - n_ctx_tiles) // tiles_per_batch)

    def rope_blk(i):
        return jnp.where(i < n_ctx_tiles, 0, 1 + (i - n_ctx_tiles) % tiles_per_batch)

    outs = [
        jax.ShapeDtypeStruct((n_tok, ATT_Q), BF16),
        jax.ShapeDtypeStruct((n_tok, 2 * ATT_KV), BF16),
        jax.ShapeDtypeStruct((n_tok, 4 * ML_INNER), BF16),
        jax.ShapeDtypeStruct((n_tok, SSD_INNER), F32),
        jax.ShapeDtypeStruct((n_tok, SSD_XBC), F32),
        jax.ShapeDtypeStruct((n_tok, 3 * D_MODEL), F32),
        jax.ShapeDtypeStruct((n_tok, LANES), F32),
    ]
    return pl.pallas_call(
        _inproj_kernel,
        out_shape=outs,
        grid=(n_tok // tm,),
        in_specs=[
            pl.BlockSpec((tm, D_MODEL), lambda i: (i, 0)),
            pl.BlockSpec((None, None, N_MOD, D_MODEL), lambda i: (layer, mod_row(i), 0, 0)),
            pl.BlockSpec((1, D_MODEL), lambda i: (0, 0)),
            pl.BlockSpec((tm, LANES), lambda i: (rope_blk(i), 0)),
            pl.BlockSpec((tm, LANES), lambda i: (rope_blk(i), 0)),
            _resident((D_MODEL, W_TOT)),
        ],
        out_specs=[pl.BlockSpec((tm, o.shape[1]), lambda i: (i, 0)) for o in outs],
        compiler_params=_cparams(("arbitrary",)),
        name="in_projection",
    )(xs, mods, g1, cos_t, sin_t, w_p)


def _attention_body(q_ref, kvl_ref, kvc_ref, sink_ref, o_ref, *, seq_len):
    q = q_ref[...]
    kv_ctx = kvc_ref[...]
    n_ctx = kv_ctx.shape[0]
    if kvl_ref is not None:
        j = pl.program_id(1)
        span = 3 * CHUNK
        start = jnp.clip((j - 1) * CHUNK, 0, seq_len - span)
        start = pl.multiple_of(start, CHUNK)
        kv = jnp.concatenate([kvl_ref[pl.ds(start, span), :], kv_ctx], axis=0)
        n_keys = span + n_ctx
        row = lax.broadcasted_iota(jnp.int32, (CHUNK, n_keys), 0)
        col = lax.broadcasted_iota(jnp.int32, (CHUNK, n_keys), 1)
        dist = (j * CHUNK - start) + row - col
        valid = (jnp.abs(dist) <= ATT_WINDOW) | (col >= span)
    else:
        kv = kv_ctx
        valid = None

    lane = lax.broadcasted_iota(jnp.int32, (1, LANES), 1)
    low = lane < ATT_HEAD_DIM
    sink = sink_ref[...]
    for g in range(ATT_KV_HEADS):
        k_g = kv[:, g * ATT_HEAD_DIM:(g + 1) * ATT_HEAD_DIM]
        v_g = kv[:, ATT_KV + g * ATT_HEAD_DIM:ATT_KV + (g + 1) * ATT_HEAD_DIM]
        kk = jnp.concatenate([k_g, k_g], axis=1)
        vv = jnp.concatenate([v_g, v_g], axis=1)
        for pair in range(ATT_GROUP // 2):
            c0 = (g * ATT_GROUP + 2 * pair) * ATT_HEAD_DIM
            qp = q[:, c0:c0 + LANES]
            outs = []
            for half in range(2):
                head = g * ATT_GROUP + 2 * pair + half
                qm = jnp.where(low if half == 0 else ~low, qp, jnp.zeros_like(qp))
                s = _dot_nt(qm, kk)
                if valid is not None:
                    s = jnp.where(valid, s, -jnp.inf)
                sk = sink[:, head:head + 1]
                m = jnp.maximum(jnp.max(s, axis=-1, keepdims=True), sk)
                p = jnp.exp(s - m)
                den = jnp.sum(p, axis=-1, keepdims=True) + jnp.exp(sk - m)
                outs.append(_dot(p.astype(BF16), vv) / den)
            o_ref[:, c0:c0 + LANES] = jnp.where(low, outs[0], outs[1]).astype(o_ref.dtype)


def _att_latent_kernel(q_ref, kvl_ref, kvc_ref, sink_ref, o_ref, *, seq_len):
    _attention_body(q_ref, kvl_ref, kvc_ref, sink_ref, o_ref, seq_len=seq_len)


def _att_context_kernel(q_ref, kvc_ref, sink_ref, latent_rows_ref, o_ref):
    del latent_rows_ref
    _attention_body(q_ref, None, kvc_ref, sink_ref, o_ref, seq_len=0)


def _attention(q, kv, sink_row, dims, need_ctx):
    n_tok = q.shape[0]
    n_b, seq, lc = dims["B"], dims["T"], dims["Lc"]
    nlt = seq // CHUNK
    nct = lc // CHUNK
    ctx_blocks = dims["ctx_rows"] // CHUNK
    assert dims["ctx_rows"] % seq == 0
    lat0 = dims["ctx_rows"] // seq
    out_shape = jax.ShapeDtypeStruct((n_tok, ATT_Q), BF16)
    sink_spec = pl.BlockSpec((1, LANES), lambda b, j: (0, 0))
    out_lat = pl.pallas_call(
        functools.partial(_att_latent_kernel, seq_len=seq),
        out_shape=out_shape,
        grid=(n_b, nlt),
        in_specs=[
            pl.BlockSpec((CHUNK, ATT_Q), lambda b, j: (ctx_blocks + b * nlt + j, 0)),
            pl.BlockSpec((seq, 2 * ATT_KV), lambda b, j: (lat0 + b, 0)),
            pl.BlockSpec((lc, 2 * ATT_KV), lambda b, j: (b, 0)),
            sink_spec,
        ],
        out_specs=pl.BlockSpec((CHUNK, ATT_Q), lambda b, j: (ctx_blocks + b * nlt + j, 0)),
        compiler_params=_cparams(("arbitrary", "arbitrary")),
        name="attention_latent",
    )(q, kv, kv, sink_row)
    if not need_ctx:
        return out_lat
    return pl.pallas_call(
        _att_context_kernel,
        out_shape=out_shape,
        grid=(n_b, nct),
        in_specs=[
            pl.BlockSpec((CHUNK, ATT_Q), lambda b, j: (b * nct + j, 0)),
            pl.BlockSpec((lc, 2 * ATT_KV), lambda b, j: (b, 0)),
            sink_spec,
            pl.BlockSpec(memory_space=pl.ANY),
        ],
        out_specs=pl.BlockSpec((CHUNK, ATT_Q), lambda b, j: (b * nct + j, 0)),
        input_output_aliases={3: 0},
        compiler_params=_cparams(("arbitrary", "arbitrary")),
        name="attention_context",
    )(q, kv, sink_row, out_lat)


def _chunk_pos(d, c, nct, nlt):
    is_ctx = c < nct
    ctx_i = jnp.where(d == 0, c, nct - 1 - c)
    lat_i = jnp.where(d == 0, c - nct, nlt - 1 - (c - nct))
    return is_ctx, ctx_i, lat_i


def _chunk_block(b, d, c, n_b, nct, nlt):
    is_ctx, ctx_i, lat_i = _chunk_pos(d, c, nct, nlt)
    return jnp.where(is_ctx, b * nct + ctx_i, n_b * nct + b * nlt + lat_i)


def _direction_masks(d):
    ri = lax.broadcasted_iota(jnp.int32, (CHUNK, CHUNK), 0)
    ci = lax.broadcasted_iota(jnp.int32, (CHUNK, CHUNK), 1)
    delta = (ci - ri) * (1 - 2 * d)
    return delta <= 0


def _mlstm_kernel(ml_ref, gs_ref, gb_ref, gain_ref, o_ref, ct_ref, n_ref, m_ref, hf_ref, *, nct, nlt):
    d = pl.program_id(1)
    c = pl.program_id(2)

    @pl.when(c == 0)
    def _():
        ct_ref[...] = jnp.zeros_like(ct_ref)
        n_ref[...] = jnp.zeros_like(n_ref)
        m_ref[...] = jnp.zeros_like(m_ref)

    is_ctx, ctx_i, lat_i = _chunk_pos(d, c, nct, nlt)
    pos = jnp.where(is_ctx, ctx_i, nct + lat_i)
    row0 = pl.multiple_of(pos * CHUNK, CHUNK)

    lane = lax.broadcasted_iota(jnp.int32, (1, LANES), 1)
    g = gs_ref[...]
    g = jnp.where(d == 0, g, pltpu.roll(g, LANES - 2 * ML_HEADS, 1)) + gb_ref[pl.ds(d, 1), :]
    is_f = (lane >= ML_HEADS) & (lane < 2 * ML_HEADS)
    g = jnp.where(is_f, jnp.minimum(g, 0.0) - jnp.log1p(jnp.exp(-jnp.abs(g))), g)

    causal = _direction_masks(d)
    tri = causal.astype(F32)
    bc = _dot_exact(tri, g)
    g_t = g.T
    bc_t = bc.T
    b_end_row = jnp.where(d == 0, bc[CHUNK - 1:CHUNK, :], bc[0:1, :])

    ml = ml_ref[...]
    hs = []
    for h in range(ML_HEADS):
        q = ml[:, h * ML_HEAD_DIM:(h + 1) * ML_HEAD_DIM]
        k = ml[:, ML_INNER + h * ML_HEAD_DIM:ML_INNER + (h + 1) * ML_HEAD_DIM]
        v = ml[:, 2 * ML_INNER + h * ML_HEAD_DIM:2 * ML_INNER + (h + 1) * ML_HEAD_DIM]
        i_col = g[:, h:h + 1]
        b_col = bc[:, ML_HEADS + h:ML_HEADS + h + 1]
        i_row = g_t[h:h + 1, :]
        b_row = bc_t[ML_HEADS + h:ML_HEADS + h + 1, :]
        b_end = b_end_row[:, ML_HEADS + h:ML_HEADS + h + 1]
        m_prev = m_ref[h][0:1, 0:1]
        ct = ct_ref[h]
        n_row = n_ref[h][0:1, :]

        log_d = jnp.where(causal, b_col - b_row + i_row, -jnp.inf)
        log_prev = b_col + m_prev
        m_t = jnp.maximum(jnp.max(log_d, axis=-1, keepdims=True), log_prev)
        dmat = jnp.exp(log_d - m_t)
        prev = jnp.exp(log_prev - m_t)
        s = _dot_nt(q, k) * dmat
        num = _dot(s.astype(BF16), v) + prev * _dot(q, ct.astype(BF16))
        qf = q.astype(F32)
        den = jnp.sum(s, axis=-1, keepdims=True) + prev * jnp.sum(qf * n_row, axis=-1, keepdims=True)
        hs.append(num / jnp.maximum(jnp.abs(den), jnp.exp(-m_t)))

        log_w = b_end - b_col + i_col
        m_new = jnp.maximum(b_end + m_prev, jnp.max(log_w, axis=0, keepdims=True))
        w = jnp.exp(log_w - m_new)
        decay = jnp.exp(b_end + m_prev - m_new)
        kf = k.astype(F32)
        wv = (w * v.astype(F32)).astype(BF16)
        ct_ref[h] = decay * ct + _dot(kf.T.astype(BF16), wv)
        n_ref[h] = jnp.broadcast_to(decay * n_row + jnp.sum(w * kf, axis=0, keepdims=True), n_ref.shape[1:])
        m_ref[h] = jnp.broadcast_to(m_new, m_ref.shape[1:])

    h_dir = jnp.concatenate(hs, axis=1)

    @pl.when(d == 0)
    def _():
        hf_ref[pl.ds(row0, CHUNK), :] = h_dir

    @pl.when(d == 1)
    def _():
        tot = hf_ref[pl.ds(row0, CHUNK), :] + h_dir
        gain = gain_ref[...]
        for h in range(ML_HEADS):
            sl = slice(h * ML_HEAD_DIM, (h + 1) * ML_HEAD_DIM)
            th = tot[:, sl]
            y = th * lax.rsqrt(jnp.mean(th * th, axis=-1, keepdims=True) + EPS) * gain[:, sl]
            o_gate = ml[:, 3 * ML_INNER + h * ML_HEAD_DIM:3 * ML_INNER + (h + 1) * ML_HEAD_DIM].astype(F32)
            o_ref[:, sl] = (y * _sigmoid(o_gate)).astype(o_ref.dtype)


def _mlstm(ml, small, gbias, gain_row, dims):
    n_tok = ml.shape[0]
    n_b = dims["B"]
    nct, nlt = dims["Lc"] // CHUNK, dims["T"] // CHUNK
    nc = nct + nlt

    def blk(b, d, c):
        return _chunk_block(b, d, c, n_b, nct, nlt)

    def out_blk(b, d, c):
        return jnp.where(d == 0, blk(b, 1, 0), blk(b, d, c))

    return pl.pallas_call(
        functools.partial(_mlstm_kernel, nct=nct, nlt=nlt),
        out_shape=jax.ShapeDtypeStruct((n_tok, ML_INNER), BF16),
        grid=(n_b, 2, nc),
        in_specs=[
            pl.BlockSpec((CHUNK, 4 * ML_INNER), lambda b, d, c: (blk(b, d, c), 0)),
            pl.BlockSpec((CHUNK, LANES), lambda b, d, c: (blk(b, d, c), 0)),
            pl.BlockSpec((8, LANES), lambda b, d, c: (0, 0)),
            pl.BlockSpec((1, ML_INNER), lambda b, d, c: (0, 0)),
        ],
        out_specs=pl.BlockSpec((CHUNK, ML_INNER), lambda b, d, c: (out_blk(b, d, c), 0)),
        scratch_shapes=[
            pltpu.VMEM((ML_HEADS, ML_HEAD_DIM, ML_HEAD_DIM), F32),
            pltpu.VMEM((ML_HEADS, 8, ML_HEAD_DIM), F32),
            pltpu.VMEM((ML_HEADS, 8, LANES), F32),
            pltpu.VMEM((nc * CHUNK, ML_INNER), F32),
        ],
        compiler_params=_cparams(("arbitrary", "arbitrary", "arbitrary")),
        name="mlstm",
    )(ml, small, gbias, gain_row)


def _ssd_kernel(xbc_ref, prev_ref, next_ref, z_ref, gs_ref, sb_ref, cw_ref, cb_ref, dsk_ref, gain_ref,
                o_ref, st_ref, xa_ref, yf_ref, ext_ref, *, nct, nlt):
    d = pl.program_id(1)
    c = pl.program_id(2)

    @pl.when(c == 0)
    def _():
        st_ref[...] = jnp.zeros_like(st_ref)

    is_ctx, ctx_i, lat_i = _chunk_pos(d, c, nct, nlt)
    pos = jnp.where(is_ctx, ctx_i, nct + lat_i)
    row0 = pl.multiple_of(pos * CHUNK, CHUNK)

    @pl.when(d == 0)
    def _():
        idx = jnp.where(is_ctx, ctx_i, lat_i)
        last = jnp.where(is_ctx, nct - 1, nlt - 1)
        has_prev = jnp.where(idx > 0, 1.0, 0.0)
        has_next = jnp.where(idx < last, 1.0, 0.0)
        ext_ref[0:HALO, :] = prev_ref[...].astype(F32) * has_prev
        ext_ref[HALO:HALO + CHUNK, :] = xbc_ref[...].astype(F32)
        ext_ref[HALO + CHUNK:, :] = next_ref[...].astype(F32) * has_next
        cw = cw_ref[...]
        acc = jnp.zeros((CHUNK, SSD_XBC), F32) + cb_ref[...]
        for tap in range(SSD_CONV):
            off = HALO - SSD_CONV // 2 + tap
            acc = acc + ext_ref[off:off + CHUNK, :] * cw[tap:tap + 1, :]
        xa_ref[pl.ds(row0, CHUNK), :] = acc * _sigmoid(acc)

    xa = xa_ref[pl.ds(row0, CHUNK), :]
    lane = lax.broadcasted_iota(jnp.int32, (1, LANES), 1)
    low = lane < SSD_HEAD_DIM

    gs = gs_ref[...]
    dt_raw = jnp.where(d == 0, pltpu.roll(gs, LANES - DT_LANE0, 1),
                       pltpu.roll(gs, LANES - DT_LANE0 - SSD_HEADS, 1))
    dt = _softplus(dt_raw + sb_ref[pl.ds(d, 1), :])
    la = dt * sb_ref[pl.ds(2 + d, 1), :]
    causal = _direction_masks(d)
    tri = causal.astype(F32)
    acs = _dot_exact(tri, la)
    acs_t = acs.T
    a_end_row = jnp.where(d == 0, acs[CHUNK - 1:CHUNK, :], acs[0:1, :])

    def pair_factor(cols, h0):
        return jnp.where(low, cols[:, h0:h0 + 1], cols[:, h0 + 1:h0 + 2])

    ys = []
    for g in range(SSD_GROUPS):
        bm = xa[:, SSD_INNER + g * SSD_STATE:SSD_INNER + (g + 1) * SSD_STATE].astype(BF16)
        cm = xa[:, SSD_INNER + (SSD_GROUPS + g) * SSD_STATE:SSD_INNER + (SSD_GROUPS + g + 1) * SSD_STATE].astype(BF16)
        cb = _dot_nt(cm, bm)
        st = st_ref[g]
        y_state = _dot(cm, st.astype(BF16))
        xw_parts = []
        y_parts = []
        decay_parts = []
        for pair in range(SSD_HPG // 2):
            h0 = g * SSD_HPG + 2 * pair
            x_pair = xa[:, h0 * SSD_HEAD_DIM:h0 * SSD_HEAD_DIM + LANES]
            xdt = x_pair * pair_factor(dt, h0)
            xdt_b = xdt.astype(BF16)
            y_pair = jnp.zeros((CHUNK, LANES), F32)
            for half in range(2):
                h = h0 + half
                seg = jnp.where(causal, acs[:, h:h + 1] - acs_t[h:h + 1, :], -jnp.inf)
                mmat = (cb * jnp.exp(seg)).astype(BF16)
                xm = jnp.where(low if half == 0 else ~low, xdt_b, jnp.zeros_like(xdt_b))
                y_pair = y_pair + _dot(mmat, xm)
            y_pair = y_pair + jnp.exp(pair_factor(acs, h0)) * y_state[:, pair * LANES:(pair + 1) * LANES]
            y_parts.append(y_pair)
            w_pair = jnp.exp(pair_factor(a_end_row, h0) - pair_factor(acs, h0))
            xw_parts.append((xdt * w_pair).astype(BF16))
            decay_parts.append(jnp.exp(pair_factor(a_end_row, h0)))
        xw = jnp.concatenate(xw_parts, axis=1)
        decay = jnp.concatenate(decay_parts, axis=1)
        bm_t = bm.astype(F32).T.astype(BF16)
        st_ref[g] = decay * st + _dot(bm_t, xw)
        ys.extend(y_parts)
    y_dir = jnp.concatenate(ys, axis=1)

    @pl.when(d == 0)
    def _():
        yf_ref[pl.ds(row0, CHUNK), :] = y_dir

    @pl.when(d == 1)
    def _():
        y = yf_ref[pl.ds(row0, CHUNK), :] + y_dir + dsk_ref[...] * xa[:, 0:SSD_INNER]
        z = z_ref[...].astype(F32)
        y = y * (z * _sigmoid(z))
        y = y * lax.rsqrt(jnp.mean(y * y, axis=-1, keepdims=True) + EPS) * gain_ref[...]
        o_ref[...] = y.astype(o_ref.dtype)


def _ssd(z, xbc, small, sbias, conv_w, conv_b, dskip_row, gain_row, dims):
    n_tok = z.shape[0]
    n_b = dims["B"]
    nct, nlt = dims["Lc"] // CHUNK, dims["T"] // CHUNK
    nc = nct + nlt
    sub = CHUNK // HALO
    n_halo_blocks = n_tok // HALO

    def blk(b, d, c):
        return _chunk_block(b, d, c, n_b, nct, nlt)

    def fwd_blk(b, d, c):
        return jnp.where(d == 0, blk(b, 0, c), blk(b, 0, nc - 1))

    def bwd_blk(b, d, c):
        return jnp.where(d == 0, blk(b, 1, 0), blk(b, 1, c))

    def prev_blk(b, d, c):
        return jnp.maximum(fwd_blk(b, d, c) * sub - 1, 0)

    def next_blk(b, d, c):
        return jnp.minimum((fwd_blk(b, d, c) + 1) * sub, n_halo_blocks - 1)

    const = lambda b, d, c: (0, 0)
    return pl.pallas_call(
        functools.partial(_ssd_kernel, nct=nct, nlt=nlt),
        out_shape=jax.ShapeDtypeStruct((n_tok, SSD_INNER), BF16),
        grid=(n_b, 2, nc),
        in_specs=[
            pl.BlockSpec((CHUNK, SSD_XBC), lambda b, d, c: (fwd_blk(b, d, c), 0)),
            pl.BlockSpec((HALO, SSD_XBC), lambda b, d, c: (prev_blk(b, d, c), 0)),
            pl.BlockSpec((HALO, SSD_XBC), lambda b, d, c: (next_blk(b, d, c), 0)),
            pl.BlockSpec((CHUNK, SSD_INNER), lambda b, d, c: (bwd_blk(b, d, c), 0)),
            pl.BlockSpec((CHUNK, LANES), lambda b, d, c: (blk(b, d, c), 0)),
            pl.BlockSpec((8, LANES), const),
            pl.BlockSpec((8, SSD_XBC), const),
            pl.BlockSpec((1, SSD_XBC), const),
            pl.BlockSpec((1, SSD_INNER), const),
            pl.BlockSpec((1, SSD_INNER), const),
        ],
        out_specs=pl.BlockSpec((CHUNK, SSD_INNER), lambda b, d, c: (bwd_blk(b, d, c), 0)),
        scratch_shapes=[
            pltpu.VMEM((SSD_GROUPS, SSD_STATE, SSD_HPG * SSD_HEAD_DIM), F32),
            pltpu.VMEM((nc * CHUNK, SSD_XBC), F32),
            pltpu.VMEM((nc * CHUNK, SSD_INNER), F32),
            pltpu.VMEM((CHUNK + 2 * HALO, SSD_XBC), F32),
        ],
        compiler_params=_cparams(("arbitrary", "arbitrary", "arbitrary")),
        name="ssd",
    )(xbc, xbc, xbc, z, small, sbias, conv_w, conv_b, dskip_row, gain_row)


FFN_CHUNK = 1408


def _merge_ffn_kernel(x_ref, att_ref, ml_ref, ss_ref, gt_ref, mod_ref, g2_ref, gf_ref,
                      wa_ref, wm_ref, ws_ref, wo_ref, wup_ref, wdn_ref, o_ref, *, final_norm):
    mod = mod_ref[...]
    y = None
    for i, (src, w) in enumerate(((att_ref, wa_ref), (ml_ref, wm_ref), (ss_ref, ws_ref))):
        gate = _sigmoid(gt_ref[:, i * D_MODEL:(i + 1) * D_MODEL].astype(F32))
        term = gate * _dot(src[...], w[...])
        y = term if y is None else y + term
    x1 = x_ref[...] + mod[2:3] * _dot(y.astype(BF16), wo_ref[...])
    hn = x1 * lax.rsqrt(jnp.mean(x1 * x1, axis=-1, keepdims=True) + EPS) * g2_ref[...]
    h = (hn * (1.0 + mod[4:5]) + mod[3:4]).astype(BF16)
    acc = None
    for c0 in range(0, D_FF, FFN_CHUNK):
        gate = _dot(h, wup_ref[:, c0:c0 + FFN_CHUNK])
        up = _dot(h, wup_ref[:, D_FF + c0:D_FF + c0 + FFN_CHUNK])
        act = (gate * _sigmoid(gate) * up).astype(BF16)
        part = _dot(act, wdn_ref[c0:c0 + FFN_CHUNK, :])
        acc = part if acc is None else acc + part
    x2 = x1 + mod[5:6] * acc
    if final_norm:
        x2 = x2 * lax.rsqrt(jnp.mean(x2 * x2, axis=-1, keepdims=True) + EPS) * gf_ref[...]
    o_ref[...] = x2


def _merge_ffn(xs, att, ml, ss, gates, mods, layer, g2, gf, wts, dims, last):
    tm = FFN_TILE
    n_ctx_tiles = dims["ctx_rows"] // tm
    tiles_per_batch = dims["T"] // tm
    n_b = dims["B"]
    n_tok = xs.shape[0]
    t0 = n_ctx_tiles if last else 0
    n_tiles = n_tok // tm - t0

    def mod_row(i):
        i = i + t0
        return jnp.where(i < n_ctx_tiles, n_b, (i - n_ctx_tiles) // tiles_per_batch)

    row = lambda i: (i + t0, 0)
    wa, wm, ws, wo, wup, wdn = wts
    return pl.pallas_call(
        functools.partial(_merge_ffn_kernel, final_norm=last),
        out_shape=jax.ShapeDtypeStruct((n_tiles * tm, D_MODEL), F32),
        grid=(n_tiles,),
        in_specs=[
            pl.BlockSpec((tm, D_MODEL), row),
            pl.BlockSpec((tm, ATT_Q), row),
            pl.BlockSpec((tm, ML_INNER), row),
            pl.BlockSpec((tm, SSD_INNER), row),
            pl.BlockSpec((tm, 3 * D_MODEL), row),
            pl.BlockSpec((None, None, N_MOD, D_MODEL), lambda i: (layer, mod_row(i), 0, 0)),
            pl.BlockSpec((1, D_MODEL), lambda i: (0, 0)),
            pl.BlockSpec((1, D_MODEL), lambda i: (0, 0)),
            _resident(wa.shape), _resident(wm.shape), _resident(ws.shape),
            _resident(wo.shape), _resident(wup.shape), _resident(wdn.shape),
        ],
        out_specs=pl.BlockSpec((tm, D_MODEL), lambda i: (i, 0)),
        compiler_params=_cparams(("arbitrary",)),
        name="merge_ffn",
    )(xs, att, ml, ss, gates, mods, g2, gf, wa, wm, ws, wo, wup, wdn)


def _rope_tables(seq, tm):
    pos = np.arange(seq)
    row = (pos // GRID_W).astype(np.float32)
    col = (pos % GRID_W).astype(np.float32)
    inv = (ROPE_BASE ** (-np.arange(ROPE_FREQS, dtype=np.float32) / ROPE_FREQS)).astype(np.float32)
    inv = jnp.asarray(inv)
    ang_r = jnp.asarray(row)[:, None] * inv
    ang_c = jnp.asarray(col)[:, None] * inv
    cos_h = jnp.concatenate([jnp.cos(ang_r), jnp.cos(ang_r), jnp.cos(ang_c), jnp.cos(ang_c)], axis=1)
    sin_h = jnp.concatenate([-jnp.sin(ang_r), jnp.sin(ang_r), -jnp.sin(ang_c), jnp.sin(ang_c)], axis=1)
    cos_t = jnp.concatenate([cos_h, cos_h], axis=1)
    sin_t = jnp.concatenate([sin_h, sin_h], axis=1)
    cos_t = jnp.concatenate([jnp.ones((tm, LANES), F32), cos_t], axis=0)
    sin_t = jnp.concatenate([jnp.zeros((tm, LANES), F32), sin_t], axis=0)
    return cos_t, sin_t


def _permute_w_in(w):
    d = w.shape[0]
    o_att = 0
    o_ml = ATT_Q + 2 * ATT_KV
    o_mlg = o_ml + 4 * ML_INNER
    o_z = o_mlg + 4 * ML_HEADS
    o_xbc = o_z + SSD_INNER
    o_dt = o_xbc + SSD_XBC
    o_g = o_dt + 2 * SSD_HEADS
    pad = jnp.zeros((d, LANES - 4 * ML_HEADS - 2 * SSD_HEADS), w.dtype)
    cols = [w[:, o_att:o_mlg], w[:, o_z:o_dt], w[:, o_g:], w[:, o_mlg:o_z], w[:, o_dt:o_g], pad]
    return jnp.concatenate(cols, axis=1).astype(BF16)


def _lane_row(vals):
    return jnp.zeros((1, LANES), F32).at[0, :vals.shape[0]].set(vals.astype(F32))


def kernel(x, c, ctx, c_ctx, w_mod, b_mod, g_norm1, w_in, att_sink, ml_i_bias, ml_f_bias, ml_head_gain,
           ssd_conv_w, ssd_conv_b, ssd_dt_bias, ssd_a_log, ssd_d, ssd_norm_gain,
           w_att_out, w_ml_out, w_ssd_out, w_o, g_norm2, w_up, w_down, g_final):
    n_b, seq, d_model = x.shape
    lc = ctx.shape[1]
    depth = w_mod.shape[0]
    dims = dict(B=n_b, T=seq, Lc=lc, ctx_rows=n_b * lc)
    assert d_model == D_MODEL and seq % ROW_TILE == 0 and (n_b * lc) % ROW_TILE == 0
    assert n_b + 1 <= MOD_ROWS and seq >= 3 * CHUNK and lc % CHUNK == 0

    cc = jnp.zeros((MOD_ROWS, d_model), F32).at[:n_b].set(c).at[n_b].set(c_ctx)
    mods = _modulation(cc, w_mod, b_mod).reshape(depth, MOD_ROWS, N_MOD, d_model)
    cos_t, sin_t = _rope_tables(seq, ROW_TILE)

    xs = jnp.concatenate([ctx.reshape(n_b * lc, d_model), x.reshape(n_b * seq, d_model)], axis=0)
    for l in range(depth):
        last = l == depth - 1
        w_p = _permute_w_in(w_in[l])
        q, kv, ml, z, xbc, gates, small = _in_projection(
            xs, mods, l, g_norm1[l].reshape(1, -1), cos_t, sin_t, w_p, dims)

        att = _attention(q, kv, _lane_row(att_sink[l]), dims, need_ctx=not last)

        gbias = jnp.zeros((8, LANES), F32)
        for dd in range(2):
            gbias = gbias.at[dd, 0:ML_HEADS].set(ml_i_bias[l, dd]).at[dd, ML_HEADS:2 * ML_HEADS].set(ml_f_bias[l, dd])
        mlo = _mlstm(ml, small, gbias, ml_head_gain[l].reshape(1, -1), dims)

        sbias = jnp.zeros((8, LANES), F32)
        a_neg = -jnp.exp(ssd_a_log[l].astype(F32))
        for dd in range(2):
            sbias = sbias.at[dd, 0:SSD_HEADS].set(ssd_dt_bias[l, dd]).at[2 + dd, 0:SSD_HEADS].set(a_neg[dd])
        conv_w = jnp.zeros((8, SSD_XBC), F32).at[:SSD_CONV].set(ssd_conv_w[l])
        sso = _ssd(z, xbc, small, sbias, conv_w, ssd_conv_b[l].reshape(1, -1),
                   jnp.repeat(ssd_d[l].astype(F32), SSD_HEAD_DIM).reshape(1, -1),
                   ssd_norm_gain[l].reshape(1, -1), dims)

        wts = tuple(w.astype(BF16) for w in (w_att_out[l], w_ml_out[l], w_ssd_out[l], w_o[l], w_up[l], w_down[l]))
        xs = _merge_ffn(xs, att, mlo, sso, gates, mods, l, g_norm2[l].reshape(1, -1),
                        g_final.reshape(1, -1), wts, dims, last)
    return xs.reshape(n_b, seq, d_model)
```

```python
import functools
import math

import numpy as np
import jax
import jax.numpy as jnp
from jax import lax
from jax.experimental import pallas as pl
from jax.experimental.pallas import tpu as pltpu

F32 = jnp.float32
BF16 = jnp.bfloat16

D_MODEL = 1024
EPS = 1e-6
N_MOD = 6
GRID_W = 64
ROPE_BASE = 10000.0

ATT_HEADS = 8
ATT_KV_HEADS = 2
ATT_GROUP = ATT_HEADS // ATT_KV_HEADS
ATT_HEAD_DIM = 64
ATT_WINDOW = 128
ATT_Q = ATT_HEADS * ATT_HEAD_DIM
ATT_KV = ATT_KV_HEADS * ATT_HEAD_DIM
ROPE_FREQS = ATT_HEAD_DIM // 4

ML_HEADS = 4
ML_HEAD_DIM = 128
ML_INNER = ML_HEADS * ML_HEAD_DIM

SSD_HEADS = 8
SSD_HEAD_DIM = 64
SSD_GROUPS = 2
SSD_HPG = SSD_HEADS // SSD_GROUPS
SSD_STATE = 128
SSD_CONV = 5
SSD_INNER = SSD_HEADS * SSD_HEAD_DIM
SSD_XBC = SSD_INNER + 2 * SSD_GROUPS * SSD_STATE

D_FF = -((-8 * D_MODEL) // (3 * 256)) * 256

CHUNK = 128
LANES = 128
HALO = 8
ROW_TILE = 512
FFN_TILE = 256
MOD_ROWS = 24
VMEM_LIMIT = 56 * 1024 * 1024

W_Q0 = 0
W_KV0 = ATT_Q
W_ML0 = W_KV0 + 2 * ATT_KV
W_Z0 = W_ML0 + 4 * ML_INNER
W_XBC0 = W_Z0 + SSD_INNER
W_G0 = W_XBC0 + SSD_XBC
W_S0 = W_G0 + 3 * D_MODEL
W_TOT = W_S0 + LANES
DT_LANE0 = 4 * ML_HEADS


def _cparams(sem):
    return pltpu.CompilerParams(dimension_semantics=sem, vmem_limit_bytes=VMEM_LIMIT)


def _resident(shape):
    nd = len(shape)
    return pl.BlockSpec(shape, lambda *_: (0,) * nd, pipeline_mode=pl.Buffered(1))


def _sigmoid(x):
    return 1.0 / (1.0 + jnp.exp(-x))


def _softplus(x):
    return jnp.maximum(x, 0.0) + jnp.log1p(jnp.exp(-jnp.abs(x)))


def _dot(a, b):
    return jnp.dot(a, b, preferred_element_type=F32)


def _dot_nt(a, b):
    return lax.dot_general(a, b, (((1,), (1,)), ((), ())), preferred_element_type=F32)


def _dot_exact(a, b):
    return jnp.dot(a, b, preferred_element_type=F32, precision=lax.Precision.HIGHEST)


def _mod_kernel(c_ref, w_ref, b_ref, o_ref):
    c = c_ref[...]
    a = (c * _sigmoid(c)).astype(BF16)
    o_ref[...] = _dot(a, w_ref[...].astype(BF16)) + b_ref[...]


def _modulation(cc, w_mod, b_mod):
    depth, d, n = w_mod.shape
    tn = 1536
    return pl.pallas_call(
        _mod_kernel,
        out_shape=jax.ShapeDtypeStruct((depth, MOD_ROWS, n), F32),
        grid=(depth, n // tn),
        in_specs=[
            pl.BlockSpec((MOD_ROWS, d), lambda l, j: (0, 0)),
            pl.BlockSpec((None, d, tn), lambda l, j: (l, 0, j)),
            pl.BlockSpec((None, 1, tn), lambda l, j: (l, 0, j)),
        ],
        out_specs=pl.BlockSpec((None, MOD_ROWS, tn), lambda l, j: (l, 0, j)),
        compiler_params=_cparams(("arbitrary", "arbitrary")),
        name="modulation",
    )(cc, w_mod, b_mod.reshape(depth, 1, n))


def _rope(x, cos, sin_signed, first_half):
    partner = jnp.where(first_half, pltpu.roll(x, LANES - ROPE_FREQS, 1), pltpu.roll(x, ROPE_FREQS, 1))
    return x * cos + partner * sin_signed


def _inproj_kernel(x_ref, mod_ref, g_ref, cos_ref, sin_ref, w_ref,
                   oq_ref, okv_ref, oml_ref, oz_ref, oxbc_ref, og_ref, os_ref):
    x = x_ref[...]
    y = x * lax.rsqrt(jnp.mean(x * x, axis=-1, keepdims=True) + EPS) * g_ref[...]
    mod = mod_ref[...]
    h = (y * (1.0 + mod[1:2]) + mod[0:1]).astype(BF16)

    cos = cos_ref[...]
    sin = sin_ref[...]
    lane = lax.broadcasted_iota(jnp.int32, (1, LANES), 1)
    first_half = (lane % (2 * ROPE_FREQS)) < ROPE_FREQS

    def proj(c0, width):
        return _dot(h, w_ref[:, c0:c0 + width])

    q = proj(W_Q0, ATT_Q)
    for s in range(ATT_Q // LANES):
        qs = _rope(q[:, s * LANES:(s + 1) * LANES], cos, sin, first_half)
        oq_ref[:, s * LANES:(s + 1) * LANES] = (qs * ATT_HEAD_DIM ** -0.5).astype(oq_ref.dtype)
    kv = proj(W_KV0, 2 * ATT_KV)
    okv_ref[:, 0:ATT_KV] = _rope(kv[:, 0:ATT_KV], cos, sin, first_half).astype(okv_ref.dtype)
    okv_ref[:, ATT_KV:] = kv[:, ATT_KV:].astype(okv_ref.dtype)

    oml_ref[:, 0:ML_INNER] = (proj(W_ML0, ML_INNER) * ML_HEAD_DIM ** -0.5).astype(oml_ref.dtype)
    for s in range(1, 4):
        oml_ref[:, s * ML_INNER:(s + 1) * ML_INNER] = proj(W_ML0 + s * ML_INNER, ML_INNER).astype(oml_ref.dtype)

    oz_ref[...] = proj(W_Z0, SSD_INNER).astype(oz_ref.dtype)
    for s in range(SSD_XBC // 512):
        oxbc_ref[:, s * 512:(s + 1) * 512] = proj(W_XBC0 + s * 512, 512).astype(oxbc_ref.dtype)
    for s in range(3 * D_MODEL // 512):
        og_ref[:, s * 512:(s + 1) * 512] = proj(W_G0 + s * 512, 512).astype(og_ref.dtype)
    os_ref[...] = proj(W_S0, LANES)


def _in_projection(xs, mods, layer, g1, cos_t, sin_t, w_p, dims):
    n_tok = xs.shape[0]
    tm = ROW_TILE
    n_ctx_tiles = dims["ctx_rows"] // tm
    tiles_per_batch = dims["T"] // tm
    n_b = dims["B"]

    def mod_row(i):
        return jnp.where(i < n_ctx_tiles, n_b, (i - n_ctx_tiles) // tiles_per_batch)

    def rope_blk(i):
        return jnp.where(i < n_ctx_tiles, 0, 1 + (i - n_ctx_tiles) % tiles_per_batch)

    outs = [
        jax.ShapeDtypeStruct((n_tok, ATT_Q), BF16),
        jax.ShapeDtypeStruct((n_tok, 2 * ATT_KV), BF16),
        jax.ShapeDtypeStruct((n_tok, 4 * ML_INNER), BF16),
        jax.ShapeDtypeStruct((n_tok, SSD_INNER), F32),
        jax.ShapeDtypeStruct((n_tok, SSD_XBC), F32),
        jax.ShapeDtypeStruct((n_tok, 3 * D_MODEL), F32),
        jax.ShapeDtypeStruct((n_tok, LANES), F32),
    ]
    return pl.pallas_call(
        _inproj_kernel,
        out_shape=outs,
        grid=(n_tok // tm,),
        in_specs=[
            pl.BlockSpec((tm, D_MODEL), lambda i: (i, 0)),
            pl.BlockSpec((None, None, N_MOD, D_MODEL), lambda i: (layer, mod_row(i), 0, 0)),
            pl.BlockSpec((1, D_MODEL), lambda i: (0, 0)),
            pl.BlockSpec((tm, LANES), lambda i: (rope_blk(i), 0)),
            pl.BlockSpec((tm, LANES), lambda i: (rope_blk(i), 0)),
            _resident((D_MODEL, W_TOT)),
        ],
        out_specs=[pl.BlockSpec((tm, o.shape[1]), lambda i: (i, 0)) for o in outs],
        compiler_params=_cparams(("arbitrary",)),
        name="in_projection",
    )(xs, mods, g1, cos_t, sin_t, w_p)


def _attention_body(q_ref, kvl_ref, kvc_ref, sink_ref, o_ref, *, seq_len):
    q = q_ref[...]
    kv_ctx = kvc_ref[...]
    n_ctx = kv_ctx.shape[0]
    if kvl_ref is not None:
        j = pl.program_id(1)
        span = 3 * CHUNK
        start = jnp.clip((j - 1) * CHUNK, 0, seq_len - span)
        start = pl.multiple_of(start, CHUNK)
        kv = jnp.concatenate([kvl_ref[pl.ds(start, span), :], kv_ctx], axis=0)
        n_keys = span + n_ctx
        row = lax.broadcasted_iota(jnp.int32, (CHUNK, n_keys), 0)
        col = lax.broadcasted_iota(jnp.int32, (CHUNK, n_keys), 1)
        dist = (j * CHUNK - start) + row - col
        valid = (jnp.abs(dist) <= ATT_WINDOW) | (col >= span)
    else:
        kv = kv_ctx
        valid = None

    lane = lax.broadcasted_iota(jnp.int32, (1, LANES), 1)
    low = lane < ATT_HEAD_DIM
    sink = sink_ref[...]
    ones = jnp.ones((kv.shape[0], LANES), BF16)
    zero = jnp.zeros((CHUNK, LANES), BF16)
    for g in range(ATT_KV_HEADS):
        k_g = kv[:, g * ATT_HEAD_DIM:(g + 1) * ATT_HEAD_DIM]
        v_g = kv[:, ATT_KV + g * ATT_HEAD_DIM:ATT_KV + (g + 1) * ATT_HEAD_DIM]
        kk = jnp.concatenate([k_g, k_g], axis=1)
        vw = jnp.concatenate([v_g, v_g, ones], axis=1)
        q_rows = []
        for pair in range(ATT_GROUP // 2):
            c0 = (g * ATT_GROUP + 2 * pair) * ATT_HEAD_DIM
            qp = q[:, c0:c0 + LANES]
            q_rows += [jnp.where(low, qp, zero), jnp.where(low, zero, qp)]
        s_all = _dot_nt(jnp.concatenate(q_rows, axis=0), kk)
        p_rows, m_rows, sk_rows = [], [], []
        for r in range(ATT_GROUP):
            s = s_all[r * CHUNK:(r + 1) * CHUNK]
            if valid is not None:
                s = jnp.where(valid, s, -jnp.inf)
            sk = sink[:, g * ATT_GROUP + r:g * ATT_GROUP + r + 1]
            m = jnp.maximum(jnp.max(s, axis=-1, keepdims=True), sk)
            p_rows.append(jnp.exp((s - m).astype(BF16)))
            m_rows.append(m)
            sk_rows.append(sk)
        o_all = _dot(jnp.concatenate(p_rows, axis=0), vw)
        outs = []
        for r in range(ATT_GROUP):
            o = o_all[r * CHUNK:(r + 1) * CHUNK]
            den = o[:, LANES:] + jnp.exp(sk_rows[r] - m_rows[r])
            outs.append(o[:, :LANES] / den)
        for pair in range(ATT_GROUP // 2):
            c0 = (g * ATT_GROUP + 2 * pair) * ATT_HEAD_DIM
            o_ref[:, c0:c0 + LANES] = jnp.where(low, outs[2 * pair], outs[2 * pair + 1]).astype(o_ref.dtype)


def _att_latent_kernel(q_ref, kvl_ref, kvc_ref, sink_ref, o_ref, *, seq_len):
    _attention_body(q_ref, kvl_ref, kvc_ref, sink_ref, o_ref, seq_len=seq_len)


def _att_context_kernel(q_ref, kvc_ref, sink_ref, latent_rows_ref, o_ref):
    del latent_rows_ref
    _attention_body(q_ref, None, kvc_ref, sink_ref, o_ref, seq_len=0)


def _attention(q, kv, sink_row, dims, need_ctx):
    n_tok = q.shape[0]
    n_b, seq, lc = dims["B"], dims["T"], dims["Lc"]
    nlt = seq // CHUNK
    nct = lc // CHUNK
    ctx_blocks = dims["ctx_rows"] // CHUNK
    assert dims["ctx_rows"] % seq == 0
    lat0 = dims["ctx_rows"] // seq
    out_shape = jax.ShapeDtypeStruct((n_tok, ATT_Q), BF16)
    sink_spec = pl.BlockSpec((1, LANES), lambda b, j: (0, 0))
    out_lat = pl.pallas_call(
        functools.partial(_att_latent_kernel, seq_len=seq),
        out_shape=out_shape,
        grid=(n_b, nlt),
        in_specs=[
            pl.BlockSpec((CHUNK, ATT_Q), lambda b, j: (ctx_blocks + b * nlt + j, 0)),
            pl.BlockSpec((seq, 2 * ATT_KV), lambda b, j: (lat0 + b, 0)),
            pl.BlockSpec((lc, 2 * ATT_KV), lambda b, j: (b, 0)),
            sink_spec,
        ],
        out_specs=pl.BlockSpec((CHUNK, ATT_Q), lambda b, j: (ctx_blocks + b * nlt + j, 0)),
        compiler_params=_cparams(("arbitrary", "arbitrary")),
        name="attention_latent",
    )(q, kv, kv, sink_row)
    if not need_ctx:
        return out_lat
    return pl.pallas_call(
        _att_context_kernel,
        out_shape=out_shape,
        grid=(n_b, nct),
        in_specs=[
            pl.BlockSpec((CHUNK, ATT_Q), lambda b, j: (b * nct + j, 0)),
            pl.BlockSpec((lc, 2 * ATT_KV), lambda b, j: (b, 0)),
            sink_spec,
            pl.BlockSpec(memory_space=pl.ANY),
        ],
        out_specs=pl.BlockSpec((CHUNK, ATT_Q), lambda b, j: (b * nct + j, 0)),
        input_output_aliases={3: 0},
        compiler_params=_cparams(("arbitrary", "arbitrary")),
        name="attention_context",
    )(q, kv, sink_row, out_lat)


def _chunk_pos(d, c, nct, nlt):
    is_ctx = c < nct
    ctx_i = jnp.where(d == 0, c, nct - 1 - c)
    lat_i = jnp.where(d == 0, c - nct, nlt - 1 - (c - nct))
    return is_ctx, ctx_i, lat_i


def _chunk_block(b, d, c, n_b, nct, nlt):
    is_ctx, ctx_i, lat_i = _chunk_pos(d, c, nct, nlt)
    return jnp.where(is_ctx, b * nct + ctx_i, n_b * nct + b * nlt + lat_i)


def _direction_masks(d):
    ri = lax.broadcasted_iota(jnp.int32, (CHUNK, CHUNK), 0)
    ci = lax.broadcasted_iota(jnp.int32, (CHUNK, CHUNK), 1)
    delta = (ci - ri) * (1 - 2 * d)
    return delta <= 0


def _mlstm_kernel(ml_ref, gs_ref, gb_ref, gain_ref, o_ref, ct_ref, n_ref, m_ref, hf_ref, *, nct, nlt):
    d = pl.program_id(1)
    c = pl.program_id(2)

    @pl.when(c == 0)
    def _():
        ct_ref[...] = jnp.zeros_like(ct_ref)
        n_ref[...] = jnp.zeros_like(n_ref)
        m_ref[...] = jnp.zeros_like(m_ref)

    is_ctx, ctx_i, lat_i = _chunk_pos(d, c, nct, nlt)
    pos = jnp.where(is_ctx, ctx_i, nct + lat_i)
    row0 = pl.multiple_of(pos * CHUNK, CHUNK)

    lane = lax.broadcasted_iota(jnp.int32, (1, LANES), 1)
    g = gs_ref[...]
    g = jnp.where(d == 0, g, pltpu.roll(g, LANES - 2 * ML_HEADS, 1)) + gb_ref[pl.ds(d, 1), :]
    is_f = (lane >= ML_HEADS) & (lane < 2 * ML_HEADS)
    g = jnp.where(is_f, jnp.minimum(g, 0.0) - jnp.log1p(jnp.exp(-jnp.abs(g))), g)

    causal = _direction_masks(d)
    tri = causal.astype(F32)
    bc = _dot_exact(tri, g)
    g_t = g.T
    bc_t = bc.T
    b_end_row = jnp.where(d == 0, bc[CHUNK - 1:CHUNK, :], bc[0:1, :])

    ml = ml_ref[...]
    hs = []
    for h in range(ML_HEADS):
        q = ml[:, h * ML_HEAD_DIM:(h + 1) * ML_HEAD_DIM]
        k = ml[:, ML_INNER + h * ML_HEAD_DIM:ML_INNER + (h + 1) * ML_HEAD_DIM]
        v = ml[:, 2 * ML_INNER + h * ML_HEAD_DIM:2 * ML_INNER + (h + 1) * ML_HEAD_DIM]
        i_col = g[:, h:h + 1]
        b_col = bc[:, ML_HEADS + h:ML_HEADS + h + 1]
        i_row = g_t[h:h + 1, :]
        b_row = bc_t[ML_HEADS + h:ML_HEADS + h + 1, :]
        b_end = b_end_row[:, ML_HEADS + h:ML_HEADS + h + 1]
        m_prev = m_ref[h][0:1, 0:1]
        ct = ct_ref[h]
        n_row = n_ref[h][0:1, :]

        log_d = jnp.where(causal, b_col - b_row + i_row, -jnp.inf)
        log_prev = b_col + m_prev
        m_t = jnp.maximum(jnp.max(log_d, axis=-1, keepdims=True), log_prev)
        dmat = jnp.exp(log_d - m_t)
        prev = jnp.exp(log_prev - m_t)
        s = _dot_nt(q, k) * dmat
        num = _dot(s.astype(BF16), v) + prev * _dot(q, ct.astype(BF16))
        qf = q.astype(F32)
        den = jnp.sum(s, axis=-1, keepdims=True) + prev * jnp.sum(qf * n_row, axis=-1, keepdims=True)
        hs.append(num / jnp.maximum(jnp.abs(den), jnp.exp(-m_t)))

        log_w = b_end - b_col + i_col
        m_new = jnp.maximum(b_end + m_prev, jnp.max(log_w, axis=0, keepdims=True))
        w = jnp.exp(log_w - m_new)
        decay = jnp.exp(b_end + m_prev - m_new)
        kf = k.astype(F32)
        wv = (w * v.astype(F32)).astype(BF16)
        ct_ref[h] = decay * ct + _dot(kf.T.astype(BF16), wv)
        n_ref[h] = jnp.broadcast_to(decay * n_row + jnp.sum(w * kf, axis=0, keepdims=True), n_ref.shape[1:])
        m_ref[h] = jnp.broadcast_to(m_new, m_ref.shape[1:])

    h_dir = jnp.concatenate(hs, axis=1)

    @pl.when(d == 0)
    def _():
        hf_ref[pl.ds(row0, CHUNK), :] = h_dir

    @pl.when(d == 1)
    def _():
        tot = hf_ref[pl.ds(row0, CHUNK), :] + h_dir
        gain = gain_ref[...]
        for h in range(ML_HEADS):
            sl = slice(h * ML_HEAD_DIM, (h + 1) * ML_HEAD_DIM)
            th = tot[:, sl]
            y = th * lax.rsqrt(jnp.mean(th * th, axis=-1, keepdims=True) + EPS) * gain[:, sl]
            o_gate = ml[:, 3 * ML_INNER + h * ML_HEAD_DIM:3 * ML_INNER + (h + 1) * ML_HEAD_DIM].astype(F32)
            o_ref[:, sl] = (y * _sigmoid(o_gate)).astype(o_ref.dtype)


def _mlstm(ml, small, gbias, gain_row, dims):
    n_tok = ml.shape[0]
    n_b = dims["B"]
    nct, nlt = dims["Lc"] // CHUNK, dims["T"] // CHUNK
    nc = nct + nlt

    def blk(b, d, c):
        return _chunk_block(b, d, c, n_b, nct, nlt)

    def out_blk(b, d, c):
        return jnp.where(d == 0, blk(b, 1, 0), blk(b, d, c))

    return pl.pallas_call(
        functools.partial(_mlstm_kernel, nct=nct, nlt=nlt),
        out_shape=jax.ShapeDtypeStruct((n_tok, ML_INNER), BF16),
        grid=(n_b, 2, nc),
        in_specs=[
            pl.BlockSpec((CHUNK, 4 * ML_INNER), lambda b, d, c: (blk(b, d, c), 0)),
            pl.BlockSpec((CHUNK, LANES), lambda b, d, c: (blk(b, d, c), 0)),
            pl.BlockSpec((8, LANES), lambda b, d, c: (0, 0)),
            pl.BlockSpec((1, ML_INNER), lambda b, d, c: (0, 0)),
        ],
        out_specs=pl.BlockSpec((CHUNK, ML_INNER), lambda b, d, c: (out_blk(b, d, c), 0)),
        scratch_shapes=[
            pltpu.VMEM((ML_HEADS, ML_HEAD_DIM, ML_HEAD_DIM), F32),
            pltpu.VMEM((ML_HEADS, 8, ML_HEAD_DIM), F32),
            pltpu.VMEM((ML_HEADS, 8, LANES), F32),
            pltpu.VMEM((nc * CHUNK, ML_INNER), F32),
        ],
        compiler_params=_cparams(("arbitrary", "arbitrary", "arbitrary")),
        name="mlstm",
    )(ml, small, gbias, gain_row)


def _ssd_kernel(xbc_ref, prev_ref, next_ref, z_ref, gs_ref, sb_ref, cw_ref, cb_ref, dsk_ref, gain_ref,
                o_ref, st_ref, xa_ref, yf_ref, ext_ref, *, nct, nlt):
    d = pl.program_id(1)
    c = pl.program_id(2)

    @pl.when(c == 0)
    def _():
        st_ref[...] = jnp.zeros_like(st_ref)

    is_ctx, ctx_i, lat_i = _chunk_pos(d, c, nct, nlt)
    pos = jnp.where(is_ctx, ctx_i, nct + lat_i)
    row0 = pl.multiple_of(pos * CHUNK, CHUNK)

    @pl.when(d == 0)
    def _():
        idx = jnp.where(is_ctx, ctx_i, lat_i)
        last = jnp.where(is_ctx, nct - 1, nlt - 1)
        has_prev = jnp.where(idx > 0, 1.0, 0.0)
        has_next = jnp.where(idx < last, 1.0, 0.0)
        ext_ref[0:HALO, :] = prev_ref[...].astype(F32) * has_prev
        ext_ref[HALO:HALO + CHUNK, :] = xbc_ref[...].astype(F32)
        ext_ref[HALO + CHUNK:, :] = next_ref[...].astype(F32) * has_next
        cw = cw_ref[...]
        acc = jnp.zeros((CHUNK, SSD_XBC), F32) + cb_ref[...]
        for tap in range(SSD_CONV):
            off = HALO - SSD_CONV // 2 + tap
            acc = acc + ext_ref[off:off + CHUNK, :] * cw[tap:tap + 1, :]
        xa_ref[pl.ds(row0, CHUNK), :] = acc * _sigmoid(acc)

    xa = xa_ref[pl.ds(row0, CHUNK), :]
    lane = lax.broadcasted_iota(jnp.int32, (1, LANES), 1)
    low = lane < SSD_HEAD_DIM

    gs = gs_ref[...]
    dt_raw = jnp.where(d == 0, pltpu.roll(gs, LANES - DT_LANE0, 1),
                       pltpu.roll(gs, LANES - DT_LANE0 - SSD_HEADS, 1))
    dt = _softplus(dt_raw + sb_ref[pl.ds(d, 1), :])
    la = dt * sb_ref[pl.ds(2 + d, 1), :]
    causal = _direction_masks(d)
    tri = causal.astype(F32)
    acs = _dot_exact(tri, la)
    acs_t = acs.T
    a_end_row = jnp.where(d == 0, acs[CHUNK - 1:CHUNK, :], acs[0:1, :])

    def pair_factor(cols, h0):
        return jnp.where(low, cols[:, h0:h0 + 1], cols[:, h0 + 1:h0 + 2])

    ys = []
    for g in range(SSD_GROUPS):
        bm = xa[:, SSD_INNER + g * SSD_STATE:SSD_INNER + (g + 1) * SSD_STATE].astype(BF16)
        cm = xa[:, SSD_INNER + (SSD_GROUPS + g) * SSD_STATE:SSD_INNER + (SSD_GROUPS + g + 1) * SSD_STATE].astype(BF16)
        cb = _dot_nt(cm, bm)
        st = st_ref[g]
        y_state = _dot(cm, st.astype(BF16))
        xw_parts = []
        y_parts = []
        decay_parts = []
        for pair in range(SSD_HPG // 2):
            h0 = g * SSD_HPG + 2 * pair
            x_pair = xa[:, h0 * SSD_HEAD_DIM:h0 * SSD_HEAD_DIM + LANES]
            xdt = x_pair * pair_factor(dt, h0)
            xdt_b = xdt.astype(BF16)
            y_pair = jnp.zeros((CHUNK, LANES), F32)
            for half in range(2):
                h = h0 + half
                seg = jnp.where(causal, acs[:, h:h + 1] - acs_t[h:h + 1, :], -jnp.inf)
                mmat = (cb * jnp.exp(seg)).astype(BF16)
                xm = jnp.where(low if half == 0 else ~low, xdt_b, jnp.zeros_like(xdt_b))
                y_pair = y_pair + _dot(mmat, xm)
            y_pair = y_pair + jnp.exp(pair_factor(acs, h0)) * y_state[:, pair * LANES:(pair + 1) * LANES]
            y_parts.append(y_pair)
            w_pair = jnp.exp(pair_factor(a_end_row, h0) - pair_factor(acs, h0))
            xw_parts.append((xdt * w_pair).astype(BF16))
            decay_parts.append(jnp.exp(pair_factor(a_end_row, h0)))
        xw = jnp.concatenate(xw_parts, axis=1)
        decay = jnp.concatenate(decay_parts, axis=1)
        bm_t = bm.astype(F32).T.astype(BF16)
        st_ref[g] = decay * st + _dot(bm_t, xw)
        ys.extend(y_parts)
    y_dir = jnp.concatenate(ys, axis=1)

    @pl.when(d == 0)
    def _():
        yf_ref[pl.ds(row0, CHUNK), :] = y_dir

    @pl.when(d == 1)
    def _():
        y = yf_ref[pl.ds(row0, CHUNK), :] + y_dir + dsk_ref[...] * xa[:, 0:SSD_INNER]
        z = z_ref[...].astype(F32)
        y = y * (z * _sigmoid(z))
        y = y * lax.rsqrt(jnp.mean(y * y, axis=-1, keepdims=True) + EPS) * gain_ref[...]
        o_ref[...] = y.astype(o_ref.dtype)


def _ssd(z, xbc, small, sbias, conv_w, conv_b, dskip_row, gain_row, dims):
    n_tok = z.shape[0]
    n_b = dims["B"]
    nct, nlt = dims["Lc"] // CHUNK, dims["T"] // CHUNK
    nc = nct + nlt
    sub = CHUNK // HALO
    n_halo_blocks = n_tok // HALO

    def blk(b, d, c):
        return _chunk_block(b, d, c, n_b, nct, nlt)

    def fwd_blk(b, d, c):
        return jnp.where(d == 0, blk(b, 0, c), blk(b, 0, nc - 1))

    def bwd_blk(b, d, c):
        return jnp.where(d == 0, blk(b, 1, 0), blk(b, 1, c))

    def prev_blk(b, d, c):
        return jnp.maximum(fwd_blk(b, d, c) * sub - 1, 0)

    def next_blk(b, d, c):
        return jnp.minimum((fwd_blk(b, d, c) + 1) * sub, n_halo_blocks - 1)

    const = lambda b, d, c: (0, 0)
    return pl.pallas_call(
        functools.partial(_ssd_kernel, nct=nct, nlt=nlt),
        out_shape=jax.ShapeDtypeStruct((n_tok, SSD_INNER), BF16),
        grid=(n_b, 2, nc),
        in_specs=[
            pl.BlockSpec((CHUNK, SSD_XBC), lambda b, d, c: (fwd_blk(b, d, c), 0)),
            pl.BlockSpec((HALO, SSD_XBC), lambda b, d, c: (prev_blk(b, d, c), 0)),
            pl.BlockSpec((HALO, SSD_XBC), lambda b, d, c: (next_blk(b, d, c), 0)),
            pl.BlockSpec((CHUNK, SSD_INNER), lambda b, d, c: (bwd_blk(b, d, c), 0)),
            pl.BlockSpec((CHUNK, LANES), lambda b, d, c: (blk(b, d, c), 0)),
            pl.BlockSpec((8, LANES), const),
            pl.BlockSpec((8, SSD_XBC), const),
            pl.BlockSpec((1, SSD_XBC), const),
            pl.BlockSpec((1, SSD_INNER), const),
            pl.BlockSpec((1, SSD_INNER), const),
        ],
        out_specs=pl.BlockSpec((CHUNK, SSD_INNER), lambda b, d, c: (bwd_blk(b, d, c), 0)),
        scratch_shapes=[
            pltpu.VMEM((SSD_GROUPS, SSD_STATE, SSD_HPG * SSD_HEAD_DIM), F32),
            pltpu.VMEM((nc * CHUNK, SSD_XBC), F32),
            pltpu.VMEM((nc * CHUNK, SSD_INNER), F32),
            pltpu.VMEM((CHUNK + 2 * HALO, SSD_XBC), F32),
        ],
        compiler_params=_cparams(("arbitrary", "arbitrary", "arbitrary")),
        name="ssd",
    )(xbc, xbc, xbc, z, small, sbias, conv_w, conv_b, dskip_row, gain_row)


FFN_CHUNK = 1408


def _merge_ffn_kernel(x_ref, att_ref, ml_ref, ss_ref, gt_ref, mod_ref, g2_ref, gf_ref,
                      wa_ref, wm_ref, ws_ref, wo_ref, wup_ref, wdn_ref, o_ref, *, final_norm):
    mod = mod_ref[...]
    y = None
    for i, (src, w) in enumerate(((att_ref, wa_ref), (ml_ref, wm_ref), (ss_ref, ws_ref))):
        gate = _sigmoid(gt_ref[:, i * D_MODEL:(i + 1) * D_MODEL].astype(F32))
        term = gate * _dot(src[...], w[...])
        y = term if y is None else y + term
    x1 = x_ref[...] + mod[2:3] * _dot(y.astype(BF16), wo_ref[...])
    hn = x1 * lax.rsqrt(jnp.mean(x1 * x1, axis=-1, keepdims=True) + EPS) * g2_ref[...]
    h = (hn * (1.0 + mod[4:5]) + mod[3:4]).astype(BF16)
    acc = None
    for c0 in range(0, D_FF, FFN_CHUNK):
        gate = _dot(h, wup_ref[:, c0:c0 + FFN_CHUNK])
        up = _dot(h, wup_ref[:, D_FF + c0:D_FF + c0 + FFN_CHUNK])
        act = (gate * _sigmoid(gate) * up).astype(BF16)
        part = _dot(act, wdn_ref[c0:c0 + FFN_CHUNK, :])
        acc = part if acc is None else acc + part
    x2 = x1 + mod[5:6] * acc
    if final_norm:
        x2 = x2 * lax.rsqrt(jnp.mean(x2 * x2, axis=-1, keepdims=True) + EPS) * gf_ref[...]
    o_ref[...] = x2


def _merge_ffn(xs, att, ml, ss, gates, mods, layer, g2, gf, wts, dims, last):
    tm = FFN_TILE
    n_ctx_tiles = dims["ctx_rows"] // tm
    tiles_per_batch = dims["T"] // tm
    n_b = dims["B"]
    n_tok = xs.shape[0]
    t0 = n_ctx_tiles if last else 0
    n_tiles = n_tok // tm - t0

    def mod_row(i):
        i = i + t0
        return jnp.where(i < n_ctx_tiles, n_b, (i - n_ctx_tiles) // tiles_per_batch)

    row = lambda i: (i + t0, 0)
    wa, wm, ws, wo, wup, wdn = wts
    return pl.pallas_call(
        functools.partial(_merge_ffn_kernel, final_norm=last),
        out_shape=jax.ShapeDtypeStruct((n_tiles * tm, D_MODEL), F32),
        grid=(n_tiles,),
        in_specs=[
            pl.BlockSpec((tm, D_MODEL), row),
            pl.BlockSpec((tm, ATT_Q), row),
            pl.BlockSpec((tm, ML_INNER), row),
            pl.BlockSpec((tm, SSD_INNER), row),
            pl.BlockSpec((tm, 3 * D_MODEL), row),
            pl.BlockSpec((None, None, N_MOD, D_MODEL), lambda i: (layer, mod_row(i), 0, 0)),
            pl.BlockSpec((1, D_MODEL), lambda i: (0, 0)),
            pl.BlockSpec((1, D_MODEL), lambda i: (0, 0)),
            _resident(wa.shape), _resident(wm.shape), _resident(ws.shape),
            _resident(wo.shape), _resident(wup.shape), _resident(wdn.shape),
        ],
        out_specs=pl.BlockSpec((tm, D_MODEL), lambda i: (i, 0)),
        compiler_params=_cparams(("arbitrary",)),
        name="merge_ffn",
    )(xs, att, ml, ss, gates, mods, g2, gf, wa, wm, ws, wo, wup, wdn)


def _rope_tables(seq, tm):
    pos = np.arange(seq)
    row = (pos // GRID_W).astype(np.float32)
    col = (pos % GRID_W).astype(np.float32)
    inv = (ROPE_BASE ** (-np.arange(ROPE_FREQS, dtype=np.float32) / ROPE_FREQS)).astype(np.float32)
    inv = jnp.asarray(inv)
    ang_r = jnp.asarray(row)[:, None] * inv
    ang_c = jnp.asarray(col)[:, None] * inv
    cos_h = jnp.concatenate([jnp.cos(ang_r), jnp.cos(ang_r), jnp.cos(ang_c), jnp.cos(ang_c)], axis=1)
    sin_h = jnp.concatenate([-jnp.sin(ang_r), jnp.sin(ang_r), -jnp.sin(ang_c), jnp.sin(ang_c)], axis=1)
    cos_t = jnp.concatenate([cos_h, cos_h], axis=1)
    sin_t = jnp.concatenate([sin_h, sin_h], axis=1)
    cos_t = jnp.concatenate([jnp.ones((tm, LANES), F32), cos_t], axis=0)
    sin_t = jnp.concatenate([jnp.zeros((tm, LANES), F32), sin_t], axis=0)
    return cos_t, sin_t


def _permute_w_in(w):
    d = w.shape[0]
    o_att = 0
    o_ml = ATT_Q + 2 * ATT_KV
    o_mlg = o_ml + 4 * ML_INNER
    o_z = o_mlg + 4 * ML_HEADS
    o_xbc = o_z + SSD_INNER
    o_dt = o_xbc + SSD_XBC
    o_g = o_dt + 2 * SSD_HEADS
    pad = jnp.zeros((d, LANES - 4 * ML_HEADS - 2 * SSD_HEADS), w.dtype)
    cols = [w[:, o_att:o_mlg], w[:, o_z:o_dt], w[:, o_g:], w[:, o_mlg:o_z], w[:, o_dt:o_g], pad]
    return jnp.concatenate(cols, axis=1).astype(BF16)


def _lane_row(vals):
    return jnp.zeros((1, LANES), F32).at[0, :vals.shape[0]].set(vals.astype(F32))


def kernel(x, c, ctx, c_ctx, w_mod, b_mod, g_norm1, w_in, att_sink, ml_i_bias, ml_f_bias, ml_head_gain,
           ssd_conv_w, ssd_conv_b, ssd_dt_bias, ssd_a_log, ssd_d, ssd_norm_gain,
           w_att_out, w_ml_out, w_ssd_out, w_o, g_norm2, w_up, w_down, g_final):
    n_b, seq, d_model = x.shape
    lc = ctx.shape[1]
    depth = w_mod.shape[0]
    dims = dict(B=n_b, T=seq, Lc=lc, ctx_rows=n_b * lc)
    assert d_model == D_MODEL and seq % ROW_TILE == 0 and (n_b * lc) % ROW_TILE == 0
    assert n_b + 1 <= MOD_ROWS and seq >= 3 * CHUNK and lc % CHUNK == 0

    cc = jnp.zeros((MOD_ROWS, d_model), F32).at[:n_b].set(c).at[n_b].set(c_ctx)
    mods = _modulation(cc, w_mod, b_mod).reshape(depth, MOD_ROWS, N_MOD, d_model)
    cos_t, sin_t = _rope_tables(seq, ROW_TILE)

    xs = jnp.concatenate([ctx.reshape(n_b * lc, d_model), x.reshape(n_b * seq, d_model)], axis=0)
    for l in range(depth):
        last = l == depth - 1
        w_p = _permute_w_in(w_in[l])
        q, kv, ml, z, xbc, gates, small = _in_projection(
            xs, mods, l, g_norm1[l].reshape(1, -1), cos_t, sin_t, w_p, dims)

        att = _attention(q, kv, _lane_row(att_sink[l]), dims, need_ctx=not last)

        gbias = jnp.zeros((8, LANES), F32)
        for dd in range(2):
            gbias = gbias.at[dd, 0:ML_HEADS].set(ml_i_bias[l, dd]).at[dd, ML_HEADS:2 * ML_HEADS].set(ml_f_bias[l, dd])
        mlo = _mlstm(ml, small, gbias, ml_head_gain[l].reshape(1, -1), dims)

        sbias = jnp.zeros((8, LANES), F32)
        a_neg = -jnp.exp(ssd_a_log[l].astype(F32))
        for dd in range(2):
            sbias = sbias.at[dd, 0:SSD_HEADS].set(ssd_dt_bias[l, dd]).at[2 + dd, 0:SSD_HEADS].set(a_neg[dd])
        conv_w = jnp.zeros((8, SSD_XBC), F32).at[:SSD_CONV].set(ssd_conv_w[l])
        sso = _ssd(z, xbc, small, sbias, conv_w, ssd_conv_b[l].reshape(1, -1),
                   jnp.repeat(ssd_d[l].astype(F32), SSD_HEAD_DIM).reshape(1, -1),
                   ssd_norm_gain[l].reshape(1, -1), dims)

        wts = tuple(w.astype(BF16) for w in (w_att_out[l], w_ml_out[l], w_ssd_out[l], w_o[l], w_up[l], w_down[l]))
        xs = _merge_ffn(xs, att, mlo, sso, gates, mods, l, g_norm2[l].reshape(1, -1),
                        g_final.reshape(1, -1), wts, dims, last)
    return xs.reshape(n_b, seq, d_model)
```

```python
import functools

import numpy as np
import jax
import jax.numpy as jnp
from jax import lax
from jax.experimental import pallas as pl
from jax.experimental.pallas import tpu as pltpu

F32 = jnp.float32
BF16 = jnp.bfloat16

D_MODEL = 1024
EPS = 1e-6
N_MOD = 6
GRID_W = 64
ROPE_BASE = 10000.0

ATT_HEADS = 8
ATT_KV_HEADS = 2
ATT_GROUP = ATT_HEADS // ATT_KV_HEADS
ATT_HEAD_DIM = 64
ATT_WINDOW = 128
ATT_Q = ATT_HEADS * ATT_HEAD_DIM
ATT_KV = ATT_KV_HEADS * ATT_HEAD_DIM
ROPE_FREQS = ATT_HEAD_DIM // 4

ML_HEADS = 4
ML_HEAD_DIM = 128
ML_INNER = ML_HEADS * ML_HEAD_DIM

SSD_HEADS = 8
SSD_HEAD_DIM = 64
SSD_GROUPS = 2
SSD_HPG = SSD_HEADS // SSD_GROUPS
SSD_STATE = 128
SSD_CONV = 5
SSD_INNER = SSD_HEADS * SSD_HEAD_DIM
SSD_XBC = SSD_INNER + 2 * SSD_GROUPS * SSD_STATE

D_FF = -((-8 * D_MODEL) // (3 * 256)) * 256
FFN_CHUNK = 1408

CHUNK = 128
LANES = 128
HALO = 8
PROJ_GROUPS = 4
FFN_GROUPS = 2
SCAN_BATCH = 2
MOD_ROWS = 24
VMEM_LIMIT = 56 * 1024 * 1024

W_Q0 = 0
W_KV0 = ATT_Q
W_ML0 = W_KV0 + 2 * ATT_KV
W_Z0 = W_ML0 + 4 * ML_INNER
W_XBC0 = W_Z0 + SSD_INNER
W_G0 = W_XBC0 + SSD_XBC
W_S0 = W_G0 + 3 * D_MODEL
W_TOT = W_S0 + LANES
DT_LANE0 = 4 * ML_HEADS


def _cparams(sem):
    return pltpu.CompilerParams(dimension_semantics=sem, vmem_limit_bytes=VMEM_LIMIT)


def _resident(shape):
    nd = len(shape)
    return pl.BlockSpec(shape, lambda *_: (0,) * nd, pipeline_mode=pl.Buffered(1))


def _sigmoid(x):
    return 1.0 / (1.0 + jnp.exp(-x))


def _softplus(x):
    return jnp.maximum(x, 0.0) + jnp.log1p(jnp.exp(-jnp.abs(x)))


def _dot(a, b):
    return jnp.dot(a, b, preferred_element_type=F32)


def _dot_nt(a, b):
    return lax.dot_general(a, b, (((1,), (1,)), ((), ())), preferred_element_type=F32)


def _dot_exact(a, b):
    return jnp.dot(a, b, preferred_element_type=F32, precision=lax.Precision.HIGHEST)


def _rmsnorm(x, gain):
    return x * lax.rsqrt(jnp.mean(x * x, axis=-1, keepdims=True) + EPS) * gain


def _mod_kernel(c_ref, w_ref, b_ref, o_ref):
    c = c_ref[...]
    a = (c * _sigmoid(c)).astype(BF16)
    o_ref[...] = _dot(a, w_ref[...].astype(BF16)) + b_ref[...]


def _modulation(cc, w_mod, b_mod):
    depth, d, n = w_mod.shape
    tn = 1536
    return pl.pallas_call(
        _mod_kernel,
        out_shape=jax.ShapeDtypeStruct((depth, MOD_ROWS, n), F32),
        grid=(depth, n // tn),
        in_specs=[
            pl.BlockSpec((MOD_ROWS, d), lambda l, j: (0, 0)),
            pl.BlockSpec((None, d, tn), lambda l, j: (l, 0, j)),
            pl.BlockSpec((None, 1, tn), lambda l, j: (l, 0, j)),
        ],
        out_specs=pl.BlockSpec((None, MOD_ROWS, tn), lambda l, j: (l, 0, j)),
        compiler_params=_cparams(("arbitrary", "arbitrary")),
        name="modulation",
    )(cc, w_mod, b_mod.reshape(depth, 1, n))


def _stream_specs(first, groups, dims):
    n_b, nct, nlt = dims["B"], dims["nct"], dims["nlt"]
    per_chunk = n_b // groups
    if not first:
        return [pl.BlockSpec((groups, CHUNK, D_MODEL), lambda i: (i, 0, 0))]

    def ctx_idx(i):
        p = jnp.minimum(i // per_chunk, nct - 1)
        return (jnp.where(i < nct * per_chunk, i % per_chunk, per_chunk - 1), p, 0, 0)

    def lat_idx(i):
        p = jnp.maximum(i // per_chunk - nct, 0)
        return (jnp.where(i < nct * per_chunk, 0, i % per_chunk), p, 0, 0)

    return [pl.BlockSpec((groups, None, CHUNK, D_MODEL), ctx_idx),
            pl.BlockSpec((groups, None, CHUNK, D_MODEL), lat_idx)]


def _mod_spec(layer, groups, dims, tile0=0):
    n_b, nct = dims["B"], dims["nct"]
    per_chunk = n_b // groups

    def idx(i):
        i = i + tile0
        return (layer, jnp.where(i < nct * per_chunk, per_chunk, i % per_chunk), 0, 0)

    return pl.BlockSpec((None, groups, N_MOD, D_MODEL), idx)


def _read_stream(refs, k, is_ctx_tile):
    if len(refs) == 1:
        return refs[0][k]
    return jnp.where(is_ctx_tile, refs[0][k], refs[1][k])


def _rope(x, cos, sin_signed, first_half):
    partner = jnp.where(first_half, pltpu.roll(x, LANES - ROPE_FREQS, 1), pltpu.roll(x, ROPE_FREQS, 1))
    return x * cos + partner * sin_signed


def _inproj_kernel(*refs, n_src, ctx_tiles):
    x_refs = refs[:n_src]
    (mod_ref, g_ref, cos_ref, sin_ref, w_ref,
     oq_ref, okv_ref, oml_ref, oz_ref, oxbc_ref, og_ref, os_ref, h_ref) = refs[n_src:]
    is_ctx_tile = pl.program_id(0) < ctx_tiles
    gain = g_ref[...]
    for k in range(PROJ_GROUPS):
        mod = mod_ref[k]
        y = _rmsnorm(_read_stream(x_refs, k, is_ctx_tile), gain)
        h_ref[k * CHUNK:(k + 1) * CHUNK, :] = (y * (1.0 + mod[1:2]) + mod[0:1]).astype(BF16)
    h = h_ref[...]

    cos = jnp.concatenate([cos_ref[...]] * PROJ_GROUPS, axis=0)
    sin = jnp.concatenate([sin_ref[...]] * PROJ_GROUPS, axis=0)
    lane = lax.broadcasted_iota(jnp.int32, (1, LANES), 1)
    first_half = (lane % (2 * ROPE_FREQS)) < ROPE_FREQS

    def proj(c0, width):
        return _dot(h, w_ref[:, c0:c0 + width])

    q = proj(W_Q0, ATT_Q)
    for s in range(ATT_Q // LANES):
        qs = _rope(q[:, s * LANES:(s + 1) * LANES], cos, sin, first_half)
        oq_ref[:, s * LANES:(s + 1) * LANES] = (qs * ATT_HEAD_DIM ** -0.5).astype(oq_ref.dtype)
    kv = proj(W_KV0, 2 * ATT_KV)
    okv_ref[:, 0:ATT_KV] = _rope(kv[:, 0:ATT_KV], cos, sin, first_half).astype(okv_ref.dtype)
    okv_ref[:, ATT_KV:] = kv[:, ATT_KV:].astype(okv_ref.dtype)

    oml_ref[:, 0:ML_INNER] = (proj(W_ML0, ML_INNER) * ML_HEAD_DIM ** -0.5).astype(oml_ref.dtype)
    kproj = proj(W_ML0 + ML_INNER, ML_INNER)
    for r in range(PROJ_GROUPS):
        for hd in range(ML_HEADS):
            blk = kproj[r * CHUNK:(r + 1) * CHUNK, hd * ML_HEAD_DIM:(hd + 1) * ML_HEAD_DIM]
            oml_ref[r * CHUNK:(r + 1) * CHUNK,
                    ML_INNER + hd * ML_HEAD_DIM:ML_INNER + (hd + 1) * ML_HEAD_DIM] = blk.T.astype(oml_ref.dtype)
    for s in range(2, 4):
        oml_ref[:, s * ML_INNER:(s + 1) * ML_INNER] = proj(W_ML0 + s * ML_INNER, ML_INNER).astype(oml_ref.dtype)

    oz_ref[...] = proj(W_Z0, SSD_INNER).astype(oz_ref.dtype)
    for s in range(SSD_XBC // 512):
        oxbc_ref[:, s * 512:(s + 1) * 512] = proj(W_XBC0 + s * 512, 512).astype(oxbc_ref.dtype)
    for s in range(3 * D_MODEL // 512):
        og_ref[:, s * 512:(s + 1) * 512] = proj(W_G0 + s * 512, 512).astype(og_ref.dtype)
    os_ref[...] = proj(W_S0, LANES)


def _in_projection(stream, first, mods, layer, g1, cos_t, sin_t, w_p, dims):
    n_b, nct = dims["B"], dims["nct"]
    n_tok = dims["n_tok"]
    tm = PROJ_GROUPS * CHUNK
    per_chunk = n_b // PROJ_GROUPS
    outs = [
        jax.ShapeDtypeStruct((n_tok, ATT_Q), BF16),
        jax.ShapeDtypeStruct((n_tok, 2 * ATT_KV), BF16),
        jax.ShapeDtypeStruct((n_tok, 4 * ML_INNER), BF16),
        jax.ShapeDtypeStruct((n_tok, SSD_INNER), F32),
        jax.ShapeDtypeStruct((n_tok, SSD_XBC), F32),
        jax.ShapeDtypeStruct((n_tok, 3 * D_MODEL), F32),
        jax.ShapeDtypeStruct((n_tok, LANES), F32),
    ]
    return pl.pallas_call(
        functools.partial(_inproj_kernel, n_src=len(stream), ctx_tiles=nct * per_chunk),
        out_shape=outs,
        grid=(n_tok // tm,),
        in_specs=_stream_specs(first, PROJ_GROUPS, dims) + [
            _mod_spec(layer, PROJ_GROUPS, dims),
            pl.BlockSpec((1, D_MODEL), lambda i: (0, 0)),
            pl.BlockSpec((CHUNK, LANES), lambda i: (i // per_chunk, 0)),
            pl.BlockSpec((CHUNK, LANES), lambda i: (i // per_chunk, 0)),
            _resident((D_MODEL, W_TOT)),
        ],
        out_specs=[pl.BlockSpec((tm, o.shape[1]), lambda i: (i, 0)) for o in outs],
        scratch_shapes=[pltpu.VMEM((tm, D_MODEL), BF16)],
        compiler_params=_cparams(("arbitrary",)),
        name="in_projection",
    )(*stream, mods, g1, cos_t, sin_t, w_p)


def _attention_block(q, kv, valid, sink, o_ref):
    lane = lax.broadcasted_iota(jnp.int32, (1, LANES), 1)
    low = lane < ATT_HEAD_DIM
    ones = jnp.ones((kv.shape[0], LANES), BF16)
    zero = jnp.zeros((CHUNK, LANES), BF16)
    for g in range(ATT_KV_HEADS):
        k_g = kv[:, g * ATT_HEAD_DIM:(g + 1) * ATT_HEAD_DIM]
        v_g = kv[:, ATT_KV + g * ATT_HEAD_DIM:ATT_KV + (g + 1) * ATT_HEAD_DIM]
        kk = jnp.concatenate([k_g, k_g], axis=1)
        vw = jnp.concatenate([v_g, v_g, ones], axis=1)
        q_rows = []
        for pair in range(ATT_GROUP // 2):
            c0 = (g * ATT_GROUP + 2 * pair) * ATT_HEAD_DIM
            qp = q[:, c0:c0 + LANES]
            q_rows += [jnp.where(low, qp, zero), jnp.where(low, zero, qp)]
        s_all = _dot_nt(jnp.concatenate(q_rows, axis=0), kk)
        p_rows, m_rows, sk_rows = [], [], []
        for r in range(ATT_GROUP):
            s = s_all[r * CHUNK:(r + 1) * CHUNK]
            if valid is not None:
                s = jnp.where(valid, s, -jnp.inf)
            sk = sink[:, g * ATT_GROUP + r:g * ATT_GROUP + r + 1]
            m = jnp.maximum(jnp.max(s, axis=-1, keepdims=True), sk)
            p_rows.append(jnp.exp((s - m).astype(BF16)))
            m_rows.append(m)
            sk_rows.append(sk)
        o_all = _dot(jnp.concatenate(p_rows, axis=0), vw)
        outs = []
        for r in range(ATT_GROUP):
            o = o_all[r * CHUNK:(r + 1) * CHUNK]
            den = o[:, LANES:] + jnp.exp(sk_rows[r] - m_rows[r])
            outs.append(o[:, :LANES] / den)
        for pair in range(ATT_GROUP // 2):
            c0 = (g * ATT_GROUP + 2 * pair) * ATT_HEAD_DIM
            o_ref[:, c0:c0 + LANES] = jnp.where(low, outs[2 * pair], outs[2 * pair + 1]).astype(o_ref.dtype)


def _attention_kernel(*refs, nct, nlt, ctx_queries):
    q_ref = refs[0]
    loc_refs = refs[1:4]
    ctx_refs = refs[4:4 + nct]
    sink_ref, o_ref = refs[4 + nct:]
    step = pl.program_id(1)
    sink = sink_ref[...]
    kv_ctx = [r[...] for r in ctx_refs]

    def latent(j):
        span = 3 * CHUNK
        first = jnp.clip(j - 1, 0, nlt - 3)
        kv = jnp.concatenate([r[...] for r in loc_refs] + kv_ctx, axis=0)
        n_keys = kv.shape[0]
        row = lax.broadcasted_iota(jnp.int32, (CHUNK, n_keys), 0)
        col = lax.broadcasted_iota(jnp.int32, (CHUNK, n_keys), 1)
        dist = (j - first) * CHUNK + row - col
        valid = (jnp.abs(dist) <= ATT_WINDOW) | (col >= span)
        _attention_block(q_ref[...], kv, valid, sink, o_ref)

    if not ctx_queries:
        latent(step)
        return

    @pl.when(step < nct)
    def _():
        _attention_block(q_ref[...], jnp.concatenate(kv_ctx, axis=0), None, sink, o_ref)

    @pl.when(step >= nct)
    def _():
        latent(step - nct)


def _attention(q, kv, sink_row, dims, ctx_queries):
    n_b, nct, nlt = dims["B"], dims["nct"], dims["nlt"]
    c0 = 0 if ctx_queries else nct
    n_steps = nct + nlt - c0

    def win(i):
        def idx(b, s):
            first = jnp.clip(s + c0 - nct - 1, 0, nlt - 3)
            return ((nct + first + i) * n_b + b, 0)
        return idx

    return pl.pallas_call(
        functools.partial(_attention_kernel, nct=nct, nlt=nlt, ctx_queries=ctx_queries),
        out_shape=jax.ShapeDtypeStruct((n_steps * n_b * CHUNK, ATT_Q), BF16),
        grid=(n_b, n_steps),
        in_specs=[pl.BlockSpec((CHUNK, ATT_Q), lambda b, s: ((s + c0) * n_b + b, 0))]
        + [pl.BlockSpec((CHUNK, 2 * ATT_KV), win(i)) for i in range(3)]
        + [pl.BlockSpec((CHUNK, 2 * ATT_KV), functools.partial(lambda b, s, i: (i * n_b + b, 0), i=i))
           for i in range(nct)]
        + [pl.BlockSpec((1, LANES), lambda b, s: (0, 0))],
        out_specs=pl.BlockSpec((CHUNK, ATT_Q), lambda b, s: (s * n_b + b, 0)),
        compiler_params=_cparams(("arbitrary", "arbitrary")),
        name="attention",
    )(q, *([kv] * (3 + nct)), sink_row)


def _scan_chunk(d, c, nct, nlt):
    fwd = c
    bwd = jnp.where(c < nct, nct - 1 - c, nct + nlt - 1 - (c - nct))
    return jnp.where(d == 0, fwd, bwd)


def _direction_masks(d):
    ri = lax.broadcasted_iota(jnp.int32, (CHUNK, CHUNK), 0)
    ci = lax.broadcasted_iota(jnp.int32, (CHUNK, CHUNK), 1)
    delta = (ci - ri) * (1 - 2 * d)
    return delta <= 0, jnp.where(delta >= 0, 1.0, 0.0).astype(BF16)


def _cumsum_lanes(tiles, tri_t):
    parts = []
    for x8 in tiles:
        hi = x8.astype(BF16).astype(F32)
        r1 = x8 - hi
        mid = r1.astype(BF16).astype(F32)
        parts += [hi, mid, r1 - mid]
    if len(parts) % 2:
        parts.append(jnp.zeros_like(parts[0]))
    out = _dot(jnp.concatenate(parts, axis=0).astype(BF16), tri_t)
    return [out[24 * j:24 * j + 8] + out[24 * j + 8:24 * j + 16] + out[24 * j + 16:24 * j + 24]
            for j in range(len(tiles))]


def _cummax_lanes(x8, backward):
    lane = lax.broadcasted_iota(jnp.int32, (1, LANES), 1)
    k = 1
    while k < LANES:
        if backward:
            shifted = jnp.where(lane < LANES - k, pltpu.roll(x8, LANES - k, 1), -jnp.inf)
        else:
            shifted = jnp.where(lane >= k, pltpu.roll(x8, k, 1), -jnp.inf)
        x8 = jnp.maximum(x8, shifted)
        k *= 2
    return x8


def _rows_to_columns(x8):
    pad = jnp.zeros((LANES - 8, LANES), F32)
    return jnp.concatenate([x8, pad], axis=0).T


def _lane_fill(cols, j):
    return jnp.broadcast_to(cols[:, j:j + 1], cols.shape)


ML_ROWS = 16
SSD_ROWS = 32


def _gate_prep_kernel(gs_ref, gb_ref, sb_ref, mr_ref, mc_ref, sr_ref, sc_ref):
    sub8 = lax.broadcasted_iota(jnp.int32, (8, 1), 0)
    head_rows = sub8 < ML_HEADS
    masks = [_direction_masks(dd)[1] for dd in range(2)]
    for r in range(gs_ref.shape[0] // CHUNK):
        rows = slice(r * CHUNK, (r + 1) * CHUNK)
        gt = gs_ref[rows, :].T
        for dd in range(2):
            g8 = gt[8 * dd:8 * dd + 8] + gb_ref[dd]
            g8 = jnp.where(head_rows, g8, jnp.minimum(g8, 0.0) - jnp.log1p(jnp.exp(-jnp.abs(g8))))
            dt8 = _softplus(gt[DT_LANE0 + 8 * dd:DT_LANE0 + 8 * dd + 8] + sb_ref[dd])
            la8 = dt8 * sb_ref[2 + dd]
            gsum, acs8 = _cumsum_lanes([g8, la8], masks[dd])

            b8 = pltpu.roll(gsum, ML_HEADS, 0)
            c8 = g8 - b8
            cm8 = _cummax_lanes(c8, backward=dd == 1)
            cm_end = jnp.broadcast_to(jnp.max(cm8, axis=1, keepdims=True), cm8.shape)
            b_end = jnp.broadcast_to(jnp.sum(g8, axis=1, keepdims=True), g8.shape)
            mr_ref[dd, r, 0:8] = jnp.where(head_rows, c8, 0.0)
            mr_ref[dd, r, 8:16] = jnp.where(head_rows, cm_end, b_end)
            mc_ref[dd, rows, :] = _rows_to_columns(jnp.where(head_rows, cm8, gsum))

            a_end = jnp.sum(la8, axis=1, keepdims=True)
            sr_ref[dd, r, 0:8] = dt8
            sr_ref[dd, r, 8:16] = acs8
            sr_ref[dd, r, 16:24] = dt8 * jnp.exp(a_end - acs8)
            sr_ref[dd, r, 24:32] = jnp.broadcast_to(jnp.exp(a_end), dt8.shape)
            sc_ref[dd, rows, :] = _rows_to_columns(acs8)


def _gate_prep(small, gbias, sbias, dims):
    n_tok = dims["n_tok"]
    groups = PROJ_GROUPS
    n_groups = n_tok // CHUNK
    tm = groups * CHUNK
    outs = [
        jax.ShapeDtypeStruct((2, n_groups, ML_ROWS, LANES), F32),
        jax.ShapeDtypeStruct((2, n_tok, LANES), F32),
        jax.ShapeDtypeStruct((2, n_groups, SSD_ROWS, LANES), F32),
        jax.ShapeDtypeStruct((2, n_tok, LANES), F32),
    ]
    return pl.pallas_call(
        _gate_prep_kernel,
        out_shape=outs,
        grid=(n_tok // tm,),
        in_specs=[
            pl.BlockSpec((tm, LANES), lambda i: (i, 0)),
            pl.BlockSpec((2, 8, LANES), lambda i: (0, 0, 0)),
            pl.BlockSpec((4, SSD_HEADS, LANES), lambda i: (0, 0, 0)),
        ],
        out_specs=[
            pl.BlockSpec((2, groups, ML_ROWS, LANES), lambda i: (0, i, 0, 0)),
            pl.BlockSpec((2, tm, LANES), lambda i: (0, i, 0)),
            pl.BlockSpec((2, groups, SSD_ROWS, LANES), lambda i: (0, i, 0, 0)),
            pl.BlockSpec((2, tm, LANES), lambda i: (0, i, 0)),
        ],
        compiler_params=_cparams(("arbitrary",)),
        name="gate_prep",
    )(small, gbias, sbias)


def _mlstm_kernel(ml_ref, mr_ref, mc_ref, gain_ref, o_ref, ct_ref, m_ref, hf_ref, *, nct, nlt):
    d = pl.program_id(1)
    c = pl.program_id(2)

    @pl.when(c == 0)
    def _():
        ct_ref[...] = jnp.zeros_like(ct_ref)
        m_ref[...] = jnp.zeros_like(m_ref)

    p = _scan_chunk(d, c, nct, nlt)
    sub8 = lax.broadcasted_iota(jnp.int32, (8, 1), 0)
    causal, _ = _direction_masks(d)
    ones = jnp.ones((CHUNK, ML_HEAD_DIM), BF16)
    gain = gain_ref[...]

    h_dirs = []
    for i in range(SCAN_BATCH):
        rows = slice(i * CHUNK, (i + 1) * CHUNK)
        c8 = mr_ref[i, 0:8]
        m8 = m_ref[i]
        ends = mr_ref[i, 8:16]
        mx_end = jnp.maximum(ends, m8)
        w8 = jnp.exp(c8 - mx_end)
        dec8 = jnp.exp(m8 - mx_end)
        m_ref[i] = jnp.where(sub8 < ML_HEADS, pltpu.roll(ends, ML_HEADS, 0) + mx_end, 0.0)
        cols = mc_ref[rows, :]

        ml = ml_ref[rows, :]
        hs = []
        for h in range(ML_HEADS):
            st = i * ML_HEADS + h
            q = ml[:, h * ML_HEAD_DIM:(h + 1) * ML_HEAD_DIM]
            k_t = ml[:, ML_INNER + h * ML_HEAD_DIM:ML_INNER + (h + 1) * ML_HEAD_DIM]
            v = ml[:, 2 * ML_INNER + h * ML_HEAD_DIM:2 * ML_INNER + (h + 1) * ML_HEAD_DIM]
            v1 = jnp.concatenate([v, ones], axis=1)
            ctn = ct_ref[st]
            cm_t = _lane_fill(cols, h)
            mx_t = jnp.maximum(cm_t, m8[h:h + 1, :])

            dmat = jnp.exp(jnp.where(causal, c8[h:h + 1, :] - cm_t, -jnp.inf))
            s = (_dot(q, k_t) * dmat).astype(BF16)
            intra = jnp.exp(cm_t - mx_t)
            prev = jnp.exp(m8[h:h + 1, :] - mx_t)
            nd = (jnp.concatenate([intra, intra], axis=1) * _dot(s, v1)
                  + jnp.concatenate([prev, prev], axis=1) * _dot(q, ctn.astype(BF16)))
            floor = jnp.exp(-(_lane_fill(cols, ML_HEADS + h) + mx_t))
            hs.append(nd[:, :ML_HEAD_DIM] / jnp.maximum(jnp.abs(nd[:, ML_HEAD_DIM:]), floor))

            dec = dec8[h:h + 1, :]
            k_w = (k_t.astype(F32) * w8[h:h + 1, :]).astype(BF16)
            ct_ref[st] = jnp.concatenate([dec, dec], axis=1) * ctn + _dot(k_w, v1)

        h_dirs.append(jnp.concatenate(hs, axis=1))

    @pl.when(d == 0)
    def _():
        hf_ref[p] = jnp.concatenate(h_dirs, axis=0)

    @pl.when(d == 1)
    def _():
        tot = hf_ref[p] + jnp.concatenate(h_dirs, axis=0)
        for h in range(ML_HEADS):
            sl = slice(h * ML_HEAD_DIM, (h + 1) * ML_HEAD_DIM)
            o_gate = ml_ref[:, 3 * ML_INNER + h * ML_HEAD_DIM:3 * ML_INNER + (h + 1) * ML_HEAD_DIM].astype(F32)
            o_ref[:, sl] = (_rmsnorm(tot[:, sl], gain[:, sl]) * _sigmoid(o_gate)).astype(o_ref.dtype)


def _scan_block_maps(dims):
    nct, nlt = dims["nct"], dims["nlt"]
    per_chunk = dims["B"] // SCAN_BATCH
    nc = nct + nlt

    def blk(bg, d, c):
        return _scan_chunk(d, c, nct, nlt) * per_chunk + bg

    def fwd_blk(bg, d, c):
        return jnp.where(d == 0, blk(bg, 0, c), blk(bg, 0, nc - 1))

    def bwd_blk(bg, d, c):
        return jnp.where(d == 0, blk(bg, 1, 0), blk(bg, 1, c))

    return blk, fwd_blk, bwd_blk


def _mlstm(ml, gate_rows, gate_cols, gain_row, dims):
    nct, nlt = dims["nct"], dims["nlt"]
    nc = nct + nlt
    rows = SCAN_BATCH * CHUNK
    blk, _, bwd_blk = _scan_block_maps(dims)
    return pl.pallas_call(
        functools.partial(_mlstm_kernel, nct=nct, nlt=nlt),
        out_shape=jax.ShapeDtypeStruct((dims["n_tok"], ML_INNER), BF16),
        grid=(dims["B"] // SCAN_BATCH, 2, nc),
        in_specs=[
            pl.BlockSpec((rows, 4 * ML_INNER), lambda bg, d, c: (blk(bg, d, c), 0)),
            pl.BlockSpec((None, SCAN_BATCH, ML_ROWS, LANES), lambda bg, d, c: (d, blk(bg, d, c), 0, 0)),
            pl.BlockSpec((None, rows, LANES), lambda bg, d, c: (d, blk(bg, d, c), 0)),
            pl.BlockSpec((1, ML_INNER), lambda bg, d, c: (0, 0)),
        ],
        out_specs=pl.BlockSpec((rows, ML_INNER), lambda bg, d, c: (bwd_blk(bg, d, c), 0)),
        scratch_shapes=[
            pltpu.VMEM((SCAN_BATCH * ML_HEADS, ML_HEAD_DIM, 2 * ML_HEAD_DIM), F32),
            pltpu.VMEM((SCAN_BATCH, 8, LANES), F32),
            pltpu.VMEM((nc, rows, ML_INNER), F32),
        ],
        compiler_params=_cparams(("arbitrary", "arbitrary", "arbitrary")),
        name="mlstm",
    )(ml, gate_rows, gate_cols, gain_row)


def _ssd_kernel(*refs, nct, nlt):
    xbc_ref = refs[0]
    prev_refs = refs[1:1 + SCAN_BATCH]
    next_refs = refs[1 + SCAN_BATCH:1 + 2 * SCAN_BATCH]
    (z_ref, sr_ref, sc_ref, cw_ref, cb_ref, dsk_ref, gain_ref,
     o_ref, st_ref, xa_ref, yf_ref, ext_ref) = refs[1 + 2 * SCAN_BATCH:]
    d = pl.program_id(1)
    c = pl.program_id(2)

    @pl.when(c == 0)
    def _():
        st_ref[...] = jnp.zeros_like(st_ref)

    p = _scan_chunk(d, c, nct, nlt)
    lane = lax.broadcasted_iota(jnp.int32, (1, LANES), 1)
    low = lane < SSD_HEAD_DIM
    causal, _ = _direction_masks(d)

    @pl.when(d == 0)
    def _():
        has_prev = jnp.where((p != 0) & (p != nct), 1.0, 0.0)
        has_next = jnp.where((p != nct - 1) & (p != nct + nlt - 1), 1.0, 0.0)
        cw = cw_ref[...]
        for i in range(SCAN_BATCH):
            rows = slice(i * CHUNK, (i + 1) * CHUNK)
            ext_ref[i, 0:HALO, :] = prev_refs[i][...].astype(F32) * has_prev
            ext_ref[i, HALO:HALO + CHUNK, :] = xbc_ref[rows, :].astype(F32)
            ext_ref[i, HALO + CHUNK:, :] = next_refs[i][...].astype(F32) * has_next
            acc = jnp.zeros((CHUNK, SSD_XBC), F32) + cb_ref[...]
            for tap in range(SSD_CONV):
                off = HALO - SSD_CONV // 2 + tap
                acc = acc + ext_ref[i, off:off + CHUNK, :] * cw[tap:tap + 1, :]
            xa_ref[p, rows, :] = acc * _sigmoid(acc)

    y_dirs = []
    for i in range(SCAN_BATCH):
        rows = slice(i * CHUNK, (i + 1) * CHUNK)
        xa = xa_ref[p, rows, :]
        dt8 = sr_ref[i, 0:8]
        acs8 = sr_ref[i, 8:16]
        dw8 = sr_ref[i, 16:24]
        dec8 = sr_ref[i, 24:32]
        cols = sc_ref[rows, :]

        ys = []
        for g in range(SSD_GROUPS):
            bm_f = xa[:, SSD_INNER + g * SSD_STATE:SSD_INNER + (g + 1) * SSD_STATE]
            cm = xa[:, SSD_INNER + (SSD_GROUPS + g) * SSD_STATE:
                    SSD_INNER + (SSD_GROUPS + g + 1) * SSD_STATE].astype(BF16)
            cb = _dot_nt(cm, bm_f.astype(BF16))
            bm_t = bm_f.T
            st = st_ref[i * SSD_GROUPS + g]
            y_state = _dot(cm, st.astype(BF16))
            st_parts = []
            for pair in range(SSD_HPG // 2):
                h0 = g * SSD_HPG + 2 * pair
                x_pair = xa[:, h0 * SSD_HEAD_DIM:h0 * SSD_HEAD_DIM + LANES].astype(BF16)
                y_pair = None
                st_pair = None
                grow = []
                for half in range(2):
                    h = h0 + half
                    a_t = _lane_fill(cols, h)
                    e = jnp.exp(jnp.where(causal, a_t - acs8[h:h + 1, :], -jnp.inf))
                    mmat = (cb * e * dt8[h:h + 1, :]).astype(BF16)
                    xm = jnp.where(low if half == 0 else ~low, x_pair, jnp.zeros_like(x_pair))
                    y_h = _dot(mmat, xm)
                    s_h = _dot((bm_t * dw8[h:h + 1, :]).astype(BF16), xm)
                    y_pair = y_h if y_pair is None else y_pair + y_h
                    st_pair = s_h if st_pair is None else st_pair + s_h
                    grow.append(jnp.exp(a_t))
                ys.append(y_pair + jnp.where(low, grow[0], grow[1]) * y_state[:, pair * LANES:(pair + 1) * LANES])
                dec = jnp.where(low, dec8[h0:h0 + 1, :], dec8[h0 + 1:h0 + 2, :])
                st_parts.append(dec * st[:, pair * LANES:(pair + 1) * LANES] + st_pair)
            st_ref[i * SSD_GROUPS + g] = jnp.concatenate(st_parts, axis=1)
        y_dirs.append(jnp.concatenate(ys, axis=1))

    @pl.when(d == 0)
    def _():
        yf_ref[p] = jnp.concatenate(y_dirs, axis=0)

    @pl.when(d == 1)
    def _():
        y = yf_ref[p] + jnp.concatenate(y_dirs, axis=0) + dsk_ref[...] * xa_ref[p, :, 0:SSD_INNER]
        z = z_ref[...].astype(F32)
        o_ref[...] = _rmsnorm(y * (z * _sigmoid(z)), gain_ref[...]).astype(o_ref.dtype)


def _ssd(z, xbc, gate_rows, gate_cols, conv_w, conv_b, dskip_row, gain_row, dims):
    n_b, nct, nlt = dims["B"], dims["nct"], dims["nlt"]
    nc = nct + nlt
    rows = SCAN_BATCH * CHUNK
    sub = CHUNK // HALO
    n_halo_blocks = dims["n_tok"] // HALO
    blk, fwd_blk, bwd_blk = _scan_block_maps(dims)

    def halo(i, side):
        def idx(bg, d, c):
            group = fwd_blk(bg, d, c) * SCAN_BATCH + i
            if side < 0:
                return (jnp.maximum((group - n_b) * sub + sub - 1, 0), 0)
            return (jnp.minimum((group + n_b) * sub, n_halo_blocks - 1), 0)
        return idx

    const = lambda bg, d, c: (0, 0)
    return pl.pallas_call(
        functools.partial(_ssd_kernel, nct=nct, nlt=nlt),
        out_shape=jax.ShapeDtypeStruct((dims["n_tok"], SSD_INNER), BF16),
        grid=(n_b // SCAN_BATCH, 2, nc),
        in_specs=[pl.BlockSpec((rows, SSD_XBC), lambda bg, d, c: (fwd_blk(bg, d, c), 0))]
        + [pl.BlockSpec((HALO, SSD_XBC), halo(i, -1)) for i in range(SCAN_BATCH)]
        + [pl.BlockSpec((HALO, SSD_XBC), halo(i, +1)) for i in range(SCAN_BATCH)]
        + [
            pl.BlockSpec((rows, SSD_INNER), lambda bg, d, c: (bwd_blk(bg, d, c), 0)),
            pl.BlockSpec((None, SCAN_BATCH, SSD_ROWS, LANES), lambda bg, d, c: (d, blk(bg, d, c), 0, 0)),
            pl.BlockSpec((None, rows, LANES), lambda bg, d, c: (d, blk(bg, d, c), 0)),
            pl.BlockSpec((8, SSD_XBC), const),
            pl.BlockSpec((1, SSD_XBC), const),
            pl.BlockSpec((1, SSD_INNER), const),
            pl.BlockSpec((1, SSD_INNER), const),
        ],
        out_specs=pl.BlockSpec((rows, SSD_INNER), lambda bg, d, c: (bwd_blk(bg, d, c), 0)),
        scratch_shapes=[
            pltpu.VMEM((SCAN_BATCH * SSD_GROUPS, SSD_STATE, SSD_HPG * SSD_HEAD_DIM), F32),
            pltpu.VMEM((nc, rows, SSD_XBC), F32),
            pltpu.VMEM((nc, rows, SSD_INNER), F32),
            pltpu.VMEM((SCAN_BATCH, CHUNK + 2 * HALO, SSD_XBC), F32),
        ],
        compiler_params=_cparams(("arbitrary", "arbitrary", "arbitrary")),
        name="ssd",
    )(xbc, *([xbc] * (2 * SCAN_BATCH)), z, gate_rows, gate_cols, conv_w, conv_b, dskip_row, gain_row)


def _merge_ffn_kernel(*refs, n_src, ctx_tiles, final_norm):
    x_refs = refs[:n_src]
    (att_ref, ml_ref, ss_ref, gt_ref, mod_ref, g2_ref, gf_ref,
     wa_ref, wm_ref, ws_ref, wo_ref, wup_ref, wdn_ref, o_ref, x1_ref, h_ref) = refs[n_src:]
    is_ctx_tile = pl.program_id(0) < ctx_tiles
    y = None
    for i, (src, w) in enumerate(((att_ref, wa_ref), (ml_ref, wm_ref), (ss_ref, ws_ref))):
        gate = _sigmoid(gt_ref[:, i * D_MODEL:(i + 1) * D_MODEL].astype(F32))
        term = gate * _dot(src[...], w[...])
        y = term if y is None else y + term
    yo = _dot(y.astype(BF16), wo_ref[...])
    g2 = g2_ref[...]
    for k in range(FFN_GROUPS):
        rows = slice(k * CHUNK, (k + 1) * CHUNK)
        mod = mod_ref[k]
        x1 = _read_stream(x_refs, k, is_ctx_tile) + mod[2:3] * yo[rows]
        x1_ref[rows, :] = x1
        h_ref[rows, :] = (_rmsnorm(x1, g2) * (1.0 + mod[4:5]) + mod[3:4]).astype(BF16)
    h = h_ref[...]
    acc = None
    for c0 in range(0, D_FF, FFN_CHUNK):
        gate = _dot(h, wup_ref[:, c0:c0 + FFN_CHUNK])
        up = _dot(h, wup_ref[:, D_FF + c0:D_FF + c0 + FFN_CHUNK])
        act = (gate * _sigmoid(gate) * up).astype(BF16)
        part = _dot(act, wdn_ref[c0:c0 + FFN_CHUNK, :])
        acc = part if acc is None else acc + part
    for k in range(FFN_GROUPS):
        rows = slice(k * CHUNK, (k + 1) * CHUNK)
        x2 = x1_ref[rows, :] + mod_ref[k][5:6] * acc[rows]
        if final_norm:
            x2 = _rmsnorm(x2, gf_ref[...])
        o_ref[k] = x2


def _merge_ffn(stream, first, att, ml, ss, gates, mods, layer, g2, gf, wts, dims, last):
    n_b, nct, nlt = dims["B"], dims["nct"], dims["nlt"]
    tm = FFN_GROUPS * CHUNK
    per_chunk = n_b // FFN_GROUPS
    ctx_tiles = nct * per_chunk
    t0 = ctx_tiles if last else 0
    n_tiles = (nct + nlt) * per_chunk - t0
    assert not (first and last)

    row = lambda i: (i + t0, 0)
    att_row = (lambda i: (i, 0)) if last else row
    wa, wm, ws, wo, wup, wdn = wts
    if last:
        out_shape = jax.ShapeDtypeStruct((n_b, nlt, CHUNK, D_MODEL), F32)
        out_spec = pl.BlockSpec((FFN_GROUPS, None, CHUNK, D_MODEL), lambda i: (i % per_chunk, i // per_chunk, 0, 0))
        stream_specs = [pl.BlockSpec((FFN_GROUPS, CHUNK, D_MODEL), lambda i: (i + t0, 0, 0))]
    else:
        out_shape = jax.ShapeDtypeStruct(((nct + nlt) * n_b, CHUNK, D_MODEL), F32)
        out_spec = pl.BlockSpec((FFN_GROUPS, CHUNK, D_MODEL), lambda i: (i, 0, 0))
        stream_specs = _stream_specs(first, FFN_GROUPS, dims)
    return pl.pallas_call(
        functools.partial(_merge_ffn_kernel, n_src=len(stream), ctx_tiles=ctx_tiles - t0, final_norm=last),
        out_shape=out_shape,
        grid=(n_tiles,),
        in_specs=stream_specs + [
            pl.BlockSpec((tm, ATT_Q), att_row),
            pl.BlockSpec((tm, ML_INNER), row),
            pl.BlockSpec((tm, SSD_INNER), row),
            pl.BlockSpec((tm, 3 * D_MODEL), row),
            _mod_spec(layer, FFN_GROUPS, dims, tile0=t0),
            pl.BlockSpec((1, D_MODEL), lambda i: (0, 0)),
            pl.BlockSpec((1, D_MODEL), lambda i: (0, 0)),
            _resident(wa.shape), _resident(wm.shape), _resident(ws.shape),
            _resident(wo.shape), _resident(wup.shape), _resident(wdn.shape),
        ],
        out_specs=out_spec,
        scratch_shapes=[pltpu.VMEM((tm, D_MODEL), F32), pltpu.VMEM((tm, D_MODEL), BF16)],
        compiler_params=_cparams(("arbitrary",)),
        name="merge_ffn",
    )(*stream, att, ml, ss, gates, mods, g2, gf, wa, wm, ws, wo, wup, wdn)


def _rope_tables(seq, nct):
    pos = np.arange(seq)
    row = jnp.asarray((pos // GRID_W).astype(np.float32))
    col = jnp.asarray((pos % GRID_W).astype(np.float32))
    inv = jnp.asarray((ROPE_BASE ** (-np.arange(ROPE_FREQS, dtype=np.float32) / ROPE_FREQS)).astype(np.float32))
    ang_r = row[:, None] * inv
    ang_c = col[:, None] * inv
    cos_h = jnp.concatenate([jnp.cos(ang_r), jnp.cos(ang_r), jnp.cos(ang_c), jnp.cos(ang_c)], axis=1)
    sin_h = jnp.concatenate([-jnp.sin(ang_r), jnp.sin(ang_r), -jnp.sin(ang_c), jnp.sin(ang_c)], axis=1)
    cos_t = jnp.concatenate([cos_h, cos_h], axis=1)
    sin_t = jnp.concatenate([sin_h, sin_h], axis=1)
    cos_t = jnp.concatenate([jnp.ones((nct * CHUNK, LANES), F32), cos_t], axis=0)
    sin_t = jnp.concatenate([jnp.zeros((nct * CHUNK, LANES), F32), sin_t], axis=0)
    return cos_t, sin_t


def _permute_w_in(w):
    d = w.shape[0]
    o_att = 0
    o_ml = ATT_Q + 2 * ATT_KV
    o_mlg = o_ml + 4 * ML_INNER
    o_z = o_mlg + 4 * ML_HEADS
    o_xbc = o_z + SSD_INNER
    o_dt = o_xbc + SSD_XBC
    o_g = o_dt + 2 * SSD_HEADS
    pad = jnp.zeros((d, LANES - 4 * ML_HEADS - 2 * SSD_HEADS), w.dtype)
    cols = [w[:, o_att:o_mlg], w[:, o_z:o_dt], w[:, o_g:], w[:, o_mlg:o_z], w[:, o_dt:o_g], pad]
    return jnp.concatenate(cols, axis=1).astype(BF16)


def _lane_row(vals):
    return jnp.zeros((1, LANES), F32).at[0, :vals.shape[0]].set(vals.astype(F32))


def kernel(x, c, ctx, c_ctx, w_mod, b_mod, g_norm1, w_in, att_sink, ml_i_bias, ml_f_bias, ml_head_gain,
           ssd_conv_w, ssd_conv_b, ssd_dt_bias, ssd_a_log, ssd_d, ssd_norm_gain,
           w_att_out, w_ml_out, w_ssd_out, w_o, g_norm2, w_up, w_down, g_final):
    n_b, seq, d_model = x.shape
    lc = ctx.shape[1]
    depth = w_mod.shape[0]
    nct, nlt = lc // CHUNK, seq // CHUNK
    dims = dict(B=n_b, nct=nct, nlt=nlt, n_tok=(nct + nlt) * n_b * CHUNK)
    assert d_model == D_MODEL and seq % CHUNK == 0 and lc % CHUNK == 0 and nlt >= 3 and depth >= 2
    assert n_b % PROJ_GROUPS == 0 and n_b % FFN_GROUPS == 0 and n_b % SCAN_BATCH == 0
    assert n_b + PROJ_GROUPS <= MOD_ROWS

    cc = jnp.zeros((MOD_ROWS, d_model), F32).at[:n_b].set(c).at[n_b:n_b + PROJ_GROUPS].set(c_ctx)
    mods = _modulation(cc, w_mod, b_mod).reshape(depth, MOD_ROWS, N_MOD, d_model)
    cos_t, sin_t = _rope_tables(seq, nct)

    stream = [ctx.reshape(n_b, nct, CHUNK, d_model), x.reshape(n_b, nlt, CHUNK, d_model)]
    for l in range(depth):
        first, last = l == 0, l == depth - 1
        w_p = _permute_w_in(w_in[l])
        q, kv, ml, z, xbc, gates, small = _in_projection(
            stream, first, mods, l, g_norm1[l].reshape(1, -1), cos_t, sin_t, w_p, dims)

        att = _attention(q, kv, _lane_row(att_sink[l]), dims, ctx_queries=not last)

        gbias = jnp.broadcast_to(jnp.concatenate([ml_i_bias[l], ml_f_bias[l]], axis=1).astype(F32)[:, :, None],
                                 (2, 2 * ML_HEADS, LANES))
        a_neg = -jnp.exp(ssd_a_log[l].astype(F32))
        sbias = jnp.broadcast_to(jnp.concatenate([ssd_dt_bias[l].astype(F32), a_neg], axis=0)[:, :, None],
                                 (4, SSD_HEADS, LANES))
        ml_rows, ml_cols, ssd_rows, ssd_cols = _gate_prep(small, gbias, sbias, dims)

        mlo = _mlstm(ml, ml_rows, ml_cols, ml_head_gain[l].reshape(1, -1), dims)

        conv_w = jnp.zeros((8, SSD_XBC), F32).at[:SSD_CONV].set(ssd_conv_w[l])
        sso = _ssd(z, xbc, ssd_rows, ssd_cols, conv_w, ssd_conv_b[l].reshape(1, -1),
                   jnp.repeat(ssd_d[l].astype(F32), SSD_HEAD_DIM).reshape(1, -1),
                   ssd_norm_gain[l].reshape(1, -1), dims)

        wts = tuple(w.astype(BF16) for w in (w_att_out[l], w_ml_out[l], w_ssd_out[l], w_o[l], w_up[l], w_down[l]))
        out = _merge_ffn(stream, first, att, mlo, sso, gates, mods, l, g_norm2[l].reshape(1, -1),
                         g_final.reshape(1, -1), wts, dims, last)
        stream = [out]
    return out.reshape(n_b, seq, d_model)
```

```python
import functools

import numpy as np
import jax
import jax.numpy as jnp
from jax import lax
from jax.experimental import pallas as pl
from jax.experimental.pallas import tpu as pltpu

F32 = jnp.float32
BF16 = jnp.bfloat16

D_MODEL = 1024
EPS = 1e-6
N_MOD = 6
GRID_W = 64
ROPE_BASE = 10000.0

ATT_HEADS = 8
ATT_KV_HEADS = 2
ATT_GROUP = ATT_HEADS // ATT_KV_HEADS
ATT_HEAD_DIM = 64
ATT_WINDOW = 128
ATT_Q = ATT_HEADS * ATT_HEAD_DIM
ATT_KV = ATT_KV_HEADS * ATT_HEAD_DIM
ROPE_FREQS = ATT_HEAD_DIM // 4

ML_HEADS = 4
ML_HEAD_DIM = 128
ML_INNER = ML_HEADS * ML_HEAD_DIM

SSD_HEADS = 8
SSD_HEAD_DIM = 64
SSD_GROUPS = 2
SSD_HPG = SSD_HEADS // SSD_GROUPS
SSD_STATE = 128
SSD_CONV = 5
SSD_INNER = SSD_HEADS * SSD_HEAD_DIM
SSD_XBC = SSD_INNER + 2 * SSD_GROUPS * SSD_STATE

D_FF = -((-8 * D_MODEL) // (3 * 256)) * 256
MXU_DIM = 256
FFN_CHUNKS = ((0, 5 * MXU_DIM), (5 * MXU_DIM, D_FF - 5 * MXU_DIM))

CHUNK = 128
LANES = 128
HALO = 8
PROJ_GROUPS = 4
FFN_GROUPS = 2
SCAN_BATCH = 2
ATT_BATCH = 2
MOD_ROWS = 24
VMEM_LIMIT = 56 * 1024 * 1024

W_Q0 = 0
W_KV0 = ATT_Q
W_ML0 = W_KV0 + 2 * ATT_KV
W_Z0 = W_ML0 + 4 * ML_INNER
W_XBC0 = W_Z0 + SSD_INNER
W_G0 = W_XBC0 + SSD_XBC
W_S0 = W_G0 + 3 * D_MODEL
W_TOT = W_S0 + LANES
DT_LANE0 = 4 * ML_HEADS


def _cparams(sem):
    return pltpu.CompilerParams(dimension_semantics=sem, vmem_limit_bytes=VMEM_LIMIT)


def _resident(shape):
    nd = len(shape)
    return pl.BlockSpec(shape, lambda *_: (0,) * nd, pipeline_mode=pl.Buffered(1))


def _sigmoid(x):
    return 1.0 / (1.0 + jnp.exp(-x))


def _softplus(x):
    return jnp.maximum(x, 0.0) + jnp.log1p(jnp.exp(-jnp.abs(x)))


def _dot(a, b):
    return jnp.dot(a, b, preferred_element_type=F32)


def _dot_nt(a, b):
    return lax.dot_general(a, b, (((1,), (1,)), ((), ())), preferred_element_type=F32)


def _dot_exact(a, b):
    return jnp.dot(a, b, preferred_element_type=F32, precision=lax.Precision.HIGHEST)


def _rmsnorm(x, gain):
    return x * lax.rsqrt(jnp.mean(x * x, axis=-1, keepdims=True) + EPS) * gain


def _mod_kernel(c_ref, w_ref, b_ref, o_ref):
    c = c_ref[...]
    a = (c * _sigmoid(c)).astype(BF16)
    o_ref[...] = _dot(a, w_ref[...].astype(BF16)) + b_ref[...]


def _modulation(cc, w_mod, b_mod):
    depth, d, n = w_mod.shape
    tn = 1536
    return pl.pallas_call(
        _mod_kernel,
        out_shape=jax.ShapeDtypeStruct((depth, MOD_ROWS, n), F32),
        grid=(depth, n // tn),
        in_specs=[
            pl.BlockSpec((MOD_ROWS, d), lambda l, j: (0, 0)),
            pl.BlockSpec((None, d, tn), lambda l, j: (l, 0, j)),
            pl.BlockSpec((None, 1, tn), lambda l, j: (l, 0, j)),
        ],
        out_specs=pl.BlockSpec((None, MOD_ROWS, tn), lambda l, j: (l, 0, j)),
        compiler_params=_cparams(("arbitrary", "arbitrary")),
        name="modulation",
    )(cc, w_mod, b_mod.reshape(depth, 1, n))


def _stream_specs(first, groups, dims):
    n_b, nct, nlt = dims["B"], dims["nct"], dims["nlt"]
    per_chunk = n_b // groups
    if not first:
        return [pl.BlockSpec((groups, CHUNK, D_MODEL), lambda i: (i, 0, 0))]

    def ctx_idx(i):
        p = jnp.minimum(i // per_chunk, nct - 1)
        return (jnp.where(i < nct * per_chunk, i % per_chunk, per_chunk - 1), p, 0, 0)

    def lat_idx(i):
        p = jnp.maximum(i // per_chunk - nct, 0)
        return (jnp.where(i < nct * per_chunk, 0, i % per_chunk), p, 0, 0)

    return [pl.BlockSpec((groups, None, CHUNK, D_MODEL), ctx_idx),
            pl.BlockSpec((groups, None, CHUNK, D_MODEL), lat_idx)]


def _mod_spec(layer, groups, dims, tile0=0):
    n_b, nct = dims["B"], dims["nct"]
    per_chunk = n_b // groups

    def idx(i):
        i = i + tile0
        return (layer, jnp.where(i < nct * per_chunk, per_chunk, i % per_chunk), 0, 0)

    return pl.BlockSpec((None, groups, N_MOD, D_MODEL), idx)


def _read_stream(refs, k, is_ctx_tile):
    if len(refs) == 1:
        return refs[0][k]
    return jnp.where(is_ctx_tile, refs[0][k], refs[1][k])


def _rope(x, cos, sin_signed, first_half):
    partner = jnp.where(first_half, pltpu.roll(x, LANES - ROPE_FREQS, 1), pltpu.roll(x, ROPE_FREQS, 1))
    return x * cos + partner * sin_signed


def _inproj_kernel(*refs, n_src, ctx_tiles):
    x_refs = refs[:n_src]
    (mod_ref, g_ref, cos_ref, sin_ref, gb_ref, sb_ref, w_ref,
     oq_ref, okv_ref, oml_ref, oz_ref, oxbc_ref, og_ref,
     mr_ref, mc_ref, sr_ref, sc_ref, h_ref) = refs[n_src:]
    is_ctx_tile = pl.program_id(0) < ctx_tiles
    gain = g_ref[...]
    for k in range(PROJ_GROUPS):
        mod = mod_ref[k]
        y = _rmsnorm(_read_stream(x_refs, k, is_ctx_tile), gain)
        h_ref[k * CHUNK:(k + 1) * CHUNK, :] = (y * (1.0 + mod[1:2]) + mod[0:1]).astype(BF16)
    h = h_ref[...]

    cos = jnp.concatenate([cos_ref[...]] * PROJ_GROUPS, axis=0)
    sin = jnp.concatenate([sin_ref[...]] * PROJ_GROUPS, axis=0)
    lane = lax.broadcasted_iota(jnp.int32, (1, LANES), 1)
    first_half = (lane % (2 * ROPE_FREQS)) < ROPE_FREQS

    def proj(c0, width):
        return _dot(h, w_ref[:, c0:c0 + width])

    _gate_prep_tile(proj(W_S0, LANES), gb_ref, sb_ref, mr_ref, mc_ref, sr_ref, sc_ref)

    q = proj(W_Q0, ATT_Q)
    for s in range(ATT_Q // LANES):
        qs = _rope(q[:, s * LANES:(s + 1) * LANES], cos, sin, first_half)
        oq_ref[:, s * LANES:(s + 1) * LANES] = (qs * ATT_HEAD_DIM ** -0.5).astype(oq_ref.dtype)
    kv = proj(W_KV0, 2 * ATT_KV)
    okv_ref[:, 0:ATT_KV] = _rope(kv[:, 0:ATT_KV], cos, sin, first_half).astype(okv_ref.dtype)
    okv_ref[:, ATT_KV:] = kv[:, ATT_KV:].astype(okv_ref.dtype)

    oml_ref[:, 0:ML_INNER] = (proj(W_ML0, ML_INNER) * ML_HEAD_DIM ** -0.5).astype(oml_ref.dtype)
    kproj = proj(W_ML0 + ML_INNER, ML_INNER)
    for r in range(PROJ_GROUPS):
        for hd in range(ML_HEADS):
            blk = kproj[r * CHUNK:(r + 1) * CHUNK, hd * ML_HEAD_DIM:(hd + 1) * ML_HEAD_DIM]
            oml_ref[r * CHUNK:(r + 1) * CHUNK,
                    ML_INNER + hd * ML_HEAD_DIM:ML_INNER + (hd + 1) * ML_HEAD_DIM] = blk.T.astype(oml_ref.dtype)
    for s in range(2, 4):
        oml_ref[:, s * ML_INNER:(s + 1) * ML_INNER] = proj(W_ML0 + s * ML_INNER, ML_INNER).astype(oml_ref.dtype)

    oz_ref[...] = proj(W_Z0, SSD_INNER).astype(oz_ref.dtype)
    for s in range(SSD_XBC // 512):
        oxbc_ref[:, s * 512:(s + 1) * 512] = proj(W_XBC0 + s * 512, 512).astype(oxbc_ref.dtype)
    for s in range(3 * D_MODEL // 512):
        og_ref[:, s * 512:(s + 1) * 512] = proj(W_G0 + s * 512, 512).astype(og_ref.dtype)


def _in_projection(stream, first, mods, layer, g1, cos_t, sin_t, w_p, gbias, sbias, dims):
    n_b, nct = dims["B"], dims["nct"]
    n_tok = dims["n_tok"]
    n_groups = n_tok // CHUNK
    tm = PROJ_GROUPS * CHUNK
    per_chunk = n_b // PROJ_GROUPS
    outs = [
        jax.ShapeDtypeStruct((n_tok, ATT_Q), BF16),
        jax.ShapeDtypeStruct((n_tok, 2 * ATT_KV), BF16),
        jax.ShapeDtypeStruct((n_tok, 4 * ML_INNER), BF16),
        jax.ShapeDtypeStruct((n_tok, SSD_INNER), F32),
        jax.ShapeDtypeStruct((n_tok, SSD_XBC), F32),
        jax.ShapeDtypeStruct((n_tok, 3 * D_MODEL), F32),
    ]
    gate_outs = [
        jax.ShapeDtypeStruct((2, n_groups, ML_ROWS, LANES), F32),
        jax.ShapeDtypeStruct((2, n_tok, LANES), F32),
        jax.ShapeDtypeStruct((2, n_groups, SSD_ROWS, LANES), F32),
        jax.ShapeDtypeStruct((2, n_tok, LANES), F32),
    ]
    gate_specs = [
        pl.BlockSpec((2, PROJ_GROUPS, ML_ROWS, LANES), lambda i: (0, i, 0, 0)),
        pl.BlockSpec((2, tm, LANES), lambda i: (0, i, 0)),
        pl.BlockSpec((2, PROJ_GROUPS, SSD_ROWS, LANES), lambda i: (0, i, 0, 0)),
        pl.BlockSpec((2, tm, LANES), lambda i: (0, i, 0)),
    ]
    return pl.pallas_call(
        functools.partial(_inproj_kernel, n_src=len(stream), ctx_tiles=nct * per_chunk),
        out_shape=outs + gate_outs,
        grid=(n_tok // tm,),
        in_specs=_stream_specs(first, PROJ_GROUPS, dims) + [
            _mod_spec(layer, PROJ_GROUPS, dims),
            pl.BlockSpec((1, D_MODEL), lambda i: (0, 0)),
            pl.BlockSpec((CHUNK, LANES), lambda i: (i // per_chunk, 0)),
            pl.BlockSpec((CHUNK, LANES), lambda i: (i // per_chunk, 0)),
            pl.BlockSpec((2, 8, LANES), lambda i: (0, 0, 0)),
            pl.BlockSpec((4, SSD_HEADS, LANES), lambda i: (0, 0, 0)),
            _resident((D_MODEL, W_TOT)),
        ],
        out_specs=[pl.BlockSpec((tm, o.shape[1]), lambda i: (i, 0)) for o in outs] + gate_specs,
        scratch_shapes=[pltpu.VMEM((tm, D_MODEL), BF16)],
        compiler_params=_cparams(("arbitrary",)),
        name="in_projection",
    )(*stream, mods, g1, cos_t, sin_t, gbias, sbias, w_p)


def _attention_block(q, kv, valid, sink, o_ref):
    lane = lax.broadcasted_iota(jnp.int32, (1, LANES), 1)
    low = lane < ATT_HEAD_DIM
    ones = jnp.ones((kv.shape[0], LANES), BF16)
    zero = jnp.zeros((CHUNK, LANES), BF16)
    for g in range(ATT_KV_HEADS):
        k_g = kv[:, g * ATT_HEAD_DIM:(g + 1) * ATT_HEAD_DIM]
        v_g = kv[:, ATT_KV + g * ATT_HEAD_DIM:ATT_KV + (g + 1) * ATT_HEAD_DIM]
        kk = jnp.concatenate([k_g, k_g], axis=1)
        vw = jnp.concatenate([v_g, v_g, ones], axis=1)
        q_rows = []
        for pair in range(ATT_GROUP // 2):
            c0 = (g * ATT_GROUP + 2 * pair) * ATT_HEAD_DIM
            qp = q[:, c0:c0 + LANES]
            q_rows += [jnp.where(low, qp, zero), jnp.where(low, zero, qp)]
        s_all = _dot_nt(jnp.concatenate(q_rows, axis=0), kk)
        p_rows, m_rows, sk_rows = [], [], []
        for r in range(ATT_GROUP):
            s = s_all[r * CHUNK:(r + 1) * CHUNK]
            if valid is not None:
                s = jnp.where(valid, s, -jnp.inf)
            sk = sink[:, g * ATT_GROUP + r:g * ATT_GROUP + r + 1]
            m = jnp.maximum(jnp.max(s, axis=-1, keepdims=True), sk)
            p_rows.append(jnp.exp((s - m).astype(BF16)))
            m_rows.append(m)
            sk_rows.append(sk)
        o_all = _dot(jnp.concatenate(p_rows, axis=0), vw)
        outs = []
        for r in range(ATT_GROUP):
            o = o_all[r * CHUNK:(r + 1) * CHUNK]
            den = o[:, LANES:] + jnp.exp(sk_rows[r] - m_rows[r])
            outs.append(o[:, :LANES] / den)
        for pair in range(ATT_GROUP // 2):
            c0 = (g * ATT_GROUP + 2 * pair) * ATT_HEAD_DIM
            o_ref[:, c0:c0 + LANES] = jnp.where(low, outs[2 * pair], outs[2 * pair + 1]).astype(o_ref.dtype)


def _attention_kernel(*refs, nct, nlt, ctx_queries):
    q_ref = refs[0]
    loc_refs = refs[1:4]
    ctx_refs = refs[4:4 + nct]
    sink_ref, o_ref = refs[4 + nct:]
    step = pl.program_id(1)
    sink = sink_ref[...]

    def rows_of(i):
        return slice(i * CHUNK, (i + 1) * CHUNK)

    def latent(j):
        span = 3 * CHUNK
        first = jnp.clip(j - 1, 0, nlt - 3)
        n_keys = span + nct * CHUNK
        row = lax.broadcasted_iota(jnp.int32, (CHUNK, n_keys), 0)
        col = lax.broadcasted_iota(jnp.int32, (CHUNK, n_keys), 1)
        dist = (j - first) * CHUNK + row - col
        valid = (jnp.abs(dist) <= ATT_WINDOW) | (col >= span)
        for i in range(ATT_BATCH):
            kv = jnp.concatenate([r[rows_of(i), :] for r in loc_refs + ctx_refs], axis=0)
            _attention_block(q_ref[rows_of(i), :], kv, valid, sink, o_ref.at[rows_of(i), :])

    if not ctx_queries:
        latent(step)
        return

    @pl.when(step < nct)
    def _():
        for i in range(ATT_BATCH):
            kv = jnp.concatenate([r[rows_of(i), :] for r in ctx_refs], axis=0)
            _attention_block(q_ref[rows_of(i), :], kv, None, sink, o_ref.at[rows_of(i), :])

    @pl.when(step >= nct)
    def _():
        latent(step - nct)


def _attention(q, kv, sink_row, dims, ctx_queries):
    n_b, nct, nlt = dims["B"], dims["nct"], dims["nlt"]
    c0 = 0 if ctx_queries else nct
    n_steps = nct + nlt - c0

    per_chunk = n_b // ATT_BATCH
    rows = ATT_BATCH * CHUNK

    def win(i):
        def idx(b, s):
            first = jnp.clip(s + c0 - nct - 1, 0, nlt - 3)
            return ((nct + first + i) * per_chunk + b, 0)
        return idx

    return pl.pallas_call(
        functools.partial(_attention_kernel, nct=nct, nlt=nlt, ctx_queries=ctx_queries),
        out_shape=jax.ShapeDtypeStruct((n_steps * n_b * CHUNK, ATT_Q), BF16),
        grid=(per_chunk, n_steps),
        in_specs=[pl.BlockSpec((rows, ATT_Q), lambda b, s: ((s + c0) * per_chunk + b, 0))]
        + [pl.BlockSpec((rows, 2 * ATT_KV), win(i)) for i in range(3)]
        + [pl.BlockSpec((rows, 2 * ATT_KV), functools.partial(lambda b, s, i: (i * per_chunk + b, 0), i=i))
           for i in range(nct)]
        + [pl.BlockSpec((1, LANES), lambda b, s: (0, 0))],
        out_specs=pl.BlockSpec((rows, ATT_Q), lambda b, s: (s * per_chunk + b, 0)),
        compiler_params=_cparams(("arbitrary", "arbitrary")),
        name="attention",
    )(q, *([kv] * (3 + nct)), sink_row)


def _scan_chunk(d, c, nct, nlt):
    fwd = c
    bwd = jnp.where(c < nct, nct - 1 - c, nct + nlt - 1 - (c - nct))
    return jnp.where(d == 0, fwd, bwd)


def _direction_masks(d):
    ri = lax.broadcasted_iota(jnp.int32, (CHUNK, CHUNK), 0)
    ci = lax.broadcasted_iota(jnp.int32, (CHUNK, CHUNK), 1)
    delta = (ci - ri) * (1 - 2 * d)
    return delta <= 0, jnp.where(delta >= 0, 1.0, 0.0).astype(BF16)


def _cumsum_lanes(tiles, tri_t):
    parts = []
    for x8 in tiles:
        hi = x8.astype(BF16).astype(F32)
        r1 = x8 - hi
        mid = r1.astype(BF16).astype(F32)
        parts += [hi, mid, r1 - mid]
    if len(parts) % 2:
        parts.append(jnp.zeros_like(parts[0]))
    out = _dot(jnp.concatenate(parts, axis=0).astype(BF16), tri_t)
    return [out[24 * j:24 * j + 8] + out[24 * j + 8:24 * j + 16] + out[24 * j + 16:24 * j + 24]
            for j in range(len(tiles))]


def _cummax_lanes(x8, backward):
    lane = lax.broadcasted_iota(jnp.int32, (1, LANES), 1)
    k = 1
    while k < LANES:
        if backward:
            shifted = jnp.where(lane < LANES - k, pltpu.roll(x8, LANES - k, 1), -jnp.inf)
        else:
            shifted = jnp.where(lane >= k, pltpu.roll(x8, k, 1), -jnp.inf)
        x8 = jnp.maximum(x8, shifted)
        k *= 2
    return x8


def _rows_to_columns(x8):
    pad = jnp.zeros((LANES - 8, LANES), F32)
    return jnp.concatenate([x8, pad], axis=0).T


def _lane_fill(cols, j):
    return jnp.broadcast_to(cols[:, j:j + 1], cols.shape)


ML_ROWS = 16
SSD_ROWS = 32


def _gate_prep_tile(small, gb_ref, sb_ref, mr_ref, mc_ref, sr_ref, sc_ref):
    sub8 = lax.broadcasted_iota(jnp.int32, (8, 1), 0)
    head_rows = sub8 < ML_HEADS
    masks = [_direction_masks(dd)[1] for dd in range(2)]
    for r in range(small.shape[0] // CHUNK):
        rows = slice(r * CHUNK, (r + 1) * CHUNK)
        gt = small[rows, :].T
        for dd in range(2):
            g8 = gt[8 * dd:8 * dd + 8] + gb_ref[dd]
            g8 = jnp.where(head_rows, g8, jnp.minimum(g8, 0.0) - jnp.log1p(jnp.exp(-jnp.abs(g8))))
            dt8 = _softplus(gt[DT_LANE0 + 8 * dd:DT_LANE0 + 8 * dd + 8] + sb_ref[dd])
            la8 = dt8 * sb_ref[2 + dd]
            gsum, acs8 = _cumsum_lanes([g8, la8], masks[dd])

            b8 = pltpu.roll(gsum, ML_HEADS, 0)
            c8 = g8 - b8
            cm8 = _cummax_lanes(c8, backward=dd == 1)
            cm_end = jnp.broadcast_to(jnp.max(cm8, axis=1, keepdims=True), cm8.shape)
            b_end = jnp.broadcast_to(jnp.sum(g8, axis=1, keepdims=True), g8.shape)
            mr_ref[dd, r, 0:8] = jnp.where(head_rows, c8, 0.0)
            mr_ref[dd, r, 8:16] = jnp.where(head_rows, cm_end, b_end)
            mc_ref[dd, rows, :] = _rows_to_columns(jnp.where(head_rows, cm8, gsum))

            a_end = jnp.sum(la8, axis=1, keepdims=True)
            sr_ref[dd, r, 0:8] = dt8
            sr_ref[dd, r, 8:16] = acs8
            sr_ref[dd, r, 16:24] = dt8 * jnp.exp(a_end - acs8)
            sr_ref[dd, r, 24:32] = jnp.broadcast_to(jnp.exp(a_end), dt8.shape)
            sc_ref[dd, rows, :] = _rows_to_columns(acs8)


def _mlstm_kernel(ml_ref, mr_ref, mc_ref, gain_ref, o_ref, ct_ref, m_ref, hf_ref, *, nct, nlt):
    d = pl.program_id(1)
    c = pl.program_id(2)

    @pl.when(c == 0)
    def _():
        ct_ref[...] = jnp.zeros_like(ct_ref)
        m_ref[...] = jnp.zeros_like(m_ref)

    p = _scan_chunk(d, c, nct, nlt)
    sub8 = lax.broadcasted_iota(jnp.int32, (8, 1), 0)
    causal, _ = _direction_masks(d)
    ones = jnp.ones((CHUNK, ML_HEAD_DIM), BF16)
    gain = gain_ref[...]

    h_dirs = []
    for i in range(SCAN_BATCH):
        rows = slice(i * CHUNK, (i + 1) * CHUNK)
        c8 = mr_ref[i, 0:8]
        m8 = m_ref[i]
        ends = mr_ref[i, 8:16]
        mx_end = jnp.maximum(ends, m8)
        w8 = jnp.exp(c8 - mx_end)
        dec8 = jnp.exp(m8 - mx_end)
        m_ref[i] = jnp.where(sub8 < ML_HEADS, pltpu.roll(ends, ML_HEADS, 0) + mx_end, 0.0)
        cols = mc_ref[rows, :]

        ml = ml_ref[rows, :]
        hs = []
        for h in range(ML_HEADS):
            st = i * ML_HEADS + h
            q = ml[:, h * ML_HEAD_DIM:(h + 1) * ML_HEAD_DIM]
            k_t = ml[:, ML_INNER + h * ML_HEAD_DIM:ML_INNER + (h + 1) * ML_HEAD_DIM]
            v = ml[:, 2 * ML_INNER + h * ML_HEAD_DIM:2 * ML_INNER + (h + 1) * ML_HEAD_DIM]
            v1 = jnp.concatenate([v, ones], axis=1)
            ctn = ct_ref[st]
            cm_t = _lane_fill(cols, h)
            mx_t = jnp.maximum(cm_t, m8[h:h + 1, :])

            dmat = jnp.exp(jnp.where(causal, c8[h:h + 1, :] - cm_t, -jnp.inf))
            s = (_dot(q, k_t) * dmat).astype(BF16)
            intra = jnp.exp(cm_t - mx_t)
            prev = jnp.exp(m8[h:h + 1, :] - mx_t)
            nd = (jnp.concatenate([intra, intra], axis=1) * _dot(s, v1)
                  + jnp.concatenate([prev, prev], axis=1) * _dot(q, ctn.astype(BF16)))
            floor = jnp.exp(-(_lane_fill(cols, ML_HEADS + h) + mx_t))
            hs.append(nd[:, :ML_HEAD_DIM] / jnp.maximum(jnp.abs(nd[:, ML_HEAD_DIM:]), floor))

            dec = dec8[h:h + 1, :]
            k_w = (k_t.astype(F32) * w8[h:h + 1, :]).astype(BF16)
            ct_ref[st] = jnp.concatenate([dec, dec], axis=1) * ctn + _dot(k_w, v1)

        h_dirs.append(jnp.concatenate(hs, axis=1))

    @pl.when(d == 0)
    def _():
        hf_ref[p] = jnp.concatenate(h_dirs, axis=0)

    @pl.when(d == 1)
    def _():
        tot = hf_ref[p] + jnp.concatenate(h_dirs, axis=0)
        for h in range(ML_HEADS):
            sl = slice(h * ML_HEAD_DIM, (h + 1) * ML_HEAD_DIM)
            o_gate = ml_ref[:, 3 * ML_INNER + h * ML_HEAD_DIM:3 * ML_INNER + (h + 1) * ML_HEAD_DIM].astype(F32)
            o_ref[:, sl] = (_rmsnorm(tot[:, sl], gain[:, sl]) * _sigmoid(o_gate)).astype(o_ref.dtype)


def _scan_block_maps(dims):
    nct, nlt = dims["nct"], dims["nlt"]
    per_chunk = dims["B"] // SCAN_BATCH
    nc = nct + nlt

    def blk(bg, d, c):
        return _scan_chunk(d, c, nct, nlt) * per_chunk + bg

    def fwd_blk(bg, d, c):
        return jnp.where(d == 0, blk(bg, 0, c), blk(bg, 0, nc - 1))

    def bwd_blk(bg, d, c):
        return jnp.where(d == 0, blk(bg, 1, 0), blk(bg, 1, c))

    return blk, fwd_blk, bwd_blk


def _mlstm(ml, gate_rows, gate_cols, gain_row, dims):
    nct, nlt = dims["nct"], dims["nlt"]
    nc = nct + nlt
    rows = SCAN_BATCH * CHUNK
    blk, _, bwd_blk = _scan_block_maps(dims)
    return pl.pallas_call(
        functools.partial(_mlstm_kernel, nct=nct, nlt=nlt),
        out_shape=jax.ShapeDtypeStruct((dims["n_tok"], ML_INNER), BF16),
        grid=(dims["B"] // SCAN_BATCH, 2, nc),
        in_specs=[
            pl.BlockSpec((rows, 4 * ML_INNER), lambda bg, d, c: (blk(bg, d, c), 0)),
            pl.BlockSpec((None, SCAN_BATCH, ML_ROWS, LANES), lambda bg, d, c: (d, blk(bg, d, c), 0, 0)),
            pl.BlockSpec((None, rows, LANES), lambda bg, d, c: (d, blk(bg, d, c), 0)),
            pl.BlockSpec((1, ML_INNER), lambda bg, d, c: (0, 0)),
        ],
        out_specs=pl.BlockSpec((rows, ML_INNER), lambda bg, d, c: (bwd_blk(bg, d, c), 0)),
        scratch_shapes=[
            pltpu.VMEM((SCAN_BATCH * ML_HEADS, ML_HEAD_DIM, 2 * ML_HEAD_DIM), F32),
            pltpu.VMEM((SCAN_BATCH, 8, LANES), F32),
            pltpu.VMEM((nc, rows, ML_INNER), F32),
        ],
        compiler_params=_cparams(("arbitrary", "arbitrary", "arbitrary")),
        name="mlstm",
    )(ml, gate_rows, gate_cols, gain_row)


def _ssd_kernel(*refs, nct, nlt):
    xbc_ref = refs[0]
    prev_refs = refs[1:1 + SCAN_BATCH]
    next_refs = refs[1 + SCAN_BATCH:1 + 2 * SCAN_BATCH]
    (z_ref, sr_ref, sc_ref, cw_ref, cb_ref, dsk_ref, gain_ref,
     o_ref, st_ref, xa_ref, yf_ref, ext_ref) = refs[1 + 2 * SCAN_BATCH:]
    d = pl.program_id(1)
    c = pl.program_id(2)

    @pl.when(c == 0)
    def _():
        st_ref[...] = jnp.zeros_like(st_ref)

    p = _scan_chunk(d, c, nct, nlt)
    lane = lax.broadcasted_iota(jnp.int32, (1, LANES), 1)
    low = lane < SSD_HEAD_DIM
    causal, _ = _direction_masks(d)

    @pl.when(d == 0)
    def _():
        has_prev = jnp.where((p != 0) & (p != nct), 1.0, 0.0)
        has_next = jnp.where((p != nct - 1) & (p != nct + nlt - 1), 1.0, 0.0)
        cw = cw_ref[...]
        cbias = cb_ref[...]
        for i in range(SCAN_BATCH):
            rows = slice(i * CHUNK, (i + 1) * CHUNK)
            for lt in range(SSD_XBC // LANES):
                ln = slice(lt * LANES, (lt + 1) * LANES)
                ext_ref[i, lt, 0:HALO, :] = prev_refs[i][:, ln].astype(F32) * has_prev
                ext_ref[i, lt, HALO:HALO + CHUNK, :] = xbc_ref[rows, ln].astype(F32)
                ext_ref[i, lt, HALO + CHUNK:, :] = next_refs[i][:, ln].astype(F32) * has_next
                acc = jnp.zeros((CHUNK, LANES), F32) + cbias[:, ln]
                for tap in range(SSD_CONV):
                    off = HALO - SSD_CONV // 2 + tap
                    acc = acc + ext_ref[i, lt, off:off + CHUNK, :] * cw[tap:tap + 1, ln]
                xa_ref[p, lt, rows, :] = acc * _sigmoid(acc)

    x_tile0, b_tile0, c_tile0 = 0, SSD_INNER // LANES, SSD_INNER // LANES + SSD_GROUPS
    y_dirs = []
    for i in range(SCAN_BATCH):
        rows = slice(i * CHUNK, (i + 1) * CHUNK)
        dt8 = sr_ref[i, 0:8]
        acs8 = sr_ref[i, 8:16]
        dw8 = sr_ref[i, 16:24]
        dec8 = sr_ref[i, 24:32]
        cols = sc_ref[rows, :]

        ys = []
        for g in range(SSD_GROUPS):
            bm_f = xa_ref[p, b_tile0 + g, rows, :]
            cm = xa_ref[p, c_tile0 + g, rows, :].astype(BF16)
            cb = _dot_nt(cm, bm_f.astype(BF16))
            bm_t = bm_f.T
            st = st_ref[i * SSD_GROUPS + g]
            y_state = _dot(cm, st.astype(BF16))
            st_parts = []
            for pair in range(SSD_HPG // 2):
                h0 = g * SSD_HPG + 2 * pair
                x_pair = xa_ref[p, x_tile0 + h0 // 2, rows, :].astype(BF16)
                y_pair = None
                st_pair = None
                grow = []
                for half in range(2):
                    h = h0 + half
                    a_t = _lane_fill(cols, h)
                    e = jnp.exp(jnp.where(causal, a_t - acs8[h:h + 1, :], -jnp.inf))
                    mmat = (cb * e * dt8[h:h + 1, :]).astype(BF16)
                    xm = jnp.where(low if half == 0 else ~low, x_pair, jnp.zeros_like(x_pair))
                    y_h = _dot(mmat, xm)
                    s_h = _dot((bm_t * dw8[h:h + 1, :]).astype(BF16), xm)
                    y_pair = y_h if y_pair is None else y_pair + y_h
                    st_pair = s_h if st_pair is None else st_pair + s_h
                    grow.append(jnp.exp(a_t))
                ys.append(y_pair + jnp.where(low, grow[0], grow[1]) * y_state[:, pair * LANES:(pair + 1) * LANES])
                dec = jnp.where(low, dec8[h0:h0 + 1, :], dec8[h0 + 1:h0 + 2, :])
                st_parts.append(dec * st[:, pair * LANES:(pair + 1) * LANES] + st_pair)
            st_ref[i * SSD_GROUPS + g] = jnp.concatenate(st_parts, axis=1)
        y_dirs.append(jnp.concatenate(ys, axis=1))

    @pl.when(d == 0)
    def _():
        yf_ref[p] = jnp.concatenate(y_dirs, axis=0)

    @pl.when(d == 1)
    def _():
        xs = jnp.concatenate([xa_ref[p, x_tile0 + lt] for lt in range(SSD_INNER // LANES)], axis=1)
        y = yf_ref[p] + jnp.concatenate(y_dirs, axis=0) + dsk_ref[...] * xs
        z = z_ref[...].astype(F32)
        o_ref[...] = _rmsnorm(y * (z * _sigmoid(z)), gain_ref[...]).astype(o_ref.dtype)


def _ssd(z, xbc, gate_rows, gate_cols, conv_w, conv_b, dskip_row, gain_row, dims):
    n_b, nct, nlt = dims["B"], dims["nct"], dims["nlt"]
    nc = nct + nlt
    rows = SCAN_BATCH * CHUNK
    sub = CHUNK // HALO
    n_halo_blocks = dims["n_tok"] // HALO
    blk, fwd_blk, bwd_blk = _scan_block_maps(dims)

    def halo(i, side):
        def idx(bg, d, c):
            group = fwd_blk(bg, d, c) * SCAN_BATCH + i
            if side < 0:
                return (jnp.maximum((group - n_b) * sub + sub - 1, 0), 0)
            return (jnp.minimum((group + n_b) * sub, n_halo_blocks - 1), 0)
        return idx

    const = lambda bg, d, c: (0, 0)
    return pl.pallas_call(
        functools.partial(_ssd_kernel, nct=nct, nlt=nlt),
        out_shape=jax.ShapeDtypeStruct((dims["n_tok"], SSD_INNER), BF16),
        grid=(n_b // SCAN_BATCH, 2, nc),
        in_specs=[pl.BlockSpec((rows, SSD_XBC), lambda bg, d, c: (fwd_blk(bg, d, c), 0))]
        + [pl.BlockSpec((HALO, SSD_XBC), halo(i, -1)) for i in range(SCAN_BATCH)]
        + [pl.BlockSpec((HALO, SSD_XBC), halo(i, +1)) for i in range(SCAN_BATCH)]
        + [
            pl.BlockSpec((rows, SSD_INNER), lambda bg, d, c: (bwd_blk(bg, d, c), 0)),
            pl.BlockSpec((None, SCAN_BATCH, SSD_ROWS, LANES), lambda bg, d, c: (d, blk(bg, d, c), 0, 0)),
            pl.BlockSpec((None, rows, LANES), lambda bg, d, c: (d, blk(bg, d, c), 0)),
            pl.BlockSpec((8, SSD_XBC), const),
            pl.BlockSpec((1, SSD_XBC), const),
            pl.BlockSpec((1, SSD_INNER), const),
            pl.BlockSpec((1, SSD_INNER), const),
        ],
        out_specs=pl.BlockSpec((rows, SSD_INNER), lambda bg, d, c: (bwd_blk(bg, d, c), 0)),
        scratch_shapes=[
            pltpu.VMEM((SCAN_BATCH * SSD_GROUPS, SSD_STATE, SSD_HPG * SSD_HEAD_DIM), F32),
            pltpu.VMEM((nc, SSD_XBC // LANES, rows, LANES), F32),
            pltpu.VMEM((nc, rows, SSD_INNER), F32),
            pltpu.VMEM((SCAN_BATCH, SSD_XBC // LANES, CHUNK + 2 * HALO, LANES), F32),
        ],
        compiler_params=_cparams(("arbitrary", "arbitrary", "arbitrary")),
        name="ssd",
    )(xbc, *([xbc] * (2 * SCAN_BATCH)), z, gate_rows, gate_cols, conv_w, conv_b, dskip_row, gain_row)


def _merge_ffn_kernel(*refs, n_src, ctx_tiles, final_norm):
    x_refs = refs[:n_src]
    (att_ref, ml_ref, ss_ref, gt_ref, mod_ref, g2_ref, gf_ref,
     wa_ref, wm_ref, ws_ref, wo_ref, wup_ref, wdn_ref, o_ref, x1_ref, h_ref) = refs[n_src:]
    is_ctx_tile = pl.program_id(0) < ctx_tiles
    y = None
    for i, (src, w) in enumerate(((att_ref, wa_ref), (ml_ref, wm_ref), (ss_ref, ws_ref))):
        gate = _sigmoid(gt_ref[:, i * D_MODEL:(i + 1) * D_MODEL].astype(F32))
        term = gate * _dot(src[...], w[...])
        y = term if y is None else y + term
    yo = _dot(y.astype(BF16), wo_ref[...])
    g2 = g2_ref[...]
    for k in range(FFN_GROUPS):
        rows = slice(k * CHUNK, (k + 1) * CHUNK)
        mod = mod_ref[k]
        x1 = _read_stream(x_refs, k, is_ctx_tile) + mod[2:3] * yo[rows]
        x1_ref[rows, :] = x1
        h_ref[rows, :] = (_rmsnorm(x1, g2) * (1.0 + mod[4:5]) + mod[3:4]).astype(BF16)
    h = h_ref[...]
    acc = None
    for c0, width in FFN_CHUNKS:
        gate = _dot(h, wup_ref[:, c0:c0 + width])
        up = _dot(h, wup_ref[:, D_FF + c0:D_FF + c0 + width])
        act = (gate * _sigmoid(gate) * up).astype(BF16)
        part = _dot(act, wdn_ref[c0:c0 + width, :])
        acc = part if acc is None else acc + part
    for k in range(FFN_GROUPS):
        rows = slice(k * CHUNK, (k + 1) * CHUNK)
        x2 = x1_ref[rows, :] + mod_ref[k][5:6] * acc[rows]
        if final_norm:
            x2 = _rmsnorm(x2, gf_ref[...])
        o_ref[k] = x2


def _merge_ffn(stream, first, att, ml, ss, gates, mods, layer, g2, gf, wts, dims, last):
    n_b, nct, nlt = dims["B"], dims["nct"], dims["nlt"]
    tm = FFN_GROUPS * CHUNK
    per_chunk = n_b // FFN_GROUPS
    ctx_tiles = nct * per_chunk
    t0 = ctx_tiles if last else 0
    n_tiles = (nct + nlt) * per_chunk - t0
    assert not (first and last)

    row = lambda i: (i + t0, 0)
    att_row = (lambda i: (i, 0)) if last else row
    wa, wm, ws, wo, wup, wdn = wts
    if last:
        out_shape = jax.ShapeDtypeStruct((n_b, nlt, CHUNK, D_MODEL), F32)
        out_spec = pl.BlockSpec((FFN_GROUPS, None, CHUNK, D_MODEL), lambda i: (i % per_chunk, i // per_chunk, 0, 0))
        stream_specs = [pl.BlockSpec((FFN_GROUPS, CHUNK, D_MODEL), lambda i: (i + t0, 0, 0))]
    else:
        out_shape = jax.ShapeDtypeStruct(((nct + nlt) * n_b, CHUNK, D_MODEL), F32)
        out_spec = pl.BlockSpec((FFN_GROUPS, CHUNK, D_MODEL), lambda i: (i, 0, 0))
        stream_specs = _stream_specs(first, FFN_GROUPS, dims)
    return pl.pallas_call(
        functools.partial(_merge_ffn_kernel, n_src=len(stream), ctx_tiles=ctx_tiles - t0, final_norm=last),
        out_shape=out_shape,
        grid=(n_tiles,),
        in_specs=stream_specs + [
            pl.BlockSpec((tm, ATT_Q), att_row),
            pl.BlockSpec((tm, ML_INNER), row),
            pl.BlockSpec((tm, SSD_INNER), row),
            pl.BlockSpec((tm, 3 * D_MODEL), row),
            _mod_spec(layer, FFN_GROUPS, dims, tile0=t0),
            pl.BlockSpec((1, D_MODEL), lambda i: (0, 0)),
            pl.BlockSpec((1, D_MODEL), lambda i: (0, 0)),
            _resident(wa.shape), _resident(wm.shape), _resident(ws.shape),
            _resident(wo.shape), _resident(wup.shape), _resident(wdn.shape),
        ],
        out_specs=out_spec,
        scratch_shapes=[pltpu.VMEM((tm, D_MODEL), F32), pltpu.VMEM((tm, D_MODEL), BF16)],
        compiler_params=_cparams(("arbitrary",)),
        name="merge_ffn",
    )(*stream, att, ml, ss, gates, mods, g2, gf, wa, wm, ws, wo, wup, wdn)


def _rope_tables(seq, nct):
    pos = np.arange(seq)
    row = jnp.asarray((pos // GRID_W).astype(np.float32))
    col = jnp.asarray((pos % GRID_W).astype(np.float32))
    inv = jnp.asarray((ROPE_BASE ** (-np.arange(ROPE_FREQS, dtype=np.float32) / ROPE_FREQS)).astype(np.float32))
    ang_r = row[:, None] * inv
    ang_c = col[:, None] * inv
    cos_h = jnp.concatenate([jnp.cos(ang_r), jnp.cos(ang_r), jnp.cos(ang_c), jnp.cos(ang_c)], axis=1)
    sin_h = jnp.concatenate([-jnp.sin(ang_r), jnp.sin(ang_r), -jnp.sin(ang_c), jnp.sin(ang_c)], axis=1)
    cos_t = jnp.concatenate([cos_h, cos_h], axis=1)
    sin_t = jnp.concatenate([sin_h, sin_h], axis=1)
    cos_t = jnp.concatenate([jnp.ones((nct * CHUNK, LANES), F32), cos_t], axis=0)
    sin_t = jnp.concatenate([jnp.zeros((nct * CHUNK, LANES), F32), sin_t], axis=0)
    return cos_t, sin_t


def _permute_w_in(w):
    d = w.shape[0]
    o_att = 0
    o_ml = ATT_Q + 2 * ATT_KV
    o_mlg = o_ml + 4 * ML_INNER
    o_z = o_mlg + 4 * ML_HEADS
    o_xbc = o_z + SSD_INNER
    o_dt = o_xbc + SSD_XBC
    o_g = o_dt + 2 * SSD_HEADS
    pad = jnp.zeros((d, LANES - 4 * ML_HEADS - 2 * SSD_HEADS), w.dtype)
    cols = [w[:, o_att:o_mlg], w[:, o_z:o_dt], w[:, o_g:], w[:, o_mlg:o_z], w[:, o_dt:o_g], pad]
    return jnp.concatenate([c.astype(BF16) for c in cols], axis=1)


def _lane_row(vals):
    return jnp.zeros((1, LANES), F32).at[0, :vals.shape[0]].set(vals.astype(F32))


def kernel(x, c, ctx, c_ctx, w_mod, b_mod, g_norm1, w_in, att_sink, ml_i_bias, ml_f_bias, ml_head_gain,
           ssd_conv_w, ssd_conv_b, ssd_dt_bias, ssd_a_log, ssd_d, ssd_norm_gain,
           w_att_out, w_ml_out, w_ssd_out, w_o, g_norm2, w_up, w_down, g_final):
    n_b, seq, d_model = x.shape
    lc = ctx.shape[1]
    depth = w_mod.shape[0]
    nct, nlt = lc // CHUNK, seq // CHUNK
    dims = dict(B=n_b, nct=nct, nlt=nlt, n_tok=(nct + nlt) * n_b * CHUNK)
    assert d_model == D_MODEL and seq % CHUNK == 0 and lc % CHUNK == 0 and nlt >= 3 and depth >= 2
    assert n_b % PROJ_GROUPS == 0 and n_b % FFN_GROUPS == 0 and n_b % SCAN_BATCH == 0 and n_b % ATT_BATCH == 0
    assert n_b + PROJ_GROUPS <= MOD_ROWS

    cc = jnp.zeros((MOD_ROWS, d_model), F32).at[:n_b].set(c).at[n_b:n_b + PROJ_GROUPS].set(c_ctx)
    mods = _modulation(cc, w_mod, b_mod).reshape(depth, MOD_ROWS, N_MOD, d_model)
    cos_t, sin_t = _rope_tables(seq, nct)

    stream = [ctx.reshape(n_b, nct, CHUNK, d_model), x.reshape(n_b, nlt, CHUNK, d_model)]
    for l in range(depth):
        first, last = l == 0, l == depth - 1
        w_p = _permute_w_in(w_in[l])
        gbias = jnp.broadcast_to(jnp.concatenate([ml_i_bias[l], ml_f_bias[l]], axis=1).astype(F32)[:, :, None],
                                 (2, 2 * ML_HEADS, LANES))
        a_neg = -jnp.exp(ssd_a_log[l].astype(F32))
        sbias = jnp.broadcast_to(jnp.concatenate([ssd_dt_bias[l].astype(F32), a_neg], axis=0)[:, :, None],
                                 (4, SSD_HEADS, LANES))
        q, kv, ml, z, xbc, gates, ml_rows, ml_cols, ssd_rows, ssd_cols = _in_projection(
            stream, first, mods, l, g_norm1[l].reshape(1, -1), cos_t, sin_t, w_p, gbias, sbias, dims)

        att = _attention(q, kv, _lane_row(att_sink[l]), dims, ctx_queries=not last)

        mlo = _mlstm(ml, ml_rows, ml_cols, ml_head_gain[l].reshape(1, -1), dims)

        conv_w = jnp.zeros((8, SSD_XBC), F32).at[:SSD_CONV].set(ssd_conv_w[l])
        sso = _ssd(z, xbc, ssd_rows, ssd_cols, conv_w, ssd_conv_b[l].reshape(1, -1),
                   jnp.repeat(ssd_d[l].astype(F32), SSD_HEAD_DIM).reshape(1, -1),
                   ssd_norm_gain[l].reshape(1, -1), dims)

        wts = tuple(w.astype(BF16) for w in (w_att_out[l], w_ml_out[l], w_ssd_out[l], w_o[l], w_up[l], w_down[l]))
        out = _merge_ffn(stream, first, att, mlo, sso, gates, mods, l, g_norm2[l].reshape(1, -1),
                         g_final.reshape(1, -1), wts, dims, last)
        stream = [out]
    return out.reshape(n_b, seq, d_model)
```

```python
import functools

import numpy as np
import jax
import jax.numpy as jnp
from jax import lax
from jax.experimental import pallas as pl
from jax.experimental.pallas import tpu as pltpu

F32 = jnp.float32
BF16 = jnp.bfloat16

D_MODEL = 1024
EPS = 1e-6
N_MOD = 6
GRID_W = 64
ROPE_BASE = 10000.0

ATT_HEADS = 8
ATT_KV_HEADS = 2
ATT_GROUP = ATT_HEADS // ATT_KV_HEADS
ATT_HEAD_DIM = 64
ATT_WINDOW = 128
ATT_Q = ATT_HEADS * ATT_HEAD_DIM
ATT_KV = ATT_KV_HEADS * ATT_HEAD_DIM
ROPE_FREQS = ATT_HEAD_DIM // 4

ML_HEADS = 4
ML_HEAD_DIM = 128
ML_INNER = ML_HEADS * ML_HEAD_DIM

SSD_HEADS = 8
SSD_HEAD_DIM = 64
SSD_GROUPS = 2
SSD_HPG = SSD_HEADS // SSD_GROUPS
SSD_STATE = 128
SSD_CONV = 5
SSD_INNER = SSD_HEADS * SSD_HEAD_DIM
SSD_XBC = SSD_INNER + 2 * SSD_GROUPS * SSD_STATE

D_FF = -((-8 * D_MODEL) // (3 * 256)) * 256
MXU_DIM = 256
FFN_CHUNKS = ((0, 5 * MXU_DIM), (5 * MXU_DIM, D_FF - 5 * MXU_DIM))

CHUNK = 128
LANES = 128
HALO = 8
PROJ_GROUPS = 4
FFN_GROUPS = 2
SCAN_BATCH = 2
ML_BATCH = 4
ATT_BATCH = 2
MOD_ROWS = 24
VMEM_LIMIT = 56 * 1024 * 1024

W_Q0 = 0
W_KV0 = ATT_Q
W_ML0 = W_KV0 + 2 * ATT_KV
W_Z0 = W_ML0 + 4 * ML_INNER
W_XBC0 = W_Z0 + SSD_INNER
W_G0 = W_XBC0 + SSD_XBC
W_S0 = W_G0 + 3 * D_MODEL
W_TOT = W_S0 + LANES
DT_LANE0 = 4 * ML_HEADS


def _cparams(sem):
    return pltpu.CompilerParams(dimension_semantics=sem, vmem_limit_bytes=VMEM_LIMIT)


def _resident(stacked_shape, layer):
    nd = len(stacked_shape)
    return pl.BlockSpec((None,) + tuple(stacked_shape[1:]), lambda *_: (layer,) + (0,) * (nd - 1),
                        pipeline_mode=pl.Buffered(1))


def _sigmoid(x):
    return 1.0 / (1.0 + jnp.exp(-x))


def _softplus(x):
    return jnp.maximum(x, 0.0) + jnp.log1p(jnp.exp(-jnp.abs(x)))


def _dot(a, b):
    return jnp.dot(a, b, preferred_element_type=F32)


def _dot_nt(a, b):
    return lax.dot_general(a, b, (((1,), (1,)), ((), ())), preferred_element_type=F32)


def _dot_exact(a, b):
    return jnp.dot(a, b, preferred_element_type=F32, precision=lax.Precision.HIGHEST)


def _rmsnorm(x, gain):
    return x * lax.rsqrt(jnp.mean(x * x, axis=-1, keepdims=True) + EPS) * gain


def _mod_kernel(c_ref, w_ref, b_ref, o_ref):
    c = c_ref[...]
    a = (c * _sigmoid(c)).astype(BF16)
    o_ref[...] = _dot(a, w_ref[...].astype(BF16)) + b_ref[...]


def _modulation(cc, w_mod, b_mod):
    depth, d, n = w_mod.shape
    tn = 3072
    return pl.pallas_call(
        _mod_kernel,
        out_shape=jax.ShapeDtypeStruct((depth, MOD_ROWS, n), F32),
        grid=(depth, n // tn),
        in_specs=[
            pl.BlockSpec((MOD_ROWS, d), lambda l, j: (0, 0)),
            pl.BlockSpec((None, d, tn), lambda l, j: (l, 0, j)),
            pl.BlockSpec((None, 1, tn), lambda l, j: (l, 0, j)),
        ],
        out_specs=pl.BlockSpec((None, MOD_ROWS, tn), lambda l, j: (l, 0, j)),
        compiler_params=_cparams(("arbitrary", "arbitrary")),
        name="modulation",
    )(cc, w_mod, b_mod.reshape(depth, 1, n))


def _stream_specs(first, groups, dims):
    n_b, nct, nlt = dims["B"], dims["nct"], dims["nlt"]
    per_chunk = n_b // groups
    if not first:
        return [pl.BlockSpec((groups, CHUNK, D_MODEL), lambda i: (i, 0, 0))]

    def ctx_idx(i):
        p = jnp.minimum(i // per_chunk, nct - 1)
        return (jnp.where(i < nct * per_chunk, i % per_chunk, per_chunk - 1), p, 0, 0)

    def lat_idx(i):
        p = jnp.maximum(i // per_chunk - nct, 0)
        return (jnp.where(i < nct * per_chunk, 0, i % per_chunk), p, 0, 0)

    return [pl.BlockSpec((groups, None, CHUNK, D_MODEL), ctx_idx),
            pl.BlockSpec((groups, None, CHUNK, D_MODEL), lat_idx)]


def _mod_spec(layer, groups, dims, tile0=0):
    n_b, nct = dims["B"], dims["nct"]
    per_chunk = n_b // groups

    def idx(i):
        i = i + tile0
        return (layer, jnp.where(i < nct * per_chunk, per_chunk, i % per_chunk), 0, 0)

    return pl.BlockSpec((None, groups, N_MOD, D_MODEL), idx)


def _read_stream(refs, k, is_ctx_tile):
    if len(refs) == 1:
        return refs[0][k]
    return jnp.where(is_ctx_tile, refs[0][k], refs[1][k])


def _rope(x, cos, sin_signed, first_half):
    partner = jnp.where(first_half, pltpu.roll(x, LANES - ROPE_FREQS, 1), pltpu.roll(x, ROPE_FREQS, 1))
    return x * cos + partner * sin_signed


def _inproj_kernel(*refs, n_src, ctx_tiles):
    x_refs = refs[:n_src]
    (mod_ref, g_ref, cos_ref, sin_ref, gb_ref, sb_ref, w_ref,
     oq_ref, okv_ref, oml_ref, oz_ref, oxbc_ref, og_ref,
     mr_ref, mc_ref, sr_ref, sc_ref, h_ref) = refs[n_src:]
    is_ctx_tile = pl.program_id(0) < ctx_tiles
    gain = g_ref[...]
    for k in range(PROJ_GROUPS):
        mod = mod_ref[k]
        y = _rmsnorm(_read_stream(x_refs, k, is_ctx_tile), gain)
        h_ref[k * CHUNK:(k + 1) * CHUNK, :] = (y * (1.0 + mod[1:2]) + mod[0:1]).astype(BF16)
    h = h_ref[...]

    cos = jnp.concatenate([cos_ref[...]] * PROJ_GROUPS, axis=0)
    sin = jnp.concatenate([sin_ref[...]] * PROJ_GROUPS, axis=0)
    lane = lax.broadcasted_iota(jnp.int32, (1, LANES), 1)
    first_half = (lane % (2 * ROPE_FREQS)) < ROPE_FREQS

    def proj(c0, width):
        return _dot(h, w_ref[:, c0:c0 + width])

    _gate_prep_tile(proj(W_S0, LANES), gb_ref, sb_ref, mr_ref, mc_ref, sr_ref, sc_ref)

    q = proj(W_Q0, ATT_Q)
    for s in range(ATT_Q // LANES):
        qs = _rope(q[:, s * LANES:(s + 1) * LANES], cos, sin, first_half)
        oq_ref[:, s * LANES:(s + 1) * LANES] = (qs * ATT_HEAD_DIM ** -0.5).astype(oq_ref.dtype)
    kv = proj(W_KV0, 2 * ATT_KV)
    okv_ref[:, 0:ATT_KV] = _rope(kv[:, 0:ATT_KV], cos, sin, first_half).astype(okv_ref.dtype)
    okv_ref[:, ATT_KV:] = kv[:, ATT_KV:].astype(okv_ref.dtype)

    oml_ref[:, 0:ML_INNER] = (proj(W_ML0, ML_INNER) * ML_HEAD_DIM ** -0.5).astype(oml_ref.dtype)
    kproj = proj(W_ML0 + ML_INNER, ML_INNER)
    for r in range(PROJ_GROUPS):
        for hd in range(ML_HEADS):
            blk = kproj[r * CHUNK:(r + 1) * CHUNK, hd * ML_HEAD_DIM:(hd + 1) * ML_HEAD_DIM]
            oml_ref[r * CHUNK:(r + 1) * CHUNK,
                    ML_INNER + hd * ML_HEAD_DIM:ML_INNER + (hd + 1) * ML_HEAD_DIM] = blk.T.astype(oml_ref.dtype)
    for s in range(2, 4):
        oml_ref[:, s * ML_INNER:(s + 1) * ML_INNER] = proj(W_ML0 + s * ML_INNER, ML_INNER).astype(oml_ref.dtype)

    oz_ref[...] = proj(W_Z0, SSD_INNER).astype(oz_ref.dtype)
    for s in range(SSD_XBC // 512):
        oxbc_ref[:, s * 512:(s + 1) * 512] = proj(W_XBC0 + s * 512, 512).astype(oxbc_ref.dtype)
    for s in range(3 * D_MODEL // 512):
        og_ref[:, s * 512:(s + 1) * 512] = proj(W_G0 + s * 512, 512).astype(og_ref.dtype)


def _in_projection(stream, first, mods, layer, g1, cos_t, sin_t, w_p, gbias, sbias, dims):
    n_b, nct = dims["B"], dims["nct"]
    n_tok = dims["n_tok"]
    n_groups = n_tok // CHUNK
    tm = PROJ_GROUPS * CHUNK
    per_chunk = n_b // PROJ_GROUPS
    outs = [
        jax.ShapeDtypeStruct((n_tok, ATT_Q), BF16),
        jax.ShapeDtypeStruct((n_tok, 2 * ATT_KV), BF16),
        jax.ShapeDtypeStruct((n_tok, 4 * ML_INNER), BF16),
        jax.ShapeDtypeStruct((n_tok, SSD_INNER), F32),
        jax.ShapeDtypeStruct((n_tok, SSD_XBC), F32),
        jax.ShapeDtypeStruct((n_tok, 3 * D_MODEL), F32),
    ]
    gate_outs = [
        jax.ShapeDtypeStruct((2, n_groups, ML_ROWS, LANES), F32),
        jax.ShapeDtypeStruct((2, n_tok, LANES), F32),
        jax.ShapeDtypeStruct((2, n_groups, SSD_ROWS, LANES), F32),
        jax.ShapeDtypeStruct((2, n_tok, LANES), F32),
    ]
    gate_specs = [
        pl.BlockSpec((2, PROJ_GROUPS, ML_ROWS, LANES), lambda i: (0, i, 0, 0)),
        pl.BlockSpec((2, tm, LANES), lambda i: (0, i, 0)),
        pl.BlockSpec((2, PROJ_GROUPS, SSD_ROWS, LANES), lambda i: (0, i, 0, 0)),
        pl.BlockSpec((2, tm, LANES), lambda i: (0, i, 0)),
    ]
    return pl.pallas_call(
        functools.partial(_inproj_kernel, n_src=len(stream), ctx_tiles=nct * per_chunk),
        out_shape=outs + gate_outs,
        grid=(n_tok // tm,),
        in_specs=_stream_specs(first, PROJ_GROUPS, dims) + [
            _mod_spec(layer, PROJ_GROUPS, dims),
            pl.BlockSpec((1, D_MODEL), lambda i: (0, 0)),
            pl.BlockSpec((CHUNK, LANES), lambda i: (i // per_chunk, 0)),
            pl.BlockSpec((CHUNK, LANES), lambda i: (i // per_chunk, 0)),
            pl.BlockSpec((2, 8, LANES), lambda i: (0, 0, 0)),
            pl.BlockSpec((4, SSD_HEADS, LANES), lambda i: (0, 0, 0)),
            _resident(w_p.shape, layer),
        ],
        out_specs=[pl.BlockSpec((tm, o.shape[1]), lambda i: (i, 0)) for o in outs] + gate_specs,
        scratch_shapes=[pltpu.VMEM((tm, D_MODEL), BF16)],
        compiler_params=_cparams(("arbitrary",)),
        name="in_projection",
    )(*stream, mods, g1, cos_t, sin_t, gbias, sbias, w_p)


def _attention_block(q, kv, valid, sink, o_ref):
    lane = lax.broadcasted_iota(jnp.int32, (1, LANES), 1)
    low = lane < ATT_HEAD_DIM
    ones = jnp.ones((kv.shape[0], LANES), BF16)
    zero = jnp.zeros((CHUNK, LANES), BF16)
    for g in range(ATT_KV_HEADS):
        k_g = kv[:, g * ATT_HEAD_DIM:(g + 1) * ATT_HEAD_DIM]
        v_g = kv[:, ATT_KV + g * ATT_HEAD_DIM:ATT_KV + (g + 1) * ATT_HEAD_DIM]
        kk = jnp.concatenate([k_g, k_g], axis=1)
        vw = jnp.concatenate([v_g, v_g, ones], axis=1)
        q_rows = []
        for pair in range(ATT_GROUP // 2):
            c0 = (g * ATT_GROUP + 2 * pair) * ATT_HEAD_DIM
            qp = q[:, c0:c0 + LANES]
            q_rows += [jnp.where(low, qp, zero), jnp.where(low, zero, qp)]
        s_all = _dot_nt(jnp.concatenate(q_rows, axis=0), kk)
        p_rows, m_rows, sk_rows = [], [], []
        for r in range(ATT_GROUP):
            s = s_all[r * CHUNK:(r + 1) * CHUNK]
            if valid is not None:
                s = jnp.where(valid, s, -jnp.inf)
            sk = sink[:, g * ATT_GROUP + r:g * ATT_GROUP + r + 1]
            m = jnp.maximum(jnp.max(s, axis=-1, keepdims=True), sk)
            p_rows.append(jnp.exp((s - m).astype(BF16)))
            m_rows.append(m)
            sk_rows.append(sk)
        o_all = _dot(jnp.concatenate(p_rows, axis=0), vw)
        outs = []
        for r in range(ATT_GROUP):
            o = o_all[r * CHUNK:(r + 1) * CHUNK]
            den = o[:, LANES:] + jnp.exp(sk_rows[r] - m_rows[r])
            outs.append(o[:, :LANES] / den)
        for pair in range(ATT_GROUP // 2):
            c0 = (g * ATT_GROUP + 2 * pair) * ATT_HEAD_DIM
            o_ref[:, c0:c0 + LANES] = jnp.where(low, outs[2 * pair], outs[2 * pair + 1]).astype(o_ref.dtype)


def _attention_kernel(*refs, nct, nlt, ctx_queries):
    q_ref = refs[0]
    loc_refs = refs[1:4]
    ctx_refs = refs[4:4 + nct]
    sink_ref, o_ref = refs[4 + nct:]
    step = pl.program_id(1)
    sink = sink_ref[...]

    def rows_of(i):
        return slice(i * CHUNK, (i + 1) * CHUNK)

    def latent(j):
        span = 3 * CHUNK
        first = jnp.clip(j - 1, 0, nlt - 3)
        n_keys = span + nct * CHUNK
        row = lax.broadcasted_iota(jnp.int32, (CHUNK, n_keys), 0)
        col = lax.broadcasted_iota(jnp.int32, (CHUNK, n_keys), 1)
        dist = (j - first) * CHUNK + row - col
        valid = (jnp.abs(dist) <= ATT_WINDOW) | (col >= span)
        for i in range(ATT_BATCH):
            kv = jnp.concatenate([r[rows_of(i), :] for r in loc_refs + ctx_refs], axis=0)
            _attention_block(q_ref[rows_of(i), :], kv, valid, sink, o_ref.at[rows_of(i), :])

    if not ctx_queries:
        latent(step)
        return

    @pl.when(step < nct)
    def _():
        for i in range(ATT_BATCH):
            kv = jnp.concatenate([r[rows_of(i), :] for r in ctx_refs], axis=0)
            _attention_block(q_ref[rows_of(i), :], kv, None, sink, o_ref.at[rows_of(i), :])

    @pl.when(step >= nct)
    def _():
        latent(step - nct)


def _attention(q, kv, sink_row, dims, ctx_queries):
    n_b, nct, nlt = dims["B"], dims["nct"], dims["nlt"]
    c0 = 0 if ctx_queries else nct
    n_steps = nct + nlt - c0

    per_chunk = n_b // ATT_BATCH
    rows = ATT_BATCH * CHUNK

    def win(i):
        def idx(b, s):
            first = jnp.clip(s + c0 - nct - 1, 0, nlt - 3)
            return ((nct + first + i) * per_chunk + b, 0)
        return idx

    return pl.pallas_call(
        functools.partial(_attention_kernel, nct=nct, nlt=nlt, ctx_queries=ctx_queries),
        out_shape=jax.ShapeDtypeStruct((n_steps * n_b * CHUNK, ATT_Q), BF16),
        grid=(per_chunk, n_steps),
        in_specs=[pl.BlockSpec((rows, ATT_Q), lambda b, s: ((s + c0) * per_chunk + b, 0))]
        + [pl.BlockSpec((rows, 2 * ATT_KV), win(i)) for i in range(3)]
        + [pl.BlockSpec((rows, 2 * ATT_KV), functools.partial(lambda b, s, i: (i * per_chunk + b, 0), i=i))
           for i in range(nct)]
        + [pl.BlockSpec((1, LANES), lambda b, s: (0, 0))],
        out_specs=pl.BlockSpec((rows, ATT_Q), lambda b, s: (s * per_chunk + b, 0)),
        compiler_params=_cparams(("arbitrary", "arbitrary")),
        name="attention",
    )(q, *([kv] * (3 + nct)), sink_row)


def _scan_chunk(d, c, nct, nlt):
    fwd = c
    bwd = jnp.where(c < nct, nct - 1 - c, nct + nlt - 1 - (c - nct))
    return jnp.where(d == 0, fwd, bwd)


def _direction_masks(d):
    ri = lax.broadcasted_iota(jnp.int32, (CHUNK, CHUNK), 0)
    ci = lax.broadcasted_iota(jnp.int32, (CHUNK, CHUNK), 1)
    delta = (ci - ri) * (1 - 2 * d)
    return delta <= 0, jnp.where(delta >= 0, 1.0, 0.0).astype(BF16)


def _cumsum_lanes(tiles, tri_t):
    parts = []
    for x8 in tiles:
        hi = x8.astype(BF16).astype(F32)
        r1 = x8 - hi
        mid = r1.astype(BF16).astype(F32)
        parts += [hi, mid, r1 - mid]
    if len(parts) % 2:
        parts.append(jnp.zeros_like(parts[0]))
    out = _dot(jnp.concatenate(parts, axis=0).astype(BF16), tri_t)
    return [out[24 * j:24 * j + 8] + out[24 * j + 8:24 * j + 16] + out[24 * j + 16:24 * j + 24]
            for j in range(len(tiles))]


def _cummax_lanes(x8, backward):
    lane = lax.broadcasted_iota(jnp.int32, (1, LANES), 1)
    k = 1
    while k < LANES:
        if backward:
            shifted = jnp.where(lane < LANES - k, pltpu.roll(x8, LANES - k, 1), -jnp.inf)
        else:
            shifted = jnp.where(lane >= k, pltpu.roll(x8, k, 1), -jnp.inf)
        x8 = jnp.maximum(x8, shifted)
        k *= 2
    return x8


def _rows_to_columns(x8):
    pad = jnp.zeros((LANES - 8, LANES), F32)
    return jnp.concatenate([x8, pad], axis=0).T


def _lane_fill(cols, j):
    return jnp.broadcast_to(cols[:, j:j + 1], cols.shape)


ML_ROWS = 16
SSD_ROWS = 32


def _gate_prep_tile(small, gb_ref, sb_ref, mr_ref, mc_ref, sr_ref, sc_ref):
    sub8 = lax.broadcasted_iota(jnp.int32, (8, 1), 0)
    head_rows = sub8 < ML_HEADS
    masks = [_direction_masks(dd)[1] for dd in range(2)]
    for r in range(small.shape[0] // CHUNK):
        rows = slice(r * CHUNK, (r + 1) * CHUNK)
        gt = small[rows, :].T
        for dd in range(2):
            g8 = gt[8 * dd:8 * dd + 8] + gb_ref[dd]
            g8 = jnp.where(head_rows, g8, jnp.minimum(g8, 0.0) - jnp.log1p(jnp.exp(-jnp.abs(g8))))
            dt8 = _softplus(gt[DT_LANE0 + 8 * dd:DT_LANE0 + 8 * dd + 8] + sb_ref[dd])
            la8 = dt8 * sb_ref[2 + dd]
            gsum, acs8 = _cumsum_lanes([g8, la8], masks[dd])

            b8 = pltpu.roll(gsum, ML_HEADS, 0)
            c8 = g8 - b8
            cm8 = _cummax_lanes(c8, backward=dd == 1)
            cm_end = jnp.broadcast_to(jnp.max(cm8, axis=1, keepdims=True), cm8.shape)
            b_end = jnp.broadcast_to(jnp.sum(g8, axis=1, keepdims=True), g8.shape)
            mr_ref[dd, r, 0:8] = jnp.where(head_rows, c8, 0.0)
            mr_ref[dd, r, 8:16] = jnp.where(head_rows, cm_end, b_end)
            mc_ref[dd, rows, :] = _rows_to_columns(jnp.where(head_rows, cm8, gsum))

            a_end = jnp.sum(la8, axis=1, keepdims=True)
            sr_ref[dd, r, 0:8] = dt8
            sr_ref[dd, r, 8:16] = acs8
            sr_ref[dd, r, 16:24] = dt8 * jnp.exp(a_end - acs8)
            sr_ref[dd, r, 24:32] = jnp.broadcast_to(jnp.exp(a_end), dt8.shape)
            sc_ref[dd, rows, :] = _rows_to_columns(acs8)


def _mlstm_kernel(ml_ref, mr_ref, mc_ref, gain_ref, o_ref, ct_ref, m_ref, hf_ref, *, nct, nlt):
    d = pl.program_id(1)
    c = pl.program_id(2)

    @pl.when(c == 0)
    def _():
        ct_ref[...] = jnp.zeros_like(ct_ref)
        m_ref[...] = jnp.zeros_like(m_ref)

    p = _scan_chunk(d, c, nct, nlt)
    sub8 = lax.broadcasted_iota(jnp.int32, (8, 1), 0)
    causal, _ = _direction_masks(d)
    ones = jnp.ones((CHUNK, ML_HEAD_DIM), BF16)
    gain = gain_ref[...]

    h_dirs = []
    for i in range(ML_BATCH):
        rows = slice(i * CHUNK, (i + 1) * CHUNK)
        c8 = mr_ref[i, 0:8]
        m8 = m_ref[i]
        ends = mr_ref[i, 8:16]
        mx_end = jnp.maximum(ends, m8)
        w8 = jnp.exp(c8 - mx_end)
        dec8 = jnp.exp(m8 - mx_end)
        m_ref[i] = jnp.where(sub8 < ML_HEADS, pltpu.roll(ends, ML_HEADS, 0) + mx_end, 0.0)
        cols = mc_ref[rows, :]

        ml = ml_ref[rows, :]
        hs = []
        for h in range(ML_HEADS):
            st = i * ML_HEADS + h
            q = ml[:, h * ML_HEAD_DIM:(h + 1) * ML_HEAD_DIM]
            k_t = ml[:, ML_INNER + h * ML_HEAD_DIM:ML_INNER + (h + 1) * ML_HEAD_DIM]
            v = ml[:, 2 * ML_INNER + h * ML_HEAD_DIM:2 * ML_INNER + (h + 1) * ML_HEAD_DIM]
            v1 = jnp.concatenate([v, ones], axis=1)
            ctn = ct_ref[st]
            cm_t = _lane_fill(cols, h)
            mx_t = jnp.maximum(cm_t, m8[h:h + 1, :])

            dmat = jnp.exp(jnp.where(causal, c8[h:h + 1, :] - cm_t, -jnp.inf))
            s = (_dot(q, k_t) * dmat).astype(BF16)
            intra = jnp.exp(cm_t - mx_t)
            prev = jnp.exp(m8[h:h + 1, :] - mx_t)
            nd = (jnp.concatenate([intra, intra], axis=1) * _dot(s, v1)
                  + jnp.concatenate([prev, prev], axis=1) * _dot(q, ctn.astype(BF16)))
            floor = jnp.exp(-(_lane_fill(cols, ML_HEADS + h) + mx_t))
            hs.append(nd[:, :ML_HEAD_DIM] / jnp.maximum(jnp.abs(nd[:, ML_HEAD_DIM:]), floor))

            dec = dec8[h:h + 1, :]
            k_w = (k_t.astype(F32) * w8[h:h + 1, :]).astype(BF16)
            ct_ref[st] = jnp.concatenate([dec, dec], axis=1) * ctn + _dot(k_w, v1)

        h_dirs.append(jnp.concatenate(hs, axis=1))

    @pl.when(d == 0)
    def _():
        hf_ref[p] = jnp.concatenate(h_dirs, axis=0)

    @pl.when(d == 1)
    def _():
        tot = hf_ref[p] + jnp.concatenate(h_dirs, axis=0)
        for h in range(ML_HEADS):
            sl = slice(h * ML_HEAD_DIM, (h + 1) * ML_HEAD_DIM)
            o_gate = ml_ref[:, 3 * ML_INNER + h * ML_HEAD_DIM:3 * ML_INNER + (h + 1) * ML_HEAD_DIM].astype(F32)
            o_ref[:, sl] = (_rmsnorm(tot[:, sl], gain[:, sl]) * _sigmoid(o_gate)).astype(o_ref.dtype)


def _scan_block_maps(dims, batch):
    nct, nlt = dims["nct"], dims["nlt"]
    per_chunk = dims["B"] // batch
    nc = nct + nlt

    def blk(bg, d, c):
        return _scan_chunk(d, c, nct, nlt) * per_chunk + bg

    def fwd_blk(bg, d, c):
        return jnp.where(d == 0, blk(bg, 0, c), blk(bg, 0, nc - 1))

    def bwd_blk(bg, d, c):
        return jnp.where(d == 0, blk(bg, 1, 0), blk(bg, 1, c))

    return blk, fwd_blk, bwd_blk


def _mlstm(ml, gate_rows, gate_cols, gain_row, dims):
    nct, nlt = dims["nct"], dims["nlt"]
    nc = nct + nlt
    rows = ML_BATCH * CHUNK
    blk, _, bwd_blk = _scan_block_maps(dims, ML_BATCH)
    return pl.pallas_call(
        functools.partial(_mlstm_kernel, nct=nct, nlt=nlt),
        out_shape=jax.ShapeDtypeStruct((dims["n_tok"], ML_INNER), BF16),
        grid=(dims["B"] // ML_BATCH, 2, nc),
        in_specs=[
            pl.BlockSpec((rows, 4 * ML_INNER), lambda bg, d, c: (blk(bg, d, c), 0)),
            pl.BlockSpec((None, ML_BATCH, ML_ROWS, LANES), lambda bg, d, c: (d, blk(bg, d, c), 0, 0)),
            pl.BlockSpec((None, rows, LANES), lambda bg, d, c: (d, blk(bg, d, c), 0)),
            pl.BlockSpec((1, ML_INNER), lambda bg, d, c: (0, 0)),
        ],
        out_specs=pl.BlockSpec((rows, ML_INNER), lambda bg, d, c: (bwd_blk(bg, d, c), 0)),
        scratch_shapes=[
            pltpu.VMEM((ML_BATCH * ML_HEADS, ML_HEAD_DIM, 2 * ML_HEAD_DIM), F32),
            pltpu.VMEM((ML_BATCH, 8, LANES), F32),
            pltpu.VMEM((nc, rows, ML_INNER), F32),
        ],
        compiler_params=_cparams(("arbitrary", "arbitrary", "arbitrary")),
        name="mlstm",
    )(ml, gate_rows, gate_cols, gain_row)


def _ssd_kernel(*refs, nct, nlt):
    xbc_ref = refs[0]
    prev_refs = refs[1:1 + SCAN_BATCH]
    next_refs = refs[1 + SCAN_BATCH:1 + 2 * SCAN_BATCH]
    (z_ref, sr_ref, sc_ref, cw_ref, cb_ref, dsk_ref, gain_ref,
     o_ref, st_ref, xa_ref, yf_ref, ext_ref) = refs[1 + 2 * SCAN_BATCH:]
    d = pl.program_id(1)
    c = pl.program_id(2)

    @pl.when(c == 0)
    def _():
        st_ref[...] = jnp.zeros_like(st_ref)

    p = _scan_chunk(d, c, nct, nlt)
    lane = lax.broadcasted_iota(jnp.int32, (1, LANES), 1)
    low = lane < SSD_HEAD_DIM
    causal, _ = _direction_masks(d)

    @pl.when(d == 0)
    def _():
        has_prev = jnp.where((p != 0) & (p != nct), 1.0, 0.0)
        has_next = jnp.where((p != nct - 1) & (p != nct + nlt - 1), 1.0, 0.0)
        cw = cw_ref[...]
        cbias = cb_ref[...]
        for i in range(SCAN_BATCH):
            rows = slice(i * CHUNK, (i + 1) * CHUNK)
            for lt in range(SSD_XBC // LANES):
                ln = slice(lt * LANES, (lt + 1) * LANES)
                ext_ref[i, lt, 0:HALO, :] = prev_refs[i][:, ln].astype(F32) * has_prev
                ext_ref[i, lt, HALO:HALO + CHUNK, :] = xbc_ref[rows, ln].astype(F32)
                ext_ref[i, lt, HALO + CHUNK:, :] = next_refs[i][:, ln].astype(F32) * has_next
                acc = jnp.zeros((CHUNK, LANES), F32) + cbias[:, ln]
                for tap in range(SSD_CONV):
                    off = HALO - SSD_CONV // 2 + tap
                    acc = acc + ext_ref[i, lt, off:off + CHUNK, :] * cw[tap:tap + 1, ln]
                xa_ref[p, lt, rows, :] = acc * _sigmoid(acc)

    x_tile0, b_tile0, c_tile0 = 0, SSD_INNER // LANES, SSD_INNER // LANES + SSD_GROUPS
    y_dirs = []
    for i in range(SCAN_BATCH):
        rows = slice(i * CHUNK, (i + 1) * CHUNK)
        dt8 = sr_ref[i, 0:8]
        acs8 = sr_ref[i, 8:16]
        dw8 = sr_ref[i, 16:24]
        dec8 = sr_ref[i, 24:32]
        cols = sc_ref[rows, :]

        ys = []
        for g in range(SSD_GROUPS):
            bm_f = xa_ref[p, b_tile0 + g, rows, :]
            cm = xa_ref[p, c_tile0 + g, rows, :].astype(BF16)
            cb = _dot_nt(cm, bm_f.astype(BF16))
            bm_t = bm_f.T
            st = st_ref[i * SSD_GROUPS + g]
            y_state = _dot(cm, st.astype(BF16))
            st_parts = []
            for pair in range(SSD_HPG // 2):
                h0 = g * SSD_HPG + 2 * pair
                x_pair = xa_ref[p, x_tile0 + h0 // 2, rows, :].astype(BF16)
                y_pair = None
                st_pair = None
                grow = []
                for half in range(2):
                    h = h0 + half
                    a_t = _lane_fill(cols, h)
                    e = jnp.exp(jnp.where(causal, a_t - acs8[h:h + 1, :], -jnp.inf))
                    mmat = (cb * e * dt8[h:h + 1, :]).astype(BF16)
                    xm = jnp.where(low if half == 0 else ~low, x_pair, jnp.zeros_like(x_pair))
                    y_h = _dot(mmat, xm)
                    s_h = _dot((bm_t * dw8[h:h + 1, :]).astype(BF16), xm)
                    y_pair = y_h if y_pair is None else y_pair + y_h
                    st_pair = s_h if st_pair is None else st_pair + s_h
                    grow.append(jnp.exp(a_t))
                ys.append(y_pair + jnp.where(low, grow[0], grow[1]) * y_state[:, pair * LANES:(pair + 1) * LANES])
                dec = jnp.where(low, dec8[h0:h0 + 1, :], dec8[h0 + 1:h0 + 2, :])
                st_parts.append(dec * st[:, pair * LANES:(pair + 1) * LANES] + st_pair)
            st_ref[i * SSD_GROUPS + g] = jnp.concatenate(st_parts, axis=1)
        y_dirs.append(jnp.concatenate(ys, axis=1))

    @pl.when(d == 0)
    def _():
        yf_ref[p] = jnp.concatenate(y_dirs, axis=0)

    @pl.when(d == 1)
    def _():
        xs = jnp.concatenate([xa_ref[p, x_tile0 + lt] for lt in range(SSD_INNER // LANES)], axis=1)
        y = yf_ref[p] + jnp.concatenate(y_dirs, axis=0) + dsk_ref[...] * xs
        z = z_ref[...].astype(F32)
        o_ref[...] = _rmsnorm(y * (z * _sigmoid(z)), gain_ref[...]).astype(o_ref.dtype)


def _ssd(z, xbc, gate_rows, gate_cols, conv_w, conv_b, dskip_row, gain_row, dims):
    n_b, nct, nlt = dims["B"], dims["nct"], dims["nlt"]
    nc = nct + nlt
    rows = SCAN_BATCH * CHUNK
    sub = CHUNK // HALO
    n_halo_blocks = dims["n_tok"] // HALO
    blk, fwd_blk, bwd_blk = _scan_block_maps(dims, SCAN_BATCH)

    def halo(i, side):
        def idx(bg, d, c):
            group = fwd_blk(bg, d, c) * SCAN_BATCH + i
            if side < 0:
                return (jnp.maximum((group - n_b) * sub + sub - 1, 0), 0)
            return (jnp.minimum((group + n_b) * sub, n_halo_blocks - 1), 0)
        return idx

    const = lambda bg, d, c: (0, 0)
    return pl.pallas_call(
        functools.partial(_ssd_kernel, nct=nct, nlt=nlt),
        out_shape=jax.ShapeDtypeStruct((dims["n_tok"], SSD_INNER), BF16),
        grid=(n_b // SCAN_BATCH, 2, nc),
        in_specs=[pl.BlockSpec((rows, SSD_XBC), lambda bg, d, c: (fwd_blk(bg, d, c), 0))]
        + [pl.BlockSpec((HALO, SSD_XBC), halo(i, -1)) for i in range(SCAN_BATCH)]
        + [pl.BlockSpec((HALO, SSD_XBC), halo(i, +1)) for i in range(SCAN_BATCH)]
        + [
            pl.BlockSpec((rows, SSD_INNER), lambda bg, d, c: (bwd_blk(bg, d, c), 0)),
            pl.BlockSpec((None, SCAN_BATCH, SSD_ROWS, LANES), lambda bg, d, c: (d, blk(bg, d, c), 0, 0)),
            pl.BlockSpec((None, rows, LANES), lambda bg, d, c: (d, blk(bg, d, c), 0)),
            pl.BlockSpec((8, SSD_XBC), const),
            pl.BlockSpec((1, SSD_XBC), const),
            pl.BlockSpec((1, SSD_INNER), const),
            pl.BlockSpec((1, SSD_INNER), const),
        ],
        out_specs=pl.BlockSpec((rows, SSD_INNER), lambda bg, d, c: (bwd_blk(bg, d, c), 0)),
        scratch_shapes=[
            pltpu.VMEM((SCAN_BATCH * SSD_GROUPS, SSD_STATE, SSD_HPG * SSD_HEAD_DIM), F32),
            pltpu.VMEM((nc, SSD_XBC // LANES, rows, LANES), F32),
            pltpu.VMEM((nc, rows, SSD_INNER), F32),
            pltpu.VMEM((SCAN_BATCH, SSD_XBC // LANES, CHUNK + 2 * HALO, LANES), F32),
        ],
        compiler_params=_cparams(("arbitrary", "arbitrary", "arbitrary")),
        name="ssd",
    )(xbc, *([xbc] * (2 * SCAN_BATCH)), z, gate_rows, gate_cols, conv_w, conv_b, dskip_row, gain_row)


def _merge_ffn_kernel(*refs, n_src, ctx_tiles, final_norm):
    x_refs = refs[:n_src]
    (att_ref, ml_ref, ss_ref, gt_ref, mod_ref, g2_ref, gf_ref,
     wa_ref, wm_ref, ws_ref, wo_ref, wup_ref, wdn_ref, o_ref, x1_ref, h_ref) = refs[n_src:]
    is_ctx_tile = pl.program_id(0) < ctx_tiles
    y = None
    for i, (src, w) in enumerate(((att_ref, wa_ref), (ml_ref, wm_ref), (ss_ref, ws_ref))):
        gate = _sigmoid(gt_ref[:, i * D_MODEL:(i + 1) * D_MODEL].astype(F32))
        term = gate * _dot(src[...], w[...])
        y = term if y is None else y + term
    yo = _dot(y.astype(BF16), wo_ref[...])
    g2 = g2_ref[...]
    for k in range(FFN_GROUPS):
        rows = slice(k * CHUNK, (k + 1) * CHUNK)
        mod = mod_ref[k]
        x1 = _read_stream(x_refs, k, is_ctx_tile) + mod[2:3] * yo[rows]
        x1_ref[rows, :] = x1
        h_ref[rows, :] = (_rmsnorm(x1, g2) * (1.0 + mod[4:5]) + mod[3:4]).astype(BF16)
    h = h_ref[...]
    acc = None
    for c0, width in FFN_CHUNKS:
        gate = _dot(h, wup_ref[:, c0:c0 + width])
        up = _dot(h, wup_ref[:, D_FF + c0:D_FF + c0 + width])
        act = (gate * _sigmoid(gate) * up).astype(BF16)
        part = _dot(act, wdn_ref[c0:c0 + width, :])
        acc = part if acc is None else acc + part
    for k in range(FFN_GROUPS):
        rows = slice(k * CHUNK, (k + 1) * CHUNK)
        x2 = x1_ref[rows, :] + mod_ref[k][5:6] * acc[rows]
        if final_norm:
            x2 = _rmsnorm(x2, gf_ref[...])
        o_ref[k] = x2


def _merge_ffn(stream, first, att, ml, ss, gates, mods, layer, g2, gf, wts, dims, last):
    n_b, nct, nlt = dims["B"], dims["nct"], dims["nlt"]
    tm = FFN_GROUPS * CHUNK
    per_chunk = n_b // FFN_GROUPS
    ctx_tiles = nct * per_chunk
    t0 = ctx_tiles if last else 0
    n_tiles = (nct + nlt) * per_chunk - t0
    assert not (first and last)

    row = lambda i: (i + t0, 0)
    att_row = (lambda i: (i, 0)) if last else row
    wa, wm, ws, wo, wup, wdn = wts
    if last:
        out_shape = jax.ShapeDtypeStruct((n_b, nlt, CHUNK, D_MODEL), F32)
        out_spec = pl.BlockSpec((FFN_GROUPS, None, CHUNK, D_MODEL), lambda i: (i % per_chunk, i // per_chunk, 0, 0))
        stream_specs = [pl.BlockSpec((FFN_GROUPS, CHUNK, D_MODEL), lambda i: (i + t0, 0, 0))]
    else:
        out_shape = jax.ShapeDtypeStruct(((nct + nlt) * n_b, CHUNK, D_MODEL), F32)
        out_spec = pl.BlockSpec((FFN_GROUPS, CHUNK, D_MODEL), lambda i: (i, 0, 0))
        stream_specs = _stream_specs(first, FFN_GROUPS, dims)
    return pl.pallas_call(
        functools.partial(_merge_ffn_kernel, n_src=len(stream), ctx_tiles=ctx_tiles - t0, final_norm=last),
        out_shape=out_shape,
        grid=(n_tiles,),
        in_specs=stream_specs + [
            pl.BlockSpec((tm, ATT_Q), att_row),
            pl.BlockSpec((tm, ML_INNER), row),
            pl.BlockSpec((tm, SSD_INNER), row),
            pl.BlockSpec((tm, 3 * D_MODEL), row),
            _mod_spec(layer, FFN_GROUPS, dims, tile0=t0),
            pl.BlockSpec((1, D_MODEL), lambda i: (0, 0)),
            pl.BlockSpec((1, D_MODEL), lambda i: (0, 0)),
            _resident(wa.shape, layer), _resident(wm.shape, layer), _resident(ws.shape, layer),
            _resident(wo.shape, layer), _resident(wup.shape, layer), _resident(wdn.shape, layer),
        ],
        out_specs=out_spec,
        scratch_shapes=[pltpu.VMEM((tm, D_MODEL), F32), pltpu.VMEM((tm, D_MODEL), BF16)],
        compiler_params=_cparams(("arbitrary",)),
        name="merge_ffn",
    )(*stream, att, ml, ss, gates, mods, g2, gf, wa, wm, ws, wo, wup, wdn)


def _rope_tables(seq, nct):
    pos = np.arange(seq)
    row = (pos // GRID_W).astype(np.float32)
    col = (pos % GRID_W).astype(np.float32)
    inv = (np.float32(ROPE_BASE) ** (-np.arange(ROPE_FREQS, dtype=np.float32) / np.float32(ROPE_FREQS))).astype(np.float32)
    ang_r = (row[:, None] * inv).astype(np.float32)
    ang_c = (col[:, None] * inv).astype(np.float32)
    cos_h = np.concatenate([np.cos(ang_r), np.cos(ang_r), np.cos(ang_c), np.cos(ang_c)], axis=1)
    sin_h = np.concatenate([-np.sin(ang_r), np.sin(ang_r), -np.sin(ang_c), np.sin(ang_c)], axis=1)
    cos_t = np.concatenate([np.ones((nct * CHUNK, LANES)), np.tile(cos_h, (1, 2))], axis=0)
    sin_t = np.concatenate([np.zeros((nct * CHUNK, LANES)), np.tile(sin_h, (1, 2))], axis=0)
    return jnp.asarray(cos_t, F32), jnp.asarray(sin_t, F32)


def _permute_w_in(w):
    o_att = 0
    o_ml = ATT_Q + 2 * ATT_KV
    o_mlg = o_ml + 4 * ML_INNER
    o_z = o_mlg + 4 * ML_HEADS
    o_xbc = o_z + SSD_INNER
    o_dt = o_xbc + SSD_XBC
    o_g = o_dt + 2 * SSD_HEADS
    pad = jnp.zeros(w.shape[:2] + (LANES - 4 * ML_HEADS - 2 * SSD_HEADS,), w.dtype)
    cols = [w[..., o_att:o_mlg], w[..., o_z:o_dt], w[..., o_g:], w[..., o_mlg:o_z], w[..., o_dt:o_g], pad]
    return jnp.concatenate([c.astype(BF16) for c in cols], axis=-1)


def _lane_row(vals):
    return jnp.zeros((1, LANES), F32).at[0, :vals.shape[0]].set(vals.astype(F32))


def kernel(x, c, ctx, c_ctx, w_mod, b_mod, g_norm1, w_in, att_sink, ml_i_bias, ml_f_bias, ml_head_gain,
           ssd_conv_w, ssd_conv_b, ssd_dt_bias, ssd_a_log, ssd_d, ssd_norm_gain,
           w_att_out, w_ml_out, w_ssd_out, w_o, g_norm2, w_up, w_down, g_final):
    n_b, seq, d_model = x.shape
    lc = ctx.shape[1]
    depth = w_mod.shape[0]
    nct, nlt = lc // CHUNK, seq // CHUNK
    dims = dict(B=n_b, nct=nct, nlt=nlt, n_tok=(nct + nlt) * n_b * CHUNK)
    assert d_model == D_MODEL and seq % CHUNK == 0 and lc % CHUNK == 0 and nlt >= 3 and depth >= 2
    assert n_b % PROJ_GROUPS == 0 and n_b % FFN_GROUPS == 0 and n_b % SCAN_BATCH == 0 and n_b % ATT_BATCH == 0
    assert n_b % ML_BATCH == 0
    assert n_b + PROJ_GROUPS <= MOD_ROWS

    cc = jnp.zeros((MOD_ROWS, d_model), F32).at[:n_b].set(c).at[n_b:n_b + PROJ_GROUPS].set(c_ctx)
    mods = _modulation(cc, w_mod, b_mod).reshape(depth, MOD_ROWS, N_MOD, d_model)
    cos_t, sin_t = _rope_tables(seq, nct)
    w_p = _permute_w_in(w_in)
    wts = tuple(w.astype(BF16) for w in (w_att_out, w_ml_out, w_ssd_out, w_o, w_up, w_down))

    stream = [ctx.reshape(n_b, nct, CHUNK, d_model), x.reshape(n_b, nlt, CHUNK, d_model)]
    for l in range(depth):
        first, last = l == 0, l == depth - 1
        gbias = jnp.broadcast_to(jnp.concatenate([ml_i_bias[l], ml_f_bias[l]], axis=1).astype(F32)[:, :, None],
                                 (2, 2 * ML_HEADS, LANES))
        a_neg = -jnp.exp(ssd_a_log[l].astype(F32))
        sbias = jnp.broadcast_to(jnp.concatenate([ssd_dt_bias[l].astype(F32), a_neg], axis=0)[:, :, None],
                                 (4, SSD_HEADS, LANES))
        q, kv, ml, z, xbc, gates, ml_rows, ml_cols, ssd_rows, ssd_cols = _in_projection(
            stream, first, mods, l, g_norm1[l].reshape(1, -1), cos_t, sin_t, w_p, gbias, sbias, dims)

        att = _attention(q, kv, _lane_row(att_sink[l]), dims, ctx_queries=not last)

        mlo = _mlstm(ml, ml_rows, ml_cols, ml_head_gain[l].reshape(1, -1), dims)

        conv_w = jnp.zeros((8, SSD_XBC), F32).at[:SSD_CONV].set(ssd_conv_w[l])
        sso = _ssd(z, xbc, ssd_rows, ssd_cols, conv_w, ssd_conv_b[l].reshape(1, -1),
                   jnp.repeat(ssd_d[l].astype(F32), SSD_HEAD_DIM).reshape(1, -1),
                   ssd_norm_gain[l].reshape(1, -1), dims)

        out = _merge_ffn(stream, first, att, mlo, sso, gates, mods, l, g_norm2[l].reshape(1, -1),
                         g_final.reshape(1, -1), wts, dims, last)
        stream = [out]
    return out.reshape(n_b, seq, d_model)
```

```python
import functools

import numpy as np
import jax
import jax.numpy as jnp
from jax import lax
from jax.experimental import pallas as pl
from jax.experimental.pallas import tpu as pltpu

F32 = jnp.float32
BF16 = jnp.bfloat16

D_MODEL = 1024
EPS = 1e-6
N_MOD = 6
GRID_W = 64
ROPE_BASE = 10000.0

ATT_HEADS = 8
ATT_KV_HEADS = 2
ATT_GROUP = ATT_HEADS // ATT_KV_HEADS
ATT_HEAD_DIM = 64
ATT_WINDOW = 128
ATT_Q = ATT_HEADS * ATT_HEAD_DIM
ATT_KV = ATT_KV_HEADS * ATT_HEAD_DIM
ROPE_FREQS = ATT_HEAD_DIM // 4

ML_HEADS = 4
ML_HEAD_DIM = 128
ML_INNER = ML_HEADS * ML_HEAD_DIM

SSD_HEADS = 8
SSD_HEAD_DIM = 64
SSD_GROUPS = 2
SSD_HPG = SSD_HEADS // SSD_GROUPS
SSD_STATE = 128
SSD_CONV = 5
SSD_INNER = SSD_HEADS * SSD_HEAD_DIM
SSD_XBC = SSD_INNER + 2 * SSD_GROUPS * SSD_STATE

D_FF = -((-8 * D_MODEL) // (3 * 256)) * 256
MXU_DIM = 256
FFN_CHUNKS = ((0, 5 * MXU_DIM), (5 * MXU_DIM, D_FF - 5 * MXU_DIM))

CHUNK = 128
LANES = 128
HALO = 8
PROJ_GROUPS = 4
FFN_GROUPS = 2
SCAN_BATCH = 2
ML_BATCH = 4
ATT_BATCH = 4
MOD_ROWS = 24
VMEM_LIMIT = 56 * 1024 * 1024

W_Q0 = 0
W_KV0 = ATT_Q
W_ML0 = W_KV0 + 2 * ATT_KV
W_MIX = W_ML0 + 4 * ML_INNER
IN_MLG0 = W_MIX
IN_Z0 = IN_MLG0 + 4 * ML_HEADS
IN_DT0 = IN_Z0 + SSD_INNER + SSD_XBC
IN_G0 = IN_DT0 + 2 * SSD_HEADS
DT_LANE0 = 4 * ML_HEADS


def _cparams(sem):
    return pltpu.CompilerParams(dimension_semantics=sem, vmem_limit_bytes=VMEM_LIMIT)


def _resident(stacked_shape, layer):
    nd = len(stacked_shape)
    return pl.BlockSpec((None,) + tuple(stacked_shape[1:]), lambda *_: (layer,) + (0,) * (nd - 1),
                        pipeline_mode=pl.Buffered(1))


def _sigmoid(x):
    return 1.0 / (1.0 + jnp.exp(-x))


def _softplus(x):
    return jnp.maximum(x, 0.0) + jnp.log1p(jnp.exp(-jnp.abs(x)))


def _dot(a, b):
    return jnp.dot(a, b, preferred_element_type=F32)


def _dot_nt(a, b):
    return lax.dot_general(a, b, (((1,), (1,)), ((), ())), preferred_element_type=F32)


def _rmsnorm(x, gain):
    return x * lax.rsqrt(jnp.mean(x * x, axis=-1, keepdims=True) + EPS) * gain


def _mod_kernel(c_ref, w_ref, b_ref, o_ref):
    c = c_ref[...]
    a = (c * _sigmoid(c)).astype(BF16)
    o_ref[...] = _dot(a, w_ref[...].astype(BF16)) + b_ref[...]


def _modulation(cc, w_mod, b_mod):
    depth, d, n = w_mod.shape
    tn = 3072
    return pl.pallas_call(
        _mod_kernel,
        out_shape=jax.ShapeDtypeStruct((depth, MOD_ROWS, n), F32),
        grid=(depth, n // tn),
        in_specs=[
            pl.BlockSpec((MOD_ROWS, d), lambda l, j: (0, 0)),
            pl.BlockSpec((None, d, tn), lambda l, j: (l, 0, j)),
            pl.BlockSpec((None, 1, tn), lambda l, j: (l, 0, j)),
        ],
        out_specs=pl.BlockSpec((None, MOD_ROWS, tn), lambda l, j: (l, 0, j)),
        compiler_params=_cparams(("arbitrary", "arbitrary")),
        name="modulation",
    )(cc, w_mod, b_mod.reshape(depth, 1, n))


def _stream_specs(first, groups, dims):
    n_b, nct, nlt = dims["B"], dims["nct"], dims["nlt"]
    per_chunk = n_b // groups
    if not first:
        return [pl.BlockSpec((groups, CHUNK, D_MODEL), lambda i: (i, 0, 0))]

    def ctx_idx(i):
        p = jnp.minimum(i // per_chunk, nct - 1)
        return (jnp.where(i < nct * per_chunk, i % per_chunk, per_chunk - 1), p, 0, 0)

    def lat_idx(i):
        p = jnp.maximum(i // per_chunk - nct, 0)
        return (jnp.where(i < nct * per_chunk, 0, i % per_chunk), p, 0, 0)

    return [pl.BlockSpec((groups, None, CHUNK, D_MODEL), ctx_idx),
            pl.BlockSpec((groups, None, CHUNK, D_MODEL), lat_idx)]


def _mod_spec(layer, groups, dims, tile0=0):
    n_b, nct = dims["B"], dims["nct"]
    per_chunk = n_b // groups

    def idx(i):
        i = i + tile0
        return (layer, jnp.where(i < nct * per_chunk, per_chunk, i % per_chunk), 0, 0)

    return pl.BlockSpec((None, groups, N_MOD, D_MODEL), idx)


def _read_stream(refs, k, is_ctx_tile):
    if len(refs) == 1:
        return refs[0][k]
    return jnp.where(is_ctx_tile, refs[0][k], refs[1][k])


def _rope(x, cos, sin_signed, first_half):
    partner = jnp.where(first_half, pltpu.roll(x, LANES - ROPE_FREQS, 1), pltpu.roll(x, ROPE_FREQS, 1))
    return x * cos + partner * sin_signed


def _inproj_kernel(*refs, n_src, ctx_tiles):
    x_refs = refs[:n_src]
    (mod_ref, g_ref, cos_ref, sin_ref, gb_ref, sb_ref, w_mix_ref, w_ssd_ref, w_merge_ref, w_small_ref,
     oq_ref, okv_ref, oml_ref, oz_ref, oxbc_ref, og_ref,
     mr_ref, mc_ref, sr_ref, sc_ref, h_ref) = refs[n_src:]
    is_ctx_tile = pl.program_id(0) < ctx_tiles
    gain = g_ref[...]
    for k in range(PROJ_GROUPS):
        mod = mod_ref[k]
        y = _rmsnorm(_read_stream(x_refs, k, is_ctx_tile), gain)
        h_ref[k * CHUNK:(k + 1) * CHUNK, :] = (y * (1.0 + mod[1:2]) + mod[0:1]).astype(BF16)
    h = h_ref[...]

    cos = jnp.concatenate([cos_ref[...]] * PROJ_GROUPS, axis=0)
    sin = jnp.concatenate([sin_ref[...]] * PROJ_GROUPS, axis=0)
    lane = lax.broadcasted_iota(jnp.int32, (1, LANES), 1)
    first_half = (lane % (2 * ROPE_FREQS)) < ROPE_FREQS

    def proj(c0, width, w_ref=w_mix_ref):
        return _dot(h, w_ref[:, c0:c0 + width])

    _gate_prep_tile(proj(0, LANES, w_small_ref), gb_ref, sb_ref, mr_ref, mc_ref, sr_ref, sc_ref)

    q = proj(W_Q0, ATT_Q)
    for s in range(ATT_Q // LANES):
        qs = _rope(q[:, s * LANES:(s + 1) * LANES], cos, sin, first_half)
        oq_ref[:, s * LANES:(s + 1) * LANES] = (qs * ATT_HEAD_DIM ** -0.5).astype(oq_ref.dtype)
    kv = proj(W_KV0, 2 * ATT_KV)
    okv_ref[:, 0:ATT_KV] = _rope(kv[:, 0:ATT_KV], cos, sin, first_half).astype(okv_ref.dtype)
    okv_ref[:, ATT_KV:] = kv[:, ATT_KV:].astype(okv_ref.dtype)

    oml_ref[:, 0:ML_INNER] = (proj(W_ML0, ML_INNER) * ML_HEAD_DIM ** -0.5).astype(oml_ref.dtype)
    kproj = proj(W_ML0 + ML_INNER, ML_INNER)
    for r in range(PROJ_GROUPS):
        for hd in range(ML_HEADS):
            blk = kproj[r * CHUNK:(r + 1) * CHUNK, hd * ML_HEAD_DIM:(hd + 1) * ML_HEAD_DIM]
            oml_ref[r * CHUNK:(r + 1) * CHUNK,
                    ML_INNER + hd * ML_HEAD_DIM:ML_INNER + (hd + 1) * ML_HEAD_DIM] = blk.T.astype(oml_ref.dtype)
    for s in range(2, 4):
        oml_ref[:, s * ML_INNER:(s + 1) * ML_INNER] = proj(W_ML0 + s * ML_INNER, ML_INNER).astype(oml_ref.dtype)

    oz_ref[...] = proj(0, SSD_INNER, w_ssd_ref).astype(oz_ref.dtype)
    for s in range(SSD_XBC // 512):
        oxbc_ref[:, s * 512:(s + 1) * 512] = proj(SSD_INNER + s * 512, 512, w_ssd_ref).astype(oxbc_ref.dtype)
    for s in range(3 * D_MODEL // 512):
        og_ref[:, s * 512:(s + 1) * 512] = proj(s * 512, 512, w_merge_ref).astype(og_ref.dtype)


def _in_projection(stream, first, mods, layer, g1, cos_t, sin_t, w_pieces, gbias, sbias, dims):
    n_b, nct = dims["B"], dims["nct"]
    n_tok = dims["n_tok"]
    n_groups = n_tok // CHUNK
    tm = PROJ_GROUPS * CHUNK
    per_chunk = n_b // PROJ_GROUPS
    outs = [
        jax.ShapeDtypeStruct((n_tok, ATT_Q), BF16),
        jax.ShapeDtypeStruct((n_tok, 2 * ATT_KV), BF16),
        jax.ShapeDtypeStruct((n_tok, 4 * ML_INNER), BF16),
        jax.ShapeDtypeStruct((n_tok, SSD_INNER), F32),
        jax.ShapeDtypeStruct((n_tok, SSD_XBC), F32),
        jax.ShapeDtypeStruct((n_tok, 3 * D_MODEL), F32),
    ]
    gate_outs = [
        jax.ShapeDtypeStruct((2, n_groups, ML_ROWS, LANES), F32),
        jax.ShapeDtypeStruct((2, n_tok, LANES), F32),
        jax.ShapeDtypeStruct((2, n_groups, SSD_ROWS, LANES), F32),
        jax.ShapeDtypeStruct((2, n_tok, LANES), F32),
    ]
    gate_specs = [
        pl.BlockSpec((2, PROJ_GROUPS, ML_ROWS, LANES), lambda i: (0, i, 0, 0)),
        pl.BlockSpec((2, tm, LANES), lambda i: (0, i, 0)),
        pl.BlockSpec((2, PROJ_GROUPS, SSD_ROWS, LANES), lambda i: (0, i, 0, 0)),
        pl.BlockSpec((2, tm, LANES), lambda i: (0, i, 0)),
    ]
    return pl.pallas_call(
        functools.partial(_inproj_kernel, n_src=len(stream), ctx_tiles=nct * per_chunk),
        out_shape=outs + gate_outs,
        grid=(n_tok // tm,),
        in_specs=_stream_specs(first, PROJ_GROUPS, dims) + [
            _mod_spec(layer, PROJ_GROUPS, dims),
            pl.BlockSpec((1, D_MODEL), lambda i: (0, 0)),
            pl.BlockSpec((CHUNK, LANES), lambda i: (i // per_chunk, 0)),
            pl.BlockSpec((CHUNK, LANES), lambda i: (i // per_chunk, 0)),
            pl.BlockSpec((2, 8, LANES), lambda i: (0, 0, 0)),
            pl.BlockSpec((4, SSD_HEADS, LANES), lambda i: (0, 0, 0)),
        ] + [_resident(w.shape, layer) for w in w_pieces],
        out_specs=[pl.BlockSpec((tm, o.shape[1]), lambda i: (i, 0)) for o in outs] + gate_specs,
        scratch_shapes=[pltpu.VMEM((tm, D_MODEL), BF16)],
        compiler_params=_cparams(("arbitrary",)),
        name="in_projection",
    )(*stream, mods, g1, cos_t, sin_t, gbias, sbias, *w_pieces)


def _attention_blocks(qs, kvs, valid, sink, o_refs):
    lane = lax.broadcasted_iota(jnp.int32, (1, LANES), 1)
    low = lane < ATT_HEAD_DIM
    ones = jnp.ones((kvs[0].shape[0], LANES), BF16)
    zero = jnp.zeros((CHUNK, LANES), BF16)
    units = [(i, g) for i in range(len(qs)) for g in range(ATT_KV_HEADS)]

    scores, values = [], []
    for i, g in units:
        k_g = kvs[i][:, g * ATT_HEAD_DIM:(g + 1) * ATT_HEAD_DIM]
        v_g = kvs[i][:, ATT_KV + g * ATT_HEAD_DIM:ATT_KV + (g + 1) * ATT_HEAD_DIM]
        kk = jnp.concatenate([k_g, k_g], axis=1)
        values.append(jnp.concatenate([v_g, v_g, ones], axis=1))
        q_rows = []
        for pair in range(ATT_GROUP // 2):
            c0 = (g * ATT_GROUP + 2 * pair) * ATT_HEAD_DIM
            qp = qs[i][:, c0:c0 + LANES]
            q_rows += [jnp.where(low, qp, zero), jnp.where(low, zero, qp)]
        scores.append(_dot_nt(jnp.concatenate(q_rows, axis=0), kk))

    probs, sink_terms = [], []
    for (i, g), s_all in zip(units, scores):
        p_rows, t_rows = [], []
        for r in range(ATT_GROUP):
            s = s_all[r * CHUNK:(r + 1) * CHUNK]
            if valid is not None:
                s = jnp.where(valid, s, -jnp.inf)
            sk = sink[:, g * ATT_GROUP + r:g * ATT_GROUP + r + 1]
            m = jnp.maximum(jnp.max(s, axis=-1, keepdims=True), sk)
            p_rows.append(jnp.exp((s - m).astype(BF16)))
            t_rows.append(jnp.exp(sk - m))
        probs.append(jnp.concatenate(p_rows, axis=0))
        sink_terms.append(t_rows)

    outs = [_dot(p, vw) for p, vw in zip(probs, values)]

    for (i, g), o_all, t_rows in zip(units, outs, sink_terms):
        heads = []
        for r in range(ATT_GROUP):
            o = o_all[r * CHUNK:(r + 1) * CHUNK]
            heads.append(o[:, :LANES] / (o[:, LANES:] + t_rows[r]))
        for pair in range(ATT_GROUP // 2):
            c0 = (g * ATT_GROUP + 2 * pair) * ATT_HEAD_DIM
            o_refs[i][:, c0:c0 + LANES] = jnp.where(low, heads[2 * pair], heads[2 * pair + 1]).astype(o_refs[i].dtype)


def _attention_kernel(*refs, nct, nlt, ctx_queries):
    q_ref = refs[0]
    loc_refs = refs[1:4]
    ctx_refs = refs[4:4 + nct]
    sink_ref, o_ref = refs[4 + nct:]
    step = pl.program_id(1)
    sink = sink_ref[...]

    def rows_of(i):
        return slice(i * CHUNK, (i + 1) * CHUNK)

    def latent(j):
        span = 3 * CHUNK
        first = jnp.clip(j - 1, 0, nlt - 3)
        n_keys = span + nct * CHUNK
        row = lax.broadcasted_iota(jnp.int32, (CHUNK, n_keys), 0)
        col = lax.broadcasted_iota(jnp.int32, (CHUNK, n_keys), 1)
        dist = (j - first) * CHUNK + row - col
        valid = (jnp.abs(dist) <= ATT_WINDOW) | (col >= span)
        kvs = [jnp.concatenate([r[rows_of(i), :] for r in loc_refs + ctx_refs], axis=0) for i in range(ATT_BATCH)]
        _attention_blocks([q_ref[rows_of(i), :] for i in range(ATT_BATCH)], kvs, valid, sink,
                          [o_ref.at[rows_of(i), :] for i in range(ATT_BATCH)])

    if not ctx_queries:
        latent(step)
        return

    @pl.when(step < nct)
    def _():
        kvs = [jnp.concatenate([r[rows_of(i), :] for r in ctx_refs], axis=0) for i in range(ATT_BATCH)]
        _attention_blocks([q_ref[rows_of(i), :] for i in range(ATT_BATCH)], kvs, None, sink,
                          [o_ref.at[rows_of(i), :] for i in range(ATT_BATCH)])

    @pl.when(step >= nct)
    def _():
        latent(step - nct)


def _attention(q, kv, sink_row, dims, ctx_queries):
    n_b, nct, nlt = dims["B"], dims["nct"], dims["nlt"]
    c0 = 0 if ctx_queries else nct
    n_steps = nct + nlt - c0

    per_chunk = n_b // ATT_BATCH
    rows = ATT_BATCH * CHUNK

    def win(i):
        def idx(b, s):
            first = jnp.clip(s + c0 - nct - 1, 0, nlt - 3)
            return ((nct + first + i) * per_chunk + b, 0)
        return idx

    return pl.pallas_call(
        functools.partial(_attention_kernel, nct=nct, nlt=nlt, ctx_queries=ctx_queries),
        out_shape=jax.ShapeDtypeStruct((n_steps * n_b * CHUNK, ATT_Q), BF16),
        grid=(per_chunk, n_steps),
        in_specs=[pl.BlockSpec((rows, ATT_Q), lambda b, s: ((s + c0) * per_chunk + b, 0))]
        + [pl.BlockSpec((rows, 2 * ATT_KV), win(i)) for i in range(3)]
        + [pl.BlockSpec((rows, 2 * ATT_KV), functools.partial(lambda b, s, i: (i * per_chunk + b, 0), i=i))
           for i in range(nct)]
        + [pl.BlockSpec((1, LANES), lambda b, s: (0, 0))],
        out_specs=pl.BlockSpec((rows, ATT_Q), lambda b, s: (s * per_chunk + b, 0)),
        compiler_params=_cparams(("arbitrary", "arbitrary")),
        name="attention",
    )(q, *([kv] * (3 + nct)), sink_row)


def _scan_chunk(d, c, nct, nlt):
    fwd = c
    bwd = jnp.where(c < nct, nct - 1 - c, nct + nlt - 1 - (c - nct))
    return jnp.where(d == 0, fwd, bwd)


def _causal_mask(d):
    ri = lax.broadcasted_iota(jnp.int32, (CHUNK, CHUNK), 0)
    ci = lax.broadcasted_iota(jnp.int32, (CHUNK, CHUNK), 1)
    return (ci - ri) * (1 - 2 * d) <= 0


def _scan_lanes(x8, backward, combine, identity):
    lane = lax.broadcasted_iota(jnp.int32, (1, LANES), 1)
    k = 1
    while k < LANES:
        if backward:
            shifted = jnp.where(lane < LANES - k, pltpu.roll(x8, LANES - k, 1), identity)
        else:
            shifted = jnp.where(lane >= k, pltpu.roll(x8, k, 1), identity)
        x8 = combine(x8, shifted)
        k *= 2
    return x8


def _cummax_lanes(x8, backward):
    return _scan_lanes(x8, backward, jnp.maximum, -jnp.inf)


def _rows_to_columns(x8):
    pad = jnp.zeros((LANES - 8, LANES), F32)
    return jnp.concatenate([x8, pad], axis=0).T


def _lane_fill(cols, j):
    return jnp.broadcast_to(cols[:, j:j + 1], cols.shape)


ML_ROWS = 16
SSD_ROWS = 32


def _gate_prep_tile(small, gb_ref, sb_ref, mr_ref, mc_ref, sr_ref, sc_ref):
    sub8 = lax.broadcasted_iota(jnp.int32, (8, 1), 0)
    head_rows = sub8 < ML_HEADS
    for r in range(small.shape[0] // CHUNK):
        rows = slice(r * CHUNK, (r + 1) * CHUNK)
        gt = small[rows, :].T
        for dd in range(2):
            g8 = gt[8 * dd:8 * dd + 8] + gb_ref[dd]
            g8 = jnp.where(head_rows, g8, jnp.minimum(g8, 0.0) - jnp.log1p(jnp.exp(-jnp.abs(g8))))
            dt8 = _softplus(gt[DT_LANE0 + 8 * dd:DT_LANE0 + 8 * dd + 8] + sb_ref[dd])
            la8 = dt8 * sb_ref[2 + dd]
            gsum = _scan_lanes(g8, dd == 1, jnp.add, 0.0)
            acs8 = _scan_lanes(la8, dd == 1, jnp.add, 0.0)

            b8 = pltpu.roll(gsum, ML_HEADS, 0)
            c8 = g8 - b8
            cm8 = _cummax_lanes(c8, backward=dd == 1)
            cm_end = jnp.broadcast_to(jnp.max(cm8, axis=1, keepdims=True), cm8.shape)
            b_end = jnp.broadcast_to(jnp.sum(g8, axis=1, keepdims=True), g8.shape)
            mr_ref[dd, r, 0:8] = jnp.where(head_rows, c8, 0.0)
            mr_ref[dd, r, 8:16] = jnp.where(head_rows, cm_end, b_end)
            mc_ref[dd, rows, :] = _rows_to_columns(jnp.where(head_rows, cm8, gsum))

            a_end = jnp.sum(la8, axis=1, keepdims=True)
            sr_ref[dd, r, 0:8] = dt8
            sr_ref[dd, r, 8:16] = acs8
            sr_ref[dd, r, 16:24] = dt8 * jnp.exp(a_end - acs8)
            sr_ref[dd, r, 24:32] = jnp.broadcast_to(jnp.exp(a_end), dt8.shape)
            sc_ref[dd, rows, :] = _rows_to_columns(acs8)


def _mlstm_kernel(ml_ref, mr_ref, mc_ref, gain_ref, o_ref, ct_ref, m_ref, hf_ref, *, nct, nlt):
    d = pl.program_id(1)
    c = pl.program_id(2)

    @pl.when(c == 0)
    def _():
        ct_ref[...] = jnp.zeros_like(ct_ref)
        m_ref[...] = jnp.zeros_like(m_ref)

    p = _scan_chunk(d, c, nct, nlt)
    sub8 = lax.broadcasted_iota(jnp.int32, (8, 1), 0)
    causal = _causal_mask(d)
    ones = jnp.ones((CHUNK, ML_HEAD_DIM), BF16)
    gain = gain_ref[...]

    rows_of = [slice(i * CHUNK, (i + 1) * CHUNK) for i in range(ML_BATCH)]
    units = [(i, h) for i in range(ML_BATCH) for h in range(ML_HEADS)]
    c8, m8, w8, dec8, cols = [], [], [], [], []
    for i in range(ML_BATCH):
        c8.append(mr_ref[i, 0:8])
        m8.append(m_ref[i])
        ends = mr_ref[i, 8:16]
        mx_end = jnp.maximum(ends, m8[i])
        w8.append(jnp.exp(c8[i] - mx_end))
        dec8.append(jnp.exp(m8[i] - mx_end))
        m_ref[i] = jnp.where(sub8 < ML_HEADS, pltpu.roll(ends, ML_HEADS, 0) + mx_end, 0.0)
        cols.append(mc_ref[rows_of[i], :])

    def piece(i, h, which):
        c0 = which * ML_INNER + h * ML_HEAD_DIM
        return ml_ref[rows_of[i], c0:c0 + ML_HEAD_DIM]

    q = {u: piece(*u, 0) for u in units}
    k_t = {u: piece(*u, 1) for u in units}
    v1 = {u: jnp.concatenate([piece(*u, 2), ones], axis=1) for u in units}
    ctn = {(i, h): ct_ref[i * ML_HEADS + h] for i, h in units}

    qk = {u: _dot(q[u], k_t[u]) for u in units}
    qc = {u: _dot(q[u], ctn[u].astype(BF16)) for u in units}

    s, k_w, cm_t = {}, {}, {}
    for i, h in units:
        cm_t[i, h] = _lane_fill(cols[i], h)
        dmat = jnp.exp(jnp.where(causal, c8[i][h:h + 1, :] - cm_t[i, h], -jnp.inf))
        s[i, h] = (qk[i, h] * dmat).astype(BF16)
        k_w[i, h] = (k_t[i, h].astype(F32) * w8[i][h:h + 1, :]).astype(BF16)

    sv = {u: _dot(s[u], v1[u]) for u in units}
    kv = {u: _dot(k_w[u], v1[u]) for u in units}

    h_dirs = []
    for i in range(ML_BATCH):
        hs = []
        for h in range(ML_HEADS):
            m_row = m8[i][h:h + 1, :]
            mx_t = jnp.maximum(cm_t[i, h], m_row)
            intra = jnp.exp(cm_t[i, h] - mx_t)
            prev = jnp.exp(m_row - mx_t)
            nd = (jnp.concatenate([intra, intra], axis=1) * sv[i, h]
                  + jnp.concatenate([prev, prev], axis=1) * qc[i, h])
            floor = jnp.exp(-(_lane_fill(cols[i], ML_HEADS + h) + mx_t))
            hs.append(nd[:, :ML_HEAD_DIM] / jnp.maximum(jnp.abs(nd[:, ML_HEAD_DIM:]), floor))
            dec = dec8[i][h:h + 1, :]
            ct_ref[i * ML_HEADS + h] = jnp.concatenate([dec, dec], axis=1) * ctn[i, h] + kv[i, h]
        h_dirs.append(jnp.concatenate(hs, axis=1))

    @pl.when(d == 0)
    def _():
        hf_ref[p] = jnp.concatenate(h_dirs, axis=0)

    @pl.when(d == 1)
    def _():
        tot = hf_ref[p] + jnp.concatenate(h_dirs, axis=0)
        for h in range(ML_HEADS):
            sl = slice(h * ML_HEAD_DIM, (h + 1) * ML_HEAD_DIM)
            o_gate = ml_ref[:, 3 * ML_INNER + h * ML_HEAD_DIM:3 * ML_INNER + (h + 1) * ML_HEAD_DIM].astype(F32)
            o_ref[:, sl] = (_rmsnorm(tot[:, sl], gain[:, sl]) * _sigmoid(o_gate)).astype(o_ref.dtype)


def _scan_block_maps(dims, batch):
    nct, nlt = dims["nct"], dims["nlt"]
    per_chunk = dims["B"] // batch
    nc = nct + nlt

    def blk(bg, d, c):
        return _scan_chunk(d, c, nct, nlt) * per_chunk + bg

    def fwd_blk(bg, d, c):
        return jnp.where(d == 0, blk(bg, 0, c), blk(bg, 0, nc - 1))

    def bwd_blk(bg, d, c):
        return jnp.where(d == 0, blk(bg, 1, 0), blk(bg, 1, c))

    return blk, fwd_blk, bwd_blk


def _mlstm(ml, gate_rows, gate_cols, gain_row, dims):
    nct, nlt = dims["nct"], dims["nlt"]
    nc = nct + nlt
    rows = ML_BATCH * CHUNK
    blk, _, bwd_blk = _scan_block_maps(dims, ML_BATCH)
    return pl.pallas_call(
        functools.partial(_mlstm_kernel, nct=nct, nlt=nlt),
        out_shape=jax.ShapeDtypeStruct((dims["n_tok"], ML_INNER), BF16),
        grid=(dims["B"] // ML_BATCH, 2, nc),
        in_specs=[
            pl.BlockSpec((rows, 4 * ML_INNER), lambda bg, d, c: (blk(bg, d, c), 0)),
            pl.BlockSpec((None, ML_BATCH, ML_ROWS, LANES), lambda bg, d, c: (d, blk(bg, d, c), 0, 0)),
            pl.BlockSpec((None, rows, LANES), lambda bg, d, c: (d, blk(bg, d, c), 0)),
            pl.BlockSpec((1, ML_INNER), lambda bg, d, c: (0, 0)),
        ],
        out_specs=pl.BlockSpec((rows, ML_INNER), lambda bg, d, c: (bwd_blk(bg, d, c), 0)),
        scratch_shapes=[
            pltpu.VMEM((ML_BATCH * ML_HEADS, ML_HEAD_DIM, 2 * ML_HEAD_DIM), F32),
            pltpu.VMEM((ML_BATCH, 8, LANES), F32),
            pltpu.VMEM((nc, rows, ML_INNER), F32),
        ],
        compiler_params=_cparams(("arbitrary", "arbitrary", "arbitrary")),
        name="mlstm",
    )(ml, gate_rows, gate_cols, gain_row)


def _ssd_kernel(*refs, nct, nlt):
    xbc_ref = refs[0]
    prev_refs = refs[1:1 + SCAN_BATCH]
    next_refs = refs[1 + SCAN_BATCH:1 + 2 * SCAN_BATCH]
    (z_ref, sr_ref, sc_ref, cw_ref, cb_ref, dsk_ref, gain_ref,
     o_ref, st_ref, xa_ref, yf_ref, ext_ref) = refs[1 + 2 * SCAN_BATCH:]
    d = pl.program_id(1)
    c = pl.program_id(2)

    @pl.when(c == 0)
    def _():
        st_ref[...] = jnp.zeros_like(st_ref)

    p = _scan_chunk(d, c, nct, nlt)
    lane = lax.broadcasted_iota(jnp.int32, (1, LANES), 1)
    low = lane < SSD_HEAD_DIM
    causal = _causal_mask(d)

    @pl.when(d == 0)
    def _():
        has_prev = jnp.where((p != 0) & (p != nct), 1.0, 0.0)
        has_next = jnp.where((p != nct - 1) & (p != nct + nlt - 1), 1.0, 0.0)
        cw = cw_ref[...]
        cbias = cb_ref[...]
        for i in range(SCAN_BATCH):
            rows = slice(i * CHUNK, (i + 1) * CHUNK)
            for lt in range(SSD_XBC // LANES):
                ln = slice(lt * LANES, (lt + 1) * LANES)
                ext_ref[i, lt, 0:HALO, :] = prev_refs[i][:, ln].astype(F32) * has_prev
                ext_ref[i, lt, HALO:HALO + CHUNK, :] = xbc_ref[rows, ln].astype(F32)
                ext_ref[i, lt, HALO + CHUNK:, :] = next_refs[i][:, ln].astype(F32) * has_next
                acc = jnp.zeros((CHUNK, LANES), F32) + cbias[:, ln]
                for tap in range(SSD_CONV):
                    off = HALO - SSD_CONV // 2 + tap
                    acc = acc + ext_ref[i, lt, off:off + CHUNK, :] * cw[tap:tap + 1, ln]
                xa_ref[p, lt, rows, :] = acc * _sigmoid(acc)

    x_tile0, b_tile0, c_tile0 = 0, SSD_INNER // LANES, SSD_INNER // LANES + SSD_GROUPS
    rows_of = [slice(i * CHUNK, (i + 1) * CHUNK) for i in range(SCAN_BATCH)]
    groups = [(i, g) for i in range(SCAN_BATCH) for g in range(SSD_GROUPS)]
    heads = [(i, h) for i in range(SCAN_BATCH) for h in range(SSD_HEADS)]
    dt8 = [sr_ref[i, 0:8] for i in range(SCAN_BATCH)]
    acs8 = [sr_ref[i, 8:16] for i in range(SCAN_BATCH)]
    dw8 = [sr_ref[i, 16:24] for i in range(SCAN_BATCH)]
    dec8 = [sr_ref[i, 24:32] for i in range(SCAN_BATCH)]
    cols = [sc_ref[rows_of[i], :] for i in range(SCAN_BATCH)]

    cb, bm_t, st, y_state = {}, {}, {}, {}
    for i, g in groups:
        bm_f = xa_ref[p, b_tile0 + g, rows_of[i], :]
        cm = xa_ref[p, c_tile0 + g, rows_of[i], :].astype(BF16)
        cb[i, g] = _dot_nt(cm, bm_f.astype(BF16))
        bm_t[i, g] = bm_f.T
        st[i, g] = st_ref[i * SSD_GROUPS + g]
        y_state[i, g] = _dot(cm, st[i, g].astype(BF16))

    mmat, bw, xm, grow = {}, {}, {}, {}
    for i, h in heads:
        g = h // SSD_HPG
        a_t = _lane_fill(cols[i], h)
        e = jnp.exp(jnp.where(causal, a_t - acs8[i][h:h + 1, :], -jnp.inf))
        mmat[i, h] = (cb[i, g] * e * dt8[i][h:h + 1, :]).astype(BF16)
        bw[i, h] = (bm_t[i, g] * dw8[i][h:h + 1, :]).astype(BF16)
        x_pair = xa_ref[p, x_tile0 + h // 2, rows_of[i], :].astype(BF16)
        xm[i, h] = jnp.where(low if h % 2 == 0 else ~low, x_pair, jnp.zeros_like(x_pair))
        grow[i, h] = jnp.exp(a_t)

    y_h = {u: _dot(mmat[u], xm[u]) for u in heads}
    s_h = {u: _dot(bw[u], xm[u]) for u in heads}

    y_dirs = []
    for i in range(SCAN_BATCH):
        ys = []
        for g in range(SSD_GROUPS):
            st_parts = []
            for pair in range(SSD_HPG // 2):
                h0 = g * SSD_HPG + 2 * pair
                lanes = slice(pair * LANES, (pair + 1) * LANES)
                ys.append(y_h[i, h0] + y_h[i, h0 + 1]
                          + jnp.where(low, grow[i, h0], grow[i, h0 + 1]) * y_state[i, g][:, lanes])
                dec = jnp.where(low, dec8[i][h0:h0 + 1, :], dec8[i][h0 + 1:h0 + 2, :])
                st_parts.append(dec * st[i, g][:, lanes] + s_h[i, h0] + s_h[i, h0 + 1])
            st_ref[i * SSD_GROUPS + g] = jnp.concatenate(st_parts, axis=1)
        y_dirs.append(jnp.concatenate(ys, axis=1))

    @pl.when(d == 0)
    def _():
        yf_ref[p] = jnp.concatenate(y_dirs, axis=0)

    @pl.when(d == 1)
    def _():
        xs = jnp.concatenate([xa_ref[p, x_tile0 + lt] for lt in range(SSD_INNER // LANES)], axis=1)
        y = yf_ref[p] + jnp.concatenate(y_dirs, axis=0) + dsk_ref[...] * xs
        z = z_ref[...].astype(F32)
        o_ref[...] = _rmsnorm(y * (z * _sigmoid(z)), gain_ref[...]).astype(o_ref.dtype)


def _ssd(z, xbc, gate_rows, gate_cols, conv_w, conv_b, dskip_row, gain_row, dims):
    n_b, nct, nlt = dims["B"], dims["nct"], dims["nlt"]
    nc = nct + nlt
    rows = SCAN_BATCH * CHUNK
    sub = CHUNK // HALO
    n_halo_blocks = dims["n_tok"] // HALO
    blk, fwd_blk, bwd_blk = _scan_block_maps(dims, SCAN_BATCH)

    def halo(i, side):
        def idx(bg, d, c):
            group = fwd_blk(bg, d, c) * SCAN_BATCH + i
            if side < 0:
                return (jnp.maximum((group - n_b) * sub + sub - 1, 0), 0)
            return (jnp.minimum((group + n_b) * sub, n_halo_blocks - 1), 0)
        return idx

    const = lambda bg, d, c: (0, 0)
    return pl.pallas_call(
        functools.partial(_ssd_kernel, nct=nct, nlt=nlt),
        out_shape=jax.ShapeDtypeStruct((dims["n_tok"], SSD_INNER), BF16),
        grid=(n_b // SCAN_BATCH, 2, nc),
        in_specs=[pl.BlockSpec((rows, SSD_XBC), lambda bg, d, c: (fwd_blk(bg, d, c), 0))]
        + [pl.BlockSpec((HALO, SSD_XBC), halo(i, -1)) for i in range(SCAN_BATCH)]
        + [pl.BlockSpec((HALO, SSD_XBC), halo(i, +1)) for i in range(SCAN_BATCH)]
        + [
            pl.BlockSpec((rows, SSD_INNER), lambda bg, d, c: (bwd_blk(bg, d, c), 0)),
            pl.BlockSpec((None, SCAN_BATCH, SSD_ROWS, LANES), lambda bg, d, c: (d, blk(bg, d, c), 0, 0)),
            pl.BlockSpec((None, rows, LANES), lambda bg, d, c: (d, blk(bg, d, c), 0)),
            pl.BlockSpec((8, SSD_XBC), const),
            pl.BlockSpec((1, SSD_XBC), const),
            pl.BlockSpec((1, SSD_INNER), const),
            pl.BlockSpec((1, SSD_INNER), const),
        ],
        out_specs=pl.BlockSpec((rows, SSD_INNER), lambda bg, d, c: (bwd_blk(bg, d, c), 0)),
        scratch_shapes=[
            pltpu.VMEM((SCAN_BATCH * SSD_GROUPS, SSD_STATE, SSD_HPG * SSD_HEAD_DIM), F32),
            pltpu.VMEM((nc, SSD_XBC // LANES, rows, LANES), F32),
            pltpu.VMEM((nc, rows, SSD_INNER), F32),
            pltpu.VMEM((SCAN_BATCH, SSD_XBC // LANES, CHUNK + 2 * HALO, LANES), F32),
        ],
        compiler_params=_cparams(("arbitrary", "arbitrary", "arbitrary")),
        name="ssd",
    )(xbc, *([xbc] * (2 * SCAN_BATCH)), z, gate_rows, gate_cols, conv_w, conv_b, dskip_row, gain_row)


def _merge_ffn_kernel(*refs, n_src, ctx_tiles, final_norm):
    x_refs = refs[:n_src]
    (att_ref, ml_ref, ss_ref, gt_ref, mod_ref, g2_ref, gf_ref,
     wa_ref, wm_ref, ws_ref, wo_ref, wup_ref, wdn_ref, o_ref, x1_ref, h_ref) = refs[n_src:]
    is_ctx_tile = pl.program_id(0) < ctx_tiles
    y = None
    for i, (src, w) in enumerate(((att_ref, wa_ref), (ml_ref, wm_ref), (ss_ref, ws_ref))):
        gate = _sigmoid(gt_ref[:, i * D_MODEL:(i + 1) * D_MODEL].astype(F32))
        term = gate * _dot(src[...], w[...])
        y = term if y is None else y + term
    yo = _dot(y.astype(BF16), wo_ref[...])
    g2 = g2_ref[...]
    for k in range(FFN_GROUPS):
        rows = slice(k * CHUNK, (k + 1) * CHUNK)
        mod = mod_ref[k]
        x1 = _read_stream(x_refs, k, is_ctx_tile) + mod[2:3] * yo[rows]
        x1_ref[rows, :] = x1
        h_ref[rows, :] = (_rmsnorm(x1, g2) * (1.0 + mod[4:5]) + mod[3:4]).astype(BF16)
    h = h_ref[...]
    acc = None
    for c0, width in FFN_CHUNKS:
        gate = _dot(h, wup_ref[:, c0:c0 + width])
        up = _dot(h, wup_ref[:, D_FF + c0:D_FF + c0 + width])
        act = (gate * _sigmoid(gate) * up).astype(BF16)
        part = _dot(act, wdn_ref[c0:c0 + width, :])
        acc = part if acc is None else acc + part
    for k in range(FFN_GROUPS):
        rows = slice(k * CHUNK, (k + 1) * CHUNK)
        x2 = x1_ref[rows, :] + mod_ref[k][5:6] * acc[rows]
        if final_norm:
            x2 = _rmsnorm(x2, gf_ref[...])
        o_ref[k] = x2


def _merge_ffn(stream, first, att, ml, ss, gates, mods, layer, g2, gf, wts, dims, last):
    n_b, nct, nlt = dims["B"], dims["nct"], dims["nlt"]
    tm = FFN_GROUPS * CHUNK
    per_chunk = n_b // FFN_GROUPS
    ctx_tiles = nct * per_chunk
    t0 = ctx_tiles if last else 0
    n_tiles = (nct + nlt) * per_chunk - t0
    assert not (first and last)

    row = lambda i: (i + t0, 0)
    att_row = (lambda i: (i, 0)) if last else row
    wa, wm, ws, wo, wup, wdn = wts
    if last:
        out_shape = jax.ShapeDtypeStruct((n_b, nlt, CHUNK, D_MODEL), F32)
        out_spec = pl.BlockSpec((FFN_GROUPS, None, CHUNK, D_MODEL), lambda i: (i % per_chunk, i // per_chunk, 0, 0))
        stream_specs = [pl.BlockSpec((FFN_GROUPS, CHUNK, D_MODEL), lambda i: (i + t0, 0, 0))]
    else:
        out_shape = jax.ShapeDtypeStruct(((nct + nlt) * n_b, CHUNK, D_MODEL), F32)
        out_spec = pl.BlockSpec((FFN_GROUPS, CHUNK, D_MODEL), lambda i: (i, 0, 0))
        stream_specs = _stream_specs(first, FFN_GROUPS, dims)
    return pl.pallas_call(
        functools.partial(_merge_ffn_kernel, n_src=len(stream), ctx_tiles=ctx_tiles - t0, final_norm=last),
        out_shape=out_shape,
        grid=(n_tiles,),
        in_specs=stream_specs + [
            pl.BlockSpec((tm, ATT_Q), att_row),
            pl.BlockSpec((tm, ML_INNER), row),
            pl.BlockSpec((tm, SSD_INNER), row),
            pl.BlockSpec((tm, 3 * D_MODEL), row),
            _mod_spec(layer, FFN_GROUPS, dims, tile0=t0),
            pl.BlockSpec((1, D_MODEL), lambda i: (0, 0)),
            pl.BlockSpec((1, D_MODEL), lambda i: (0, 0)),
            _resident(wa.shape, layer), _resident(wm.shape, layer), _resident(ws.shape, layer),
            _resident(wo.shape, layer), _resident(wup.shape, layer), _resident(wdn.shape, layer),
        ],
        out_specs=out_spec,
        scratch_shapes=[pltpu.VMEM((tm, D_MODEL), F32), pltpu.VMEM((tm, D_MODEL), BF16)],
        compiler_params=_cparams(("arbitrary",)),
        name="merge_ffn",
    )(*stream, att, ml, ss, gates, mods, g2, gf, wa, wm, ws, wo, wup, wdn)


def _rope_tables(seq, nct):
    pos = np.arange(seq)
    row = (pos // GRID_W).astype(np.float32)
    col = (pos % GRID_W).astype(np.float32)
    inv = (np.float32(ROPE_BASE) ** (-np.arange(ROPE_FREQS, dtype=np.float32) / np.float32(ROPE_FREQS))).astype(np.float32)
    ang_r = (row[:, None] * inv).astype(np.float32)
    ang_c = (col[:, None] * inv).astype(np.float32)
    cos_h = np.concatenate([np.cos(ang_r), np.cos(ang_r), np.cos(ang_c), np.cos(ang_c)], axis=1)
    sin_h = np.concatenate([-np.sin(ang_r), np.sin(ang_r), -np.sin(ang_c), np.sin(ang_c)], axis=1)
    cos_t = np.concatenate([np.ones((nct * CHUNK, LANES)), np.tile(cos_h, (1, 2))], axis=0)
    sin_t = np.concatenate([np.zeros((nct * CHUNK, LANES)), np.tile(sin_h, (1, 2))], axis=0)
    return jnp.asarray(cos_t, F32), jnp.asarray(sin_t, F32)


def _split_w_in(w):
    pad = jnp.zeros(w.shape[:2] + (LANES - 4 * ML_HEADS - 2 * SSD_HEADS,), w.dtype)
    small = jnp.concatenate([w[..., IN_MLG0:IN_Z0], w[..., IN_DT0:IN_G0], pad], axis=-1)
    return [w[..., :W_MIX].astype(BF16), w[..., IN_Z0:IN_DT0].astype(BF16), w[..., IN_G0:].astype(BF16),
            small.astype(BF16)]


def _lane_row(vals):
    return jnp.zeros((1, LANES), F32).at[0, :vals.shape[0]].set(vals.astype(F32))


def kernel(x, c, ctx, c_ctx, w_mod, b_mod, g_norm1, w_in, att_sink, ml_i_bias, ml_f_bias, ml_head_gain,
           ssd_conv_w, ssd_conv_b, ssd_dt_bias, ssd_a_log, ssd_d, ssd_norm_gain,
           w_att_out, w_ml_out, w_ssd_out, w_o, g_norm2, w_up, w_down, g_final):
    n_b, seq, d_model = x.shape
    lc = ctx.shape[1]
    depth = w_mod.shape[0]
    nct, nlt = lc // CHUNK, seq // CHUNK
    dims = dict(B=n_b, nct=nct, nlt=nlt, n_tok=(nct + nlt) * n_b * CHUNK)
    assert d_model == D_MODEL and seq % CHUNK == 0 and lc % CHUNK == 0 and nlt >= 3 and depth >= 2
    assert n_b % PROJ_GROUPS == 0 and n_b % FFN_GROUPS == 0 and n_b % SCAN_BATCH == 0 and n_b % ATT_BATCH == 0
    assert n_b % ML_BATCH == 0
    assert n_b + PROJ_GROUPS <= MOD_ROWS

    cc = jnp.zeros((MOD_ROWS, d_model), F32).at[:n_b].set(c).at[n_b:n_b + PROJ_GROUPS].set(c_ctx)
    mods = _modulation(cc, w_mod, b_mod).reshape(depth, MOD_ROWS, N_MOD, d_model)
    cos_t, sin_t = _rope_tables(seq, nct)
    w_pieces = _split_w_in(w_in)
    wts = tuple(w.astype(BF16) for w in (w_att_out, w_ml_out, w_ssd_out, w_o, w_up, w_down))

    stream = [ctx.reshape(n_b, nct, CHUNK, d_model), x.reshape(n_b, nlt, CHUNK, d_model)]
    for l in range(depth):
        first, last = l == 0, l == depth - 1
        gbias = jnp.broadcast_to(jnp.concatenate([ml_i_bias[l], ml_f_bias[l]], axis=1).astype(F32)[:, :, None],
                                 (2, 2 * ML_HEADS, LANES))
        a_neg = -jnp.exp(ssd_a_log[l].astype(F32))
        sbias = jnp.broadcast_to(jnp.concatenate([ssd_dt_bias[l].astype(F32), a_neg], axis=0)[:, :, None],
                                 (4, SSD_HEADS, LANES))
        q, kv, ml, z, xbc, gates, ml_rows, ml_cols, ssd_rows, ssd_cols = _in_projection(
            stream, first, mods, l, g_norm1[l].reshape(1, -1), cos_t, sin_t, w_pieces, gbias, sbias, dims)

        att = _attention(q, kv, _lane_row(att_sink[l]), dims, ctx_queries=not last)

        mlo = _mlstm(ml, ml_rows, ml_cols, ml_head_gain[l].reshape(1, -1), dims)

        conv_w = jnp.zeros((8, SSD_XBC), F32).at[:SSD_CONV].set(ssd_conv_w[l])
        sso = _ssd(z, xbc, ssd_rows, ssd_cols, conv_w, ssd_conv_b[l].reshape(1, -1),
                   jnp.repeat(ssd_d[l].astype(F32), SSD_HEAD_DIM).reshape(1, -1),
                   ssd_norm_gain[l].reshape(1, -1), dims)

        out = _merge_ffn(stream, first, att, mlo, sso, gates, mods, l, g_norm2[l].reshape(1, -1),
                         g_final.reshape(1, -1), wts, dims, last)
        stream = [out]
    return out.reshape(n_b, seq, d_model)
```

```python
import functools

import numpy as np
import jax
import jax.numpy as jnp
from jax import lax
from jax.experimental import pallas as pl
from jax.experimental.pallas import tpu as pltpu

F32 = jnp.float32
BF16 = jnp.bfloat16

D_MODEL = 1024
EPS = 1e-6
N_MOD = 6
GRID_W = 64
ROPE_BASE = 10000.0

ATT_HEADS = 8
ATT_KV_HEADS = 2
ATT_GROUP = ATT_HEADS // ATT_KV_HEADS
ATT_HEAD_DIM = 64
ATT_WINDOW = 128
ATT_Q = ATT_HEADS * ATT_HEAD_DIM
ATT_KV = ATT_KV_HEADS * ATT_HEAD_DIM
ROPE_FREQS = ATT_HEAD_DIM // 4

ML_HEADS = 4
ML_HEAD_DIM = 128
ML_INNER = ML_HEADS * ML_HEAD_DIM

SSD_HEADS = 8
SSD_HEAD_DIM = 64
SSD_GROUPS = 2
SSD_HPG = SSD_HEADS // SSD_GROUPS
SSD_STATE = 128
SSD_CONV = 5
SSD_INNER = SSD_HEADS * SSD_HEAD_DIM
SSD_XBC = SSD_INNER + 2 * SSD_GROUPS * SSD_STATE

D_FF = -((-8 * D_MODEL) // (3 * 256)) * 256
MXU_DIM = 256
FFN_CHUNKS = ((0, 5 * MXU_DIM), (5 * MXU_DIM, D_FF - 5 * MXU_DIM))

CHUNK = 128
LANES = 128
HALO = 8
PROJ_GROUPS = 4
FFN_GROUPS = 2
SCAN_BATCH = 2
ML_BATCH = 4
ATT_BATCH = 4
MOD_ROWS = 24
VMEM_LIMIT = 56 * 1024 * 1024
LOG2E = 1.4426950408889634

W_Q0 = 0
W_KV0 = ATT_Q
W_ML0 = W_KV0 + 2 * ATT_KV
W_MIX = W_ML0 + 4 * ML_INNER
IN_MLG0 = W_MIX
IN_Z0 = IN_MLG0 + 4 * ML_HEADS
IN_DT0 = IN_Z0 + SSD_INNER + SSD_XBC
IN_G0 = IN_DT0 + 2 * SSD_HEADS
DT_LANE0 = 4 * ML_HEADS


def _cparams(sem):
    return pltpu.CompilerParams(dimension_semantics=sem, vmem_limit_bytes=VMEM_LIMIT)


def _resident(stacked_shape, layer, depth):
    rows, cols = stacked_shape
    return pl.BlockSpec((rows // depth, cols), lambda *_: (layer, 0), pipeline_mode=pl.Buffered(1))


def _stack_rows(w):
    return w.reshape(w.shape[0] * w.shape[1], w.shape[2])


def _sigmoid(x):
    return 1.0 / (1.0 + jnp.exp(-x))


def _softplus(x):
    return jnp.maximum(x, 0.0) + jnp.log1p(jnp.exp(-jnp.abs(x)))


def _dot(a, b):
    return jnp.dot(a, b, preferred_element_type=F32)


def _dot_nt(a, b):
    return lax.dot_general(a, b, (((1,), (1,)), ((), ())), preferred_element_type=F32)


def _rmsnorm(x, gain):
    return x * lax.rsqrt(jnp.mean(x * x, axis=-1, keepdims=True) + EPS) * gain


def _mod_kernel(c_ref, w_ref, b_ref, o_ref):
    c = c_ref[...]
    a = (c * _sigmoid(c)).astype(BF16)
    o_ref[...] = _dot(a, w_ref[...].astype(BF16)) + b_ref[...]


def _modulation(cc, w_mod, b_mod):
    depth, d, n = w_mod.shape
    tn = 3072
    return pl.pallas_call(
        _mod_kernel,
        out_shape=jax.ShapeDtypeStruct((depth, MOD_ROWS, n), F32),
        grid=(depth, n // tn),
        in_specs=[
            pl.BlockSpec((MOD_ROWS, d), lambda l, j: (0, 0)),
            pl.BlockSpec((None, d, tn), lambda l, j: (l, 0, j)),
            pl.BlockSpec((None, 1, tn), lambda l, j: (l, 0, j)),
        ],
        out_specs=pl.BlockSpec((None, MOD_ROWS, tn), lambda l, j: (l, 0, j)),
        compiler_params=_cparams(("arbitrary", "arbitrary")),
        name="modulation",
    )(cc, w_mod, b_mod.reshape(depth, 1, n))


def _stream_specs(first, groups, dims):
    n_b, nct, nlt = dims["B"], dims["nct"], dims["nlt"]
    per_chunk = n_b // groups
    if not first:
        return [pl.BlockSpec((groups, CHUNK, D_MODEL), lambda i: (i, 0, 0))]

    def ctx_idx(i):
        p = jnp.minimum(i // per_chunk, nct - 1)
        return (jnp.where(i < nct * per_chunk, i % per_chunk, per_chunk - 1), p, 0, 0)

    def lat_idx(i):
        p = jnp.maximum(i // per_chunk - nct, 0)
        return (jnp.where(i < nct * per_chunk, 0, i % per_chunk), p, 0, 0)

    return [pl.BlockSpec((groups, None, CHUNK, D_MODEL), ctx_idx),
            pl.BlockSpec((groups, None, CHUNK, D_MODEL), lat_idx)]


def _mod_spec(layer, groups, dims, tile0=0):
    n_b, nct = dims["B"], dims["nct"]
    per_chunk = n_b // groups

    def idx(i):
        i = i + tile0
        return (layer, jnp.where(i < nct * per_chunk, per_chunk, i % per_chunk), 0, 0)

    return pl.BlockSpec((None, groups, N_MOD, D_MODEL), idx)


def _read_stream(refs, k, is_ctx_tile):
    if len(refs) == 1:
        return refs[0][k]
    return jnp.where(is_ctx_tile, refs[0][k], refs[1][k])


def _rope(x, cos, sin_signed, first_half):
    partner = jnp.where(first_half, pltpu.roll(x, LANES - ROPE_FREQS, 1), pltpu.roll(x, ROPE_FREQS, 1))
    return x * cos + partner * sin_signed


def _inproj_kernel(*refs, n_src, ctx_tiles):
    x_refs = refs[:n_src]
    (mod_ref, g_ref, cos_ref, sin_ref, gb_ref, sb_ref, w_mix_ref, w_ssd_ref, w_merge_ref, w_small_ref,
     oq_ref, okv_ref, oml_ref, oz_ref, oxbc_ref, og_ref,
     mr_ref, mc_ref, sr_ref, sc_ref, h_ref) = refs[n_src:]
    is_ctx_tile = pl.program_id(0) < ctx_tiles
    gain = g_ref[...]
    for k in range(PROJ_GROUPS):
        mod = mod_ref[k]
        y = _rmsnorm(_read_stream(x_refs, k, is_ctx_tile), gain)
        h_ref[k * CHUNK:(k + 1) * CHUNK, :] = (y * (1.0 + mod[1:2]) + mod[0:1]).astype(BF16)
    h = h_ref[...]

    cos = jnp.concatenate([cos_ref[...]] * PROJ_GROUPS, axis=0)
    sin = jnp.concatenate([sin_ref[...]] * PROJ_GROUPS, axis=0)
    lane = lax.broadcasted_iota(jnp.int32, (1, LANES), 1)
    first_half = (lane % (2 * ROPE_FREQS)) < ROPE_FREQS

    def proj(c0, width, w_ref=w_mix_ref):
        return _dot(h, w_ref[:, c0:c0 + width])

    _gate_prep_tile(proj(0, LANES, w_small_ref), gb_ref, sb_ref, mr_ref, mc_ref, sr_ref, sc_ref)

    q = proj(W_Q0, ATT_Q)
    for s in range(ATT_Q // LANES):
        qs = _rope(q[:, s * LANES:(s + 1) * LANES], cos, sin, first_half)
        oq_ref[:, s * LANES:(s + 1) * LANES] = (qs * ATT_HEAD_DIM ** -0.5).astype(oq_ref.dtype)
    kv = proj(W_KV0, 2 * ATT_KV)
    okv_ref[:, 0:ATT_KV] = _rope(kv[:, 0:ATT_KV], cos, sin, first_half).astype(okv_ref.dtype)
    okv_ref[:, ATT_KV:] = kv[:, ATT_KV:].astype(okv_ref.dtype)

    oml_ref[:, 0:ML_INNER] = (proj(W_ML0, ML_INNER) * ML_HEAD_DIM ** -0.5).astype(oml_ref.dtype)
    kproj = proj(W_ML0 + ML_INNER, ML_INNER)
    for r in range(PROJ_GROUPS):
        for hd in range(ML_HEADS):
            blk = kproj[r * CHUNK:(r + 1) * CHUNK, hd * ML_HEAD_DIM:(hd + 1) * ML_HEAD_DIM]
            oml_ref[r * CHUNK:(r + 1) * CHUNK,
                    ML_INNER + hd * ML_HEAD_DIM:ML_INNER + (hd + 1) * ML_HEAD_DIM] = blk.T.astype(oml_ref.dtype)
    for s in range(2, 4):
        oml_ref[:, s * ML_INNER:(s + 1) * ML_INNER] = proj(W_ML0 + s * ML_INNER, ML_INNER).astype(oml_ref.dtype)

    oz_ref[...] = proj(0, SSD_INNER, w_ssd_ref).astype(oz_ref.dtype)
    for s in range(SSD_XBC // 512):
        oxbc_ref[:, s * 512:(s + 1) * 512] = proj(SSD_INNER + s * 512, 512, w_ssd_ref).astype(oxbc_ref.dtype)
    for s in range(3 * D_MODEL // 512):
        og_ref[:, s * 512:(s + 1) * 512] = proj(s * 512, 512, w_merge_ref).astype(og_ref.dtype)


def _in_projection(stream, first, mods, layer, g1, cos_t, sin_t, w_pieces, gbias, sbias, dims):
    n_b, nct = dims["B"], dims["nct"]
    n_tok = dims["n_tok"]
    n_groups = n_tok // CHUNK
    tm = PROJ_GROUPS * CHUNK
    per_chunk = n_b // PROJ_GROUPS
    outs = [
        jax.ShapeDtypeStruct((n_tok, ATT_Q), BF16),
        jax.ShapeDtypeStruct((n_tok, 2 * ATT_KV), BF16),
        jax.ShapeDtypeStruct((n_tok, 4 * ML_INNER), BF16),
        jax.ShapeDtypeStruct((n_tok, SSD_INNER), F32),
        jax.ShapeDtypeStruct((n_tok, SSD_XBC), F32),
        jax.ShapeDtypeStruct((n_tok, 3 * D_MODEL), F32),
    ]
    gate_outs = [
        jax.ShapeDtypeStruct((2, n_groups, ML_ROWS, LANES), F32),
        jax.ShapeDtypeStruct((2, n_tok, LANES), F32),
        jax.ShapeDtypeStruct((2, n_groups, SSD_ROWS, LANES), F32),
        jax.ShapeDtypeStruct((2, n_tok, LANES), F32),
    ]
    gate_specs = [
        pl.BlockSpec((2, PROJ_GROUPS, ML_ROWS, LANES), lambda i: (0, i, 0, 0)),
        pl.BlockSpec((2, tm, LANES), lambda i: (0, i, 0)),
        pl.BlockSpec((2, PROJ_GROUPS, SSD_ROWS, LANES), lambda i: (0, i, 0, 0)),
        pl.BlockSpec((2, tm, LANES), lambda i: (0, i, 0)),
    ]
    return pl.pallas_call(
        functools.partial(_inproj_kernel, n_src=len(stream), ctx_tiles=nct * per_chunk),
        out_shape=outs + gate_outs,
        grid=(n_tok // tm,),
        in_specs=_stream_specs(first, PROJ_GROUPS, dims) + [
            _mod_spec(layer, PROJ_GROUPS, dims),
            pl.BlockSpec((1, D_MODEL), lambda i: (0, 0)),
            pl.BlockSpec((CHUNK, LANES), lambda i: (i // per_chunk, 0)),
            pl.BlockSpec((CHUNK, LANES), lambda i: (i // per_chunk, 0)),
            pl.BlockSpec((2, 8, LANES), lambda i: (0, 0, 0)),
            pl.BlockSpec((4, SSD_HEADS, LANES), lambda i: (0, 0, 0)),
        ] + [_resident(w.shape, layer, mods.shape[0]) for w in w_pieces],
        out_specs=[pl.BlockSpec((tm, o.shape[1]), lambda i: (i, 0)) for o in outs] + gate_specs,
        scratch_shapes=[pltpu.VMEM((tm, D_MODEL), BF16)],
        compiler_params=_cparams(("arbitrary",)),
        name="in_projection",
    )(*stream, mods, g1, cos_t, sin_t, gbias, sbias, *w_pieces)


def _attention_blocks(qs, kvs, valid, sink, o_refs):
    lane = lax.broadcasted_iota(jnp.int32, (1, LANES), 1)
    low = lane < ATT_HEAD_DIM
    ones = jnp.ones((kvs[0].shape[0], LANES), BF16)
    zero = jnp.zeros((CHUNK, LANES), BF16)
    units = [(i, g) for i in range(len(qs)) for g in range(ATT_KV_HEADS)]

    scores, values = [], []
    for i, g in units:
        k_g = kvs[i][:, g * ATT_HEAD_DIM:(g + 1) * ATT_HEAD_DIM]
        v_g = kvs[i][:, ATT_KV + g * ATT_HEAD_DIM:ATT_KV + (g + 1) * ATT_HEAD_DIM]
        kk = jnp.concatenate([k_g, k_g], axis=1)
        values.append(jnp.concatenate([v_g, v_g, ones], axis=1))
        q_rows = []
        for pair in range(ATT_GROUP // 2):
            c0 = (g * ATT_GROUP + 2 * pair) * ATT_HEAD_DIM
            qp = qs[i][:, c0:c0 + LANES]
            q_rows += [jnp.where(low, qp, zero), jnp.where(low, zero, qp)]
        scores.append(_dot_nt(jnp.concatenate(q_rows, axis=0), kk))

    probs, sink_terms = [], []
    for (i, g), s_all in zip(units, scores):
        p_rows, t_rows = [], []
        for r in range(ATT_GROUP):
            s = s_all[r * CHUNK:(r + 1) * CHUNK]
            if valid is not None:
                s = jnp.where(valid, s, -jnp.inf)
            sk = sink[:, g * ATT_GROUP + r:g * ATT_GROUP + r + 1]
            m = jnp.maximum(jnp.max(s, axis=-1, keepdims=True), sk)
            p_rows.append(jnp.exp((s - m).astype(BF16)))
            t_rows.append(jnp.exp(sk - m))
        probs.append(jnp.concatenate(p_rows, axis=0))
        sink_terms.append(t_rows)

    outs = [_dot(p, vw) for p, vw in zip(probs, values)]

    for (i, g), o_all, t_rows in zip(units, outs, sink_terms):
        heads = []
        for r in range(ATT_GROUP):
            o = o_all[r * CHUNK:(r + 1) * CHUNK]
            heads.append(o[:, :LANES] / (o[:, LANES:] + t_rows[r]))
        for pair in range(ATT_GROUP // 2):
            c0 = (g * ATT_GROUP + 2 * pair) * ATT_HEAD_DIM
            o_refs[i][:, c0:c0 + LANES] = jnp.where(low, heads[2 * pair], heads[2 * pair + 1]).astype(o_refs[i].dtype)


def _attention_kernel(*refs, nct, nlt, ctx_queries):
    q_ref = refs[0]
    loc_refs = refs[1:4]
    ctx_refs = refs[4:4 + nct]
    sink_ref, o_ref = refs[4 + nct:]
    step = pl.program_id(1)
    sink = sink_ref[...]

    def rows_of(i):
        return slice(i * CHUNK, (i + 1) * CHUNK)

    def latent(j):
        span = 3 * CHUNK
        first = jnp.clip(j - 1, 0, nlt - 3)
        n_keys = span + nct * CHUNK
        row = lax.broadcasted_iota(jnp.int32, (CHUNK, n_keys), 0)
        col = lax.broadcasted_iota(jnp.int32, (CHUNK, n_keys), 1)
        dist = (j - first) * CHUNK + row - col
        valid = (jnp.abs(dist) <= ATT_WINDOW) | (col >= span)
        kvs = [jnp.concatenate([r[rows_of(i), :] for r in loc_refs + ctx_refs], axis=0) for i in range(ATT_BATCH)]
        _attention_blocks([q_ref[rows_of(i), :] for i in range(ATT_BATCH)], kvs, valid, sink,
                          [o_ref.at[rows_of(i), :] for i in range(ATT_BATCH)])

    if not ctx_queries:
        latent(step)
        return

    @pl.when(step < nct)
    def _():
        kvs = [jnp.concatenate([r[rows_of(i), :] for r in ctx_refs], axis=0) for i in range(ATT_BATCH)]
        _attention_blocks([q_ref[rows_of(i), :] for i in range(ATT_BATCH)], kvs, None, sink,
                          [o_ref.at[rows_of(i), :] for i in range(ATT_BATCH)])

    @pl.when(step >= nct)
    def _():
        latent(step - nct)


def _attention(q, kv, sink_row, dims, ctx_queries):
    n_b, nct, nlt = dims["B"], dims["nct"], dims["nlt"]
    c0 = 0 if ctx_queries else nct
    n_steps = nct + nlt - c0

    per_chunk = n_b // ATT_BATCH
    rows = ATT_BATCH * CHUNK

    def win(i):
        def idx(b, s):
            first = jnp.clip(s + c0 - nct - 1, 0, nlt - 3)
            return ((nct + first + i) * per_chunk + b, 0)
        return idx

    return pl.pallas_call(
        functools.partial(_attention_kernel, nct=nct, nlt=nlt, ctx_queries=ctx_queries),
        out_shape=jax.ShapeDtypeStruct((n_steps * n_b * CHUNK, ATT_Q), BF16),
        grid=(per_chunk, n_steps),
        in_specs=[pl.BlockSpec((rows, ATT_Q), lambda b, s: ((s + c0) * per_chunk + b, 0))]
        + [pl.BlockSpec((rows, 2 * ATT_KV), win(i)) for i in range(3)]
        + [pl.BlockSpec((rows, 2 * ATT_KV), functools.partial(lambda b, s, i: (i * per_chunk + b, 0), i=i))
           for i in range(nct)]
        + [pl.BlockSpec((1, LANES), lambda b, s: (0, 0))],
        out_specs=pl.BlockSpec((rows, ATT_Q), lambda b, s: (s * per_chunk + b, 0)),
        compiler_params=_cparams(("arbitrary", "arbitrary")),
        name="attention",
    )(q, *([kv] * (3 + nct)), sink_row)


def _scan_chunk(d, c, nct, nlt):
    fwd = c
    bwd = jnp.where(c < nct, nct - 1 - c, nct + nlt - 1 - (c - nct))
    return jnp.where(d == 0, fwd, bwd)


def _causal_mask(d):
    ri = lax.broadcasted_iota(jnp.int32, (CHUNK, CHUNK), 0)
    ci = lax.broadcasted_iota(jnp.int32, (CHUNK, CHUNK), 1)
    return (ci - ri) * (1 - 2 * d) <= 0


def _scan_lanes(x8, backward, combine, identity):
    lane = lax.broadcasted_iota(jnp.int32, (1, LANES), 1)
    k = 1
    while k < LANES:
        if backward:
            shifted = jnp.where(lane < LANES - k, pltpu.roll(x8, LANES - k, 1), identity)
        else:
            shifted = jnp.where(lane >= k, pltpu.roll(x8, k, 1), identity)
        x8 = combine(x8, shifted)
        k *= 2
    return x8


def _cummax_lanes(x8, backward):
    return _scan_lanes(x8, backward, jnp.maximum, -jnp.inf)


def _rows_to_columns(x8):
    pad = jnp.zeros((LANES - 8, LANES), F32)
    return jnp.concatenate([x8, pad], axis=0).T


def _lane_fill(cols, j):
    return jnp.broadcast_to(cols[:, j:j + 1], cols.shape)


ML_ROWS = 16
SSD_ROWS = 32


def _gate_prep_tile(small, gb_ref, sb_ref, mr_ref, mc_ref, sr_ref, sc_ref):
    sub8 = lax.broadcasted_iota(jnp.int32, (8, 1), 0)
    head_rows = sub8 < ML_HEADS
    for r in range(small.shape[0] // CHUNK):
        rows = slice(r * CHUNK, (r + 1) * CHUNK)
        gt = small[rows, :].T
        for dd in range(2):
            g8 = gt[8 * dd:8 * dd + 8] + gb_ref[dd]
            g8 = jnp.where(head_rows, g8, jnp.minimum(g8, 0.0) - jnp.log1p(jnp.exp(-jnp.abs(g8)))) * LOG2E
            dt8 = _softplus(gt[DT_LANE0 + 8 * dd:DT_LANE0 + 8 * dd + 8] + sb_ref[dd])
            la8 = dt8 * sb_ref[2 + dd]
            gsum = _scan_lanes(g8, dd == 1, jnp.add, 0.0)
            acs8 = _scan_lanes(la8, dd == 1, jnp.add, 0.0)

            b8 = pltpu.roll(gsum, ML_HEADS, 0)
            c8 = g8 - b8
            cm8 = _cummax_lanes(c8, backward=dd == 1)
            cm_end = jnp.broadcast_to(jnp.max(cm8, axis=1, keepdims=True), cm8.shape)
            b_end = jnp.broadcast_to(jnp.sum(g8, axis=1, keepdims=True), g8.shape)
            mr_ref[dd, r, 0:8] = jnp.where(head_rows, c8, 0.0)
            mr_ref[dd, r, 8:16] = jnp.where(head_rows, cm_end, b_end)
            mc_ref[dd, rows, :] = _rows_to_columns(jnp.where(head_rows, cm8, gsum))

            a_end = jnp.sum(la8, axis=1, keepdims=True)
            sr_ref[dd, r, 0:8] = dt8
            sr_ref[dd, r, 8:16] = acs8 * LOG2E
            sr_ref[dd, r, 16:24] = dt8 * jnp.exp(a_end - acs8)
            sr_ref[dd, r, 24:32] = jnp.broadcast_to(jnp.exp(a_end), dt8.shape)
            sc_ref[dd, rows, :] = _rows_to_columns(acs8 * LOG2E)


def _mlstm_kernel(ml_ref, mr_ref, mc_ref, gain_ref, o_ref, ct_ref, m_ref, hf_ref, *, nct, nlt):
    d = pl.program_id(1)
    c = pl.program_id(2)

    @pl.when(c == 0)
    def _():
        ct_ref[...] = jnp.zeros_like(ct_ref)
        m_ref[...] = jnp.zeros_like(m_ref)

    p = _scan_chunk(d, c, nct, nlt)
    sub8 = lax.broadcasted_iota(jnp.int32, (8, 1), 0)
    causal = _causal_mask(d)
    ones = jnp.ones((CHUNK, ML_HEAD_DIM), BF16)
    gain = gain_ref[...]

    rows_of = [slice(i * CHUNK, (i + 1) * CHUNK) for i in range(ML_BATCH)]
    units = [(i, h) for i in range(ML_BATCH) for h in range(ML_HEADS)]
    c8, m8, w8, dec8, cols = [], [], [], [], []
    for i in range(ML_BATCH):
        c8.append(mr_ref[i, 0:8])
        m8.append(m_ref[i])
        ends = mr_ref[i, 8:16]
        mx_end = jnp.maximum(ends, m8[i])
        w8.append(jnp.exp2(c8[i] - mx_end))
        dec8.append(jnp.exp2(m8[i] - mx_end))
        m_ref[i] = jnp.where(sub8 < ML_HEADS, pltpu.roll(ends, ML_HEADS, 0) + mx_end, 0.0)
        cols.append(mc_ref[rows_of[i], :])

    def piece(i, h, which):
        c0 = which * ML_INNER + h * ML_HEAD_DIM
        return ml_ref[rows_of[i], c0:c0 + ML_HEAD_DIM]

    q = {u: piece(*u, 0) for u in units}
    k_t = {u: piece(*u, 1) for u in units}
    v1 = {u: jnp.concatenate([piece(*u, 2), ones], axis=1) for u in units}
    ctn = {(i, h): ct_ref[i * ML_HEADS + h] for i, h in units}

    qk = {u: _dot(q[u], k_t[u]) for u in units}
    qc = {u: _dot(q[u], ctn[u].astype(BF16)) for u in units}

    s, k_w, cm_t = {}, {}, {}
    for i, h in units:
        cm_t[i, h] = _lane_fill(cols[i], h)
        dmat = jnp.exp2(jnp.where(causal, c8[i][h:h + 1, :] - cm_t[i, h], -jnp.inf))
        s[i, h] = (qk[i, h] * dmat).astype(BF16)
        k_w[i, h] = (k_t[i, h].astype(F32) * w8[i][h:h + 1, :]).astype(BF16)

    sv = {u: _dot(s[u], v1[u]) for u in units}
    kv = {u: _dot(k_w[u], v1[u]) for u in units}

    h_dirs = []
    for i in range(ML_BATCH):
        hs = []
        for h in range(ML_HEADS):
            m_row = m8[i][h:h + 1, :]
            mx_t = jnp.maximum(cm_t[i, h], m_row)
            intra = jnp.exp2(cm_t[i, h] - mx_t)
            prev = jnp.exp2(m_row - mx_t)
            nd = (jnp.concatenate([intra, intra], axis=1) * sv[i, h]
                  + jnp.concatenate([prev, prev], axis=1) * qc[i, h])
            floor = jnp.exp2(-(_lane_fill(cols[i], ML_HEADS + h) + mx_t))
            hs.append(nd[:, :ML_HEAD_DIM] / jnp.maximum(jnp.abs(nd[:, ML_HEAD_DIM:]), floor))
            dec = dec8[i][h:h + 1, :]
            ct_ref[i * ML_HEADS + h] = jnp.concatenate([dec, dec], axis=1) * ctn[i, h] + kv[i, h]
        h_dirs.append(jnp.concatenate(hs, axis=1))

    @pl.when(d == 0)
    def _():
        hf_ref[p] = jnp.concatenate(h_dirs, axis=0)

    @pl.when(d == 1)
    def _():
        tot = hf_ref[p] + jnp.concatenate(h_dirs, axis=0)
        for h in range(ML_HEADS):
            sl = slice(h * ML_HEAD_DIM, (h + 1) * ML_HEAD_DIM)
            o_gate = ml_ref[:, 3 * ML_INNER + h * ML_HEAD_DIM:3 * ML_INNER + (h + 1) * ML_HEAD_DIM].astype(F32)
            o_ref[:, sl] = (_rmsnorm(tot[:, sl], gain[:, sl]) * _sigmoid(o_gate)).astype(o_ref.dtype)


def _scan_block_maps(dims, batch):
    nct, nlt = dims["nct"], dims["nlt"]
    per_chunk = dims["B"] // batch
    nc = nct + nlt

    def blk(bg, d, c):
        return _scan_chunk(d, c, nct, nlt) * per_chunk + bg

    def fwd_blk(bg, d, c):
        return jnp.where(d == 0, blk(bg, 0, c), blk(bg, 0, nc - 1))

    def bwd_blk(bg, d, c):
        return jnp.where(d == 0, blk(bg, 1, 0), blk(bg, 1, c))

    return blk, fwd_blk, bwd_blk


def _mlstm(ml, gate_rows, gate_cols, gain_row, dims):
    nct, nlt = dims["nct"], dims["nlt"]
    nc = nct + nlt
    rows = ML_BATCH * CHUNK
    blk, _, bwd_blk = _scan_block_maps(dims, ML_BATCH)
    return pl.pallas_call(
        functools.partial(_mlstm_kernel, nct=nct, nlt=nlt),
        out_shape=jax.ShapeDtypeStruct((dims["n_tok"], ML_INNER), BF16),
        grid=(dims["B"] // ML_BATCH, 2, nc),
        in_specs=[
            pl.BlockSpec((rows, 4 * ML_INNER), lambda bg, d, c: (blk(bg, d, c), 0)),
            pl.BlockSpec((None, ML_BATCH, ML_ROWS, LANES), lambda bg, d, c: (d, blk(bg, d, c), 0, 0)),
            pl.BlockSpec((None, rows, LANES), lambda bg, d, c: (d, blk(bg, d, c), 0)),
            pl.BlockSpec((1, ML_INNER), lambda bg, d, c: (0, 0)),
        ],
        out_specs=pl.BlockSpec((rows, ML_INNER), lambda bg, d, c: (bwd_blk(bg, d, c), 0)),
        scratch_shapes=[
            pltpu.VMEM((ML_BATCH * ML_HEADS, ML_HEAD_DIM, 2 * ML_HEAD_DIM), F32),
            pltpu.VMEM((ML_BATCH, 8, LANES), F32),
            pltpu.VMEM((nc, rows, ML_INNER), F32),
        ],
        compiler_params=_cparams(("arbitrary", "arbitrary", "arbitrary")),
        name="mlstm",
    )(ml, gate_rows, gate_cols, gain_row)


def _ssd_kernel(*refs, nct, nlt):
    xbc_ref = refs[0]
    prev_refs = refs[1:1 + SCAN_BATCH]
    next_refs = refs[1 + SCAN_BATCH:1 + 2 * SCAN_BATCH]
    (z_ref, sr_ref, sc_ref, cw_ref, cb_ref, dsk_ref, gain_ref,
     o_ref, st_ref, xa_ref, yf_ref, ext_ref) = refs[1 + 2 * SCAN_BATCH:]
    d = pl.program_id(1)
    c = pl.program_id(2)

    @pl.when(c == 0)
    def _():
        st_ref[...] = jnp.zeros_like(st_ref)

    p = _scan_chunk(d, c, nct, nlt)
    lane = lax.broadcasted_iota(jnp.int32, (1, LANES), 1)
    low = lane < SSD_HEAD_DIM
    causal = _causal_mask(d)

    @pl.when(d == 0)
    def _():
        has_prev = jnp.where((p != 0) & (p != nct), 1.0, 0.0)
        has_next = jnp.where((p != nct - 1) & (p != nct + nlt - 1), 1.0, 0.0)
        cw = cw_ref[...]
        cbias = cb_ref[...]
        for i in range(SCAN_BATCH):
            rows = slice(i * CHUNK, (i + 1) * CHUNK)
            for lt in range(SSD_XBC // LANES):
                ln = slice(lt * LANES, (lt + 1) * LANES)
                ext_ref[i, lt, 0:HALO, :] = prev_refs[i][:, ln].astype(F32) * has_prev
                ext_ref[i, lt, HALO:HALO + CHUNK, :] = xbc_ref[rows, ln].astype(F32)
                ext_ref[i, lt, HALO + CHUNK:, :] = next_refs[i][:, ln].astype(F32) * has_next
                acc = jnp.zeros((CHUNK, LANES), F32) + cbias[:, ln]
                for tap in range(SSD_CONV):
                    off = HALO - SSD_CONV // 2 + tap
                    acc = acc + ext_ref[i, lt, off:off + CHUNK, :] * cw[tap:tap + 1, ln]
                xa_ref[p, lt, rows, :] = acc * _sigmoid(acc)

    x_tile0, b_tile0, c_tile0 = 0, SSD_INNER // LANES, SSD_INNER // LANES + SSD_GROUPS
    rows_of = [slice(i * CHUNK, (i + 1) * CHUNK) for i in range(SCAN_BATCH)]
    groups = [(i, g) for i in range(SCAN_BATCH) for g in range(SSD_GROUPS)]
    heads = [(i, h) for i in range(SCAN_BATCH) for h in range(SSD_HEADS)]
    dt8 = [sr_ref[i, 0:8] for i in range(SCAN_BATCH)]
    acs8 = [sr_ref[i, 8:16] for i in range(SCAN_BATCH)]
    dw8 = [sr_ref[i, 16:24] for i in range(SCAN_BATCH)]
    dec8 = [sr_ref[i, 24:32] for i in range(SCAN_BATCH)]
    cols = [sc_ref[rows_of[i], :] for i in range(SCAN_BATCH)]

    cb, bm_t, st, y_state = {}, {}, {}, {}
    for i, g in groups:
        bm_f = xa_ref[p, b_tile0 + g, rows_of[i], :]
        cm = xa_ref[p, c_tile0 + g, rows_of[i], :].astype(BF16)
        cb[i, g] = _dot_nt(cm, bm_f.astype(BF16))
        bm_t[i, g] = bm_f.T
        st[i, g] = st_ref[i * SSD_GROUPS + g]
        y_state[i, g] = _dot(cm, st[i, g].astype(BF16))

    mmat, bw, xm, grow = {}, {}, {}, {}
    for i, h in heads:
        g = h // SSD_HPG
        a_t = _lane_fill(cols[i], h)
        e = jnp.exp2(jnp.where(causal, a_t - acs8[i][h:h + 1, :], -jnp.inf))
        mmat[i, h] = (cb[i, g] * e * dt8[i][h:h + 1, :]).astype(BF16)
        bw[i, h] = (bm_t[i, g] * dw8[i][h:h + 1, :]).astype(BF16)
        x_pair = xa_ref[p, x_tile0 + h // 2, rows_of[i], :].astype(BF16)
        xm[i, h] = jnp.where(low if h % 2 == 0 else ~low, x_pair, jnp.zeros_like(x_pair))
        grow[i, h] = jnp.exp2(a_t)

    y_h = {u: _dot(mmat[u], xm[u]) for u in heads}
    s_h = {u: _dot(bw[u], xm[u]) for u in heads}

    y_dirs = []
    for i in range(SCAN_BATCH):
        ys = []
        for g in range(SSD_GROUPS):
            st_parts = []
            for pair in range(SSD_HPG // 2):
                h0 = g * SSD_HPG + 2 * pair
                lanes = slice(pair * LANES, (pair + 1) * LANES)
                ys.append(y_h[i, h0] + y_h[i, h0 + 1]
                          + jnp.where(low, grow[i, h0], grow[i, h0 + 1]) * y_state[i, g][:, lanes])
                dec = jnp.where(low, dec8[i][h0:h0 + 1, :], dec8[i][h0 + 1:h0 + 2, :])
                st_parts.append(dec * st[i, g][:, lanes] + s_h[i, h0] + s_h[i, h0 + 1])
            st_ref[i * SSD_GROUPS + g] = jnp.concatenate(st_parts, axis=1)
        y_dirs.append(jnp.concatenate(ys, axis=1))

    @pl.when(d == 0)
    def _():
        yf_ref[p] = jnp.concatenate(y_dirs, axis=0)

    @pl.when(d == 1)
    def _():
        xs = jnp.concatenate([xa_ref[p, x_tile0 + lt] for lt in range(SSD_INNER // LANES)], axis=1)
        y = yf_ref[p] + jnp.concatenate(y_dirs, axis=0) + dsk_ref[...] * xs
        z = z_ref[...].astype(F32)
        o_ref[...] = _rmsnorm(y * (z * _sigmoid(z)), gain_ref[...]).astype(o_ref.dtype)


def _ssd(z, xbc, gate_rows, gate_cols, conv_w, conv_b, dskip_row, gain_row, dims):
    n_b, nct, nlt = dims["B"], dims["nct"], dims["nlt"]
    nc = nct + nlt
    rows = SCAN_BATCH * CHUNK
    sub = CHUNK // HALO
    n_halo_blocks = dims["n_tok"] // HALO
    blk, fwd_blk, bwd_blk = _scan_block_maps(dims, SCAN_BATCH)

    def halo(i, side):
        def idx(bg, d, c):
            group = fwd_blk(bg, d, c) * SCAN_BATCH + i
            if side < 0:
                return (jnp.maximum((group - n_b) * sub + sub - 1, 0), 0)
            return (jnp.minimum((group + n_b) * sub, n_halo_blocks - 1), 0)
        return idx

    const = lambda bg, d, c: (0, 0)
    return pl.pallas_call(
        functools.partial(_ssd_kernel, nct=nct, nlt=nlt),
        out_shape=jax.ShapeDtypeStruct((dims["n_tok"], SSD_INNER), BF16),
        grid=(n_b // SCAN_BATCH, 2, nc),
        in_specs=[pl.BlockSpec((rows, SSD_XBC), lambda bg, d, c: (fwd_blk(bg, d, c), 0))]
        + [pl.BlockSpec((HALO, SSD_XBC), halo(i, -1)) for i in range(SCAN_BATCH)]
        + [pl.BlockSpec((HALO, SSD_XBC), halo(i, +1)) for i in range(SCAN_BATCH)]
        + [
            pl.BlockSpec((rows, SSD_INNER), lambda bg, d, c: (bwd_blk(bg, d, c), 0)),
            pl.BlockSpec((None, SCAN_BATCH, SSD_ROWS, LANES), lambda bg, d, c: (d, blk(bg, d, c), 0, 0)),
            pl.BlockSpec((None, rows, LANES), lambda bg, d, c: (d, blk(bg, d, c), 0)),
            pl.BlockSpec((8, SSD_XBC), const),
            pl.BlockSpec((1, SSD_XBC), const),
            pl.BlockSpec((1, SSD_INNER), const),
            pl.BlockSpec((1, SSD_INNER), const),
        ],
        out_specs=pl.BlockSpec((rows, SSD_INNER), lambda bg, d, c: (bwd_blk(bg, d, c), 0)),
        scratch_shapes=[
            pltpu.VMEM((SCAN_BATCH * SSD_GROUPS, SSD_STATE, SSD_HPG * SSD_HEAD_DIM), F32),
            pltpu.VMEM((nc, SSD_XBC // LANES, rows, LANES), F32),
            pltpu.VMEM((nc, rows, SSD_INNER), F32),
            pltpu.VMEM((SCAN_BATCH, SSD_XBC // LANES, CHUNK + 2 * HALO, LANES), F32),
        ],
        compiler_params=_cparams(("arbitrary", "arbitrary", "arbitrary")),
        name="ssd",
    )(xbc, *([xbc] * (2 * SCAN_BATCH)), z, gate_rows, gate_cols, conv_w, conv_b, dskip_row, gain_row)


def _merge_ffn_kernel(*refs, n_src, ctx_tiles, final_norm):
    x_refs = refs[:n_src]
    (att_ref, ml_ref, ss_ref, gt_ref, mod_ref, g2_ref, gf_ref,
     wa_ref, wm_ref, ws_ref, wo_ref, wup_ref, wdn_ref, o_ref, x1_ref, h_ref) = refs[n_src:]
    is_ctx_tile = pl.program_id(0) < ctx_tiles
    y = None
    for i, (src, w) in enumerate(((att_ref, wa_ref), (ml_ref, wm_ref), (ss_ref, ws_ref))):
        gate = _sigmoid(gt_ref[:, i * D_MODEL:(i + 1) * D_MODEL].astype(F32))
        term = gate * _dot(src[...], w[...])
        y = term if y is None else y + term
    yo = _dot(y.astype(BF16), wo_ref[...])
    g2 = g2_ref[...]
    for k in range(FFN_GROUPS):
        rows = slice(k * CHUNK, (k + 1) * CHUNK)
        mod = mod_ref[k]
        x1 = _read_stream(x_refs, k, is_ctx_tile) + mod[2:3] * yo[rows]
        x1_ref[rows, :] = x1
        h_ref[rows, :] = (_rmsnorm(x1, g2) * (1.0 + mod[4:5]) + mod[3:4]).astype(BF16)
    h = h_ref[...]
    acc = None
    for c0, width in FFN_CHUNKS:
        gate = _dot(h, wup_ref[:, c0:c0 + width])
        up = _dot(h, wup_ref[:, D_FF + c0:D_FF + c0 + width])
        act = (gate * _sigmoid(gate) * up).astype(BF16)
        part = _dot(act, wdn_ref[c0:c0 + width, :])
        acc = part if acc is None else acc + part
    for k in range(FFN_GROUPS):
        rows = slice(k * CHUNK, (k + 1) * CHUNK)
        x2 = x1_ref[rows, :] + mod_ref[k][5:6] * acc[rows]
        if final_norm:
            x2 = _rmsnorm(x2, gf_ref[...])
        o_ref[k] = x2


def _merge_ffn(stream, first, att, ml, ss, gates, mods, layer, g2, gf, wts, dims, last):
    n_b, nct, nlt = dims["B"], dims["nct"], dims["nlt"]
    tm = FFN_GROUPS * CHUNK
    per_chunk = n_b // FFN_GROUPS
    ctx_tiles = nct * per_chunk
    t0 = ctx_tiles if last else 0
    n_tiles = (nct + nlt) * per_chunk - t0
    assert not (first and last)

    row = lambda i: (i + t0, 0)
    att_row = (lambda i: (i, 0)) if last else row
    if last:
        out_shape = jax.ShapeDtypeStruct((n_b, nlt, CHUNK, D_MODEL), F32)
        out_spec = pl.BlockSpec((FFN_GROUPS, None, CHUNK, D_MODEL), lambda i: (i % per_chunk, i // per_chunk, 0, 0))
        stream_specs = [pl.BlockSpec((FFN_GROUPS, CHUNK, D_MODEL), lambda i: (i + t0, 0, 0))]
    else:
        out_shape = jax.ShapeDtypeStruct(((nct + nlt) * n_b, CHUNK, D_MODEL), F32)
        out_spec = pl.BlockSpec((FFN_GROUPS, CHUNK, D_MODEL), lambda i: (i, 0, 0))
        stream_specs = _stream_specs(first, FFN_GROUPS, dims)
    return pl.pallas_call(
        functools.partial(_merge_ffn_kernel, n_src=len(stream), ctx_tiles=ctx_tiles - t0, final_norm=last),
        out_shape=out_shape,
        grid=(n_tiles,),
        in_specs=stream_specs + [
            pl.BlockSpec((tm, ATT_Q), att_row),
            pl.BlockSpec((tm, ML_INNER), row),
            pl.BlockSpec((tm, SSD_INNER), row),
            pl.BlockSpec((tm, 3 * D_MODEL), row),
            _mod_spec(layer, FFN_GROUPS, dims, tile0=t0),
            pl.BlockSpec((1, D_MODEL), lambda i: (0, 0)),
            pl.BlockSpec((1, D_MODEL), lambda i: (0, 0)),
        ] + [_resident(w.shape, layer, mods.shape[0]) for w in wts],
        out_specs=out_spec,
        scratch_shapes=[pltpu.VMEM((tm, D_MODEL), F32), pltpu.VMEM((tm, D_MODEL), BF16)],
        compiler_params=_cparams(("arbitrary",)),
        name="merge_ffn",
    )(*stream, att, ml, ss, gates, mods, g2, gf, *wts)


def _rope_tables(seq, nct):
    pos = np.arange(seq)
    row = (pos // GRID_W).astype(np.float32)
    col = (pos % GRID_W).astype(np.float32)
    inv = (np.float32(ROPE_BASE) ** (-np.arange(ROPE_FREQS, dtype=np.float32) / np.float32(ROPE_FREQS))).astype(np.float32)
    ang_r = (row[:, None] * inv).astype(np.float32)
    ang_c = (col[:, None] * inv).astype(np.float32)
    cos_h = np.concatenate([np.cos(ang_r), np.cos(ang_r), np.cos(ang_c), np.cos(ang_c)], axis=1)
    sin_h = np.concatenate([-np.sin(ang_r), np.sin(ang_r), -np.sin(ang_c), np.sin(ang_c)], axis=1)
    cos_t = np.concatenate([np.ones((nct * CHUNK, LANES)), np.tile(cos_h, (1, 2))], axis=0)
    sin_t = np.concatenate([np.zeros((nct * CHUNK, LANES)), np.tile(sin_h, (1, 2))], axis=0)
    return jnp.asarray(cos_t, F32), jnp.asarray(sin_t, F32)


def _split_w_in(w):
    pad = jnp.zeros(w.shape[:2] + (LANES - 4 * ML_HEADS - 2 * SSD_HEADS,), w.dtype)
    small = jnp.concatenate([w[..., IN_MLG0:IN_Z0], w[..., IN_DT0:IN_G0], pad], axis=-1)
    pieces = [w[..., :W_MIX], w[..., IN_Z0:IN_DT0], w[..., IN_G0:], small]
    return [_stack_rows(piece).astype(BF16) for piece in pieces]


def _lane_row(vals):
    return jnp.zeros((1, LANES), F32).at[0, :vals.shape[0]].set(vals.astype(F32))


def kernel(x, c, ctx, c_ctx, w_mod, b_mod, g_norm1, w_in, att_sink, ml_i_bias, ml_f_bias, ml_head_gain,
           ssd_conv_w, ssd_conv_b, ssd_dt_bias, ssd_a_log, ssd_d, ssd_norm_gain,
           w_att_out, w_ml_out, w_ssd_out, w_o, g_norm2, w_up, w_down, g_final):
    n_b, seq, d_model = x.shape
    lc = ctx.shape[1]
    depth = w_mod.shape[0]
    nct, nlt = lc // CHUNK, seq // CHUNK
    dims = dict(B=n_b, nct=nct, nlt=nlt, n_tok=(nct + nlt) * n_b * CHUNK)
    assert d_model == D_MODEL and seq % CHUNK == 0 and lc % CHUNK == 0 and nlt >= 3 and depth >= 2
    assert n_b % PROJ_GROUPS == 0 and n_b % FFN_GROUPS == 0 and n_b % SCAN_BATCH == 0 and n_b % ATT_BATCH == 0
    assert n_b % ML_BATCH == 0
    assert n_b + PROJ_GROUPS <= MOD_ROWS

    cc = jnp.zeros((MOD_ROWS, d_model), F32).at[:n_b].set(c).at[n_b:n_b + PROJ_GROUPS].set(c_ctx)
    mods = _modulation(cc, w_mod, b_mod).reshape(depth, MOD_ROWS, N_MOD, d_model)
    cos_t, sin_t = _rope_tables(seq, nct)
    w_pieces = _split_w_in(w_in)
    wts = tuple(_stack_rows(w).astype(BF16) for w in (w_att_out, w_ml_out, w_ssd_out, w_o, w_up, w_down))

    stream = [ctx.reshape(n_b, nct, CHUNK, d_model), x.reshape(n_b, nlt, CHUNK, d_model)]
    for l in range(depth):
        first, last = l == 0, l == depth - 1
        gbias = jnp.broadcast_to(jnp.concatenate([ml_i_bias[l], ml_f_bias[l]], axis=1).astype(F32)[:, :, None],
                                 (2, 2 * ML_HEADS, LANES))
        a_neg = -jnp.exp(ssd_a_log[l].astype(F32))
        sbias = jnp.broadcast_to(jnp.concatenate([ssd_dt_bias[l].astype(F32), a_neg], axis=0)[:, :, None],
                                 (4, SSD_HEADS, LANES))
        q, kv, ml, z, xbc, gates, ml_rows, ml_cols, ssd_rows, ssd_cols = _in_projection(
            stream, first, mods, l, g_norm1[l].reshape(1, -1), cos_t, sin_t, w_pieces, gbias, sbias, dims)

        att = _attention(q, kv, _lane_row(att_sink[l]), dims, ctx_queries=not last)

        mlo = _mlstm(ml, ml_rows, ml_cols, ml_head_gain[l].reshape(1, -1), dims)

        conv_w = jnp.zeros((8, SSD_XBC), F32).at[:SSD_CONV].set(ssd_conv_w[l])
        sso = _ssd(z, xbc, ssd_rows, ssd_cols, conv_w, ssd_conv_b[l].reshape(1, -1),
                   jnp.repeat(ssd_d[l].astype(F32), SSD_HEAD_DIM).reshape(1, -1),
                   ssd_norm_gain[l].reshape(1, -1), dims)

        out = _merge_ffn(stream, first, att, mlo, sso, gates, mods, l, g_norm2[l].reshape(1, -1),
                         g_final.reshape(1, -1), wts, dims, last)
        stream = [out]
    return out.reshape(n_b, seq, d_model)
```

```python
import functools

import numpy as np
import jax
import jax.numpy as jnp
from jax import lax
from jax.experimental import pallas as pl
from jax.experimental.pallas import tpu as pltpu

F32 = jnp.float32
BF16 = jnp.bfloat16

D_MODEL = 1024
EPS = 1e-6
N_MOD = 6
GRID_W = 64
ROPE_BASE = 10000.0

ATT_HEADS = 8
ATT_KV_HEADS = 2
ATT_GROUP = ATT_HEADS // ATT_KV_HEADS
ATT_HEAD_DIM = 64
ATT_WINDOW = 128
ATT_Q = ATT_HEADS * ATT_HEAD_DIM
ATT_KV = ATT_KV_HEADS * ATT_HEAD_DIM
ROPE_FREQS = ATT_HEAD_DIM // 4

ML_HEADS = 4
ML_HEAD_DIM = 128
ML_INNER = ML_HEADS * ML_HEAD_DIM

SSD_HEADS = 8
SSD_HEAD_DIM = 64
SSD_GROUPS = 2
SSD_HPG = SSD_HEADS // SSD_GROUPS
SSD_STATE = 128
SSD_CONV = 5
SSD_INNER = SSD_HEADS * SSD_HEAD_DIM
SSD_XBC = SSD_INNER + 2 * SSD_GROUPS * SSD_STATE

D_FF = -((-8 * D_MODEL) // (3 * 256)) * 256
MXU_DIM = 256
FFN_CHUNKS = ((0, 5 * MXU_DIM), (5 * MXU_DIM, D_FF - 5 * MXU_DIM))

CHUNK = 128
LANES = 128
HALO = 8
PROJ_GROUPS = 4
FFN_GROUPS = 2
SCAN_BATCH = 2
ML_BATCH = 4
ATT_BATCH = 8
MOD_ROWS = 24
VMEM_LIMIT = 56 * 1024 * 1024
LOG2E = 1.4426950408889634

W_Q0 = 0
W_KV0 = ATT_Q
W_ML0 = W_KV0 + 2 * ATT_KV
W_MIX = W_ML0 + 4 * ML_INNER
IN_MLG0 = W_MIX
IN_Z0 = IN_MLG0 + 4 * ML_HEADS
IN_DT0 = IN_Z0 + SSD_INNER + SSD_XBC
IN_G0 = IN_DT0 + 2 * SSD_HEADS
DT_LANE0 = 4 * ML_HEADS


def _cparams(sem):
    return pltpu.CompilerParams(dimension_semantics=sem, vmem_limit_bytes=VMEM_LIMIT)


def _resident(stacked_shape, layer, depth):
    rows, cols = stacked_shape
    return pl.BlockSpec((rows // depth, cols), lambda *_: (layer, 0), pipeline_mode=pl.Buffered(1))


def _stack_rows(w):
    return w.reshape(w.shape[0] * w.shape[1], w.shape[2])


def _sigmoid(x):
    return 1.0 / (1.0 + jnp.exp(-x))


def _softplus(x):
    return jnp.maximum(x, 0.0) + jnp.log1p(jnp.exp(-jnp.abs(x)))


def _dot(a, b):
    return jnp.dot(a, b, preferred_element_type=F32)


def _dot_nt(a, b):
    return lax.dot_general(a, b, (((1,), (1,)), ((), ())), preferred_element_type=F32)


def _rmsnorm(x, gain):
    return x * lax.rsqrt(jnp.mean(x * x, axis=-1, keepdims=True) + EPS) * gain


def _mod_kernel(c_ref, w_ref, b_ref, o_ref):
    c = c_ref[...]
    a = (c * _sigmoid(c)).astype(BF16)
    o_ref[...] = _dot(a, w_ref[...].astype(BF16)) + b_ref[...]


def _modulation(cc, w_mod, b_mod):
    depth, d, n = w_mod.shape
    tn = 3072
    return pl.pallas_call(
        _mod_kernel,
        out_shape=jax.ShapeDtypeStruct((depth, MOD_ROWS, n), F32),
        grid=(depth, n // tn),
        in_specs=[
            pl.BlockSpec((MOD_ROWS, d), lambda l, j: (0, 0)),
            pl.BlockSpec((None, d, tn), lambda l, j: (l, 0, j)),
            pl.BlockSpec((None, 1, tn), lambda l, j: (l, 0, j)),
        ],
        out_specs=pl.BlockSpec((None, MOD_ROWS, tn), lambda l, j: (l, 0, j)),
        compiler_params=_cparams(("arbitrary", "arbitrary")),
        name="modulation",
    )(cc, w_mod, b_mod.reshape(depth, 1, n))


def _stream_specs(first, groups, dims):
    n_b, nct, nlt = dims["B"], dims["nct"], dims["nlt"]
    per_chunk = n_b // groups
    if not first:
        return [pl.BlockSpec((groups, CHUNK, D_MODEL), lambda i: (i, 0, 0))]

    def ctx_idx(i):
        p = jnp.minimum(i // per_chunk, nct - 1)
        return (jnp.where(i < nct * per_chunk, i % per_chunk, per_chunk - 1), p, 0, 0)

    def lat_idx(i):
        p = jnp.maximum(i // per_chunk - nct, 0)
        return (jnp.where(i < nct * per_chunk, 0, i % per_chunk), p, 0, 0)

    return [pl.BlockSpec((groups, None, CHUNK, D_MODEL), ctx_idx),
            pl.BlockSpec((groups, None, CHUNK, D_MODEL), lat_idx)]


def _mod_spec(layer, groups, dims, tile0=0):
    n_b, nct = dims["B"], dims["nct"]
    per_chunk = n_b // groups

    def idx(i):
        i = i + tile0
        return (layer, jnp.where(i < nct * per_chunk, per_chunk, i % per_chunk), 0, 0)

    return pl.BlockSpec((None, groups, N_MOD, D_MODEL), idx)


def _read_stream(refs, k, is_ctx_tile):
    if len(refs) == 1:
        return refs[0][k]
    return jnp.where(is_ctx_tile, refs[0][k], refs[1][k])


def _rope(x, cos, sin_signed, first_half):
    partner = jnp.where(first_half, pltpu.roll(x, LANES - ROPE_FREQS, 1), pltpu.roll(x, ROPE_FREQS, 1))
    return x * cos + partner * sin_signed


def _inproj_kernel(*refs, n_src, ctx_tiles):
    x_refs = refs[:n_src]
    (mod_ref, g_ref, cos_ref, sin_ref, gb_ref, sb_ref, w_mix_ref, w_ssd_ref, w_merge_ref, w_small_ref,
     oq_ref, okv_ref, oml_ref, oz_ref, oxbc_ref, og_ref,
     mr_ref, mc_ref, sr_ref, sc_ref, h_ref) = refs[n_src:]
    is_ctx_tile = pl.program_id(0) < ctx_tiles
    gain = g_ref[...]
    for k in range(PROJ_GROUPS):
        mod = mod_ref[k]
        y = _rmsnorm(_read_stream(x_refs, k, is_ctx_tile), gain)
        h_ref[k * CHUNK:(k + 1) * CHUNK, :] = (y * (1.0 + mod[1:2]) + mod[0:1]).astype(BF16)
    h = h_ref[...]

    cos = jnp.concatenate([cos_ref[...]] * PROJ_GROUPS, axis=0)
    sin = jnp.concatenate([sin_ref[...]] * PROJ_GROUPS, axis=0)
    lane = lax.broadcasted_iota(jnp.int32, (1, LANES), 1)
    first_half = (lane % (2 * ROPE_FREQS)) < ROPE_FREQS

    def proj(c0, width, w_ref=w_mix_ref):
        return _dot(h, w_ref[:, c0:c0 + width])

    _gate_prep_tile(proj(0, LANES, w_small_ref), gb_ref, sb_ref, mr_ref, mc_ref, sr_ref, sc_ref)

    q = proj(W_Q0, ATT_Q)
    for s in range(ATT_Q // LANES):
        qs = _rope(q[:, s * LANES:(s + 1) * LANES], cos, sin, first_half)
        oq_ref[:, s * LANES:(s + 1) * LANES] = (qs * ATT_HEAD_DIM ** -0.5).astype(oq_ref.dtype)
    kv = proj(W_KV0, 2 * ATT_KV)
    okv_ref[:, 0:ATT_KV] = _rope(kv[:, 0:ATT_KV], cos, sin, first_half).astype(okv_ref.dtype)
    okv_ref[:, ATT_KV:] = kv[:, ATT_KV:].astype(okv_ref.dtype)

    oml_ref[:, 0:ML_INNER] = (proj(W_ML0, ML_INNER) * ML_HEAD_DIM ** -0.5).astype(oml_ref.dtype)
    kproj = proj(W_ML0 + ML_INNER, ML_INNER)
    for r in range(PROJ_GROUPS):
        for hd in range(ML_HEADS):
            blk = kproj[r * CHUNK:(r + 1) * CHUNK, hd * ML_HEAD_DIM:(hd + 1) * ML_HEAD_DIM]
            oml_ref[r * CHUNK:(r + 1) * CHUNK,
                    ML_INNER + hd * ML_HEAD_DIM:ML_INNER + (hd + 1) * ML_HEAD_DIM] = blk.T.astype(oml_ref.dtype)
    for s in range(2, 4):
        oml_ref[:, s * ML_INNER:(s + 1) * ML_INNER] = proj(W_ML0 + s * ML_INNER, ML_INNER).astype(oml_ref.dtype)

    oz_ref[...] = proj(0, SSD_INNER, w_ssd_ref).astype(oz_ref.dtype)
    for s in range(SSD_XBC // 512):
        oxbc_ref[:, s * 512:(s + 1) * 512] = proj(SSD_INNER + s * 512, 512, w_ssd_ref).astype(oxbc_ref.dtype)
    for s in range(3 * D_MODEL // 512):
        og_ref[:, s * 512:(s + 1) * 512] = proj(s * 512, 512, w_merge_ref).astype(og_ref.dtype)


def _in_projection(stream, first, mods, layer, g1, cos_t, sin_t, w_pieces, gbias, sbias, dims):
    n_b, nct = dims["B"], dims["nct"]
    n_tok = dims["n_tok"]
    n_groups = n_tok // CHUNK
    tm = PROJ_GROUPS * CHUNK
    per_chunk = n_b // PROJ_GROUPS
    outs = [
        jax.ShapeDtypeStruct((n_tok, ATT_Q), BF16),
        jax.ShapeDtypeStruct((n_tok, 2 * ATT_KV), BF16),
        jax.ShapeDtypeStruct((n_tok, 4 * ML_INNER), BF16),
        jax.ShapeDtypeStruct((n_tok, SSD_INNER), F32),
        jax.ShapeDtypeStruct((n_tok, SSD_XBC), F32),
        jax.ShapeDtypeStruct((n_tok, 3 * D_MODEL), F32),
    ]
    gate_outs = [
        jax.ShapeDtypeStruct((2, n_groups, ML_ROWS, LANES), F32),
        jax.ShapeDtypeStruct((2, n_tok, LANES), F32),
        jax.ShapeDtypeStruct((2, n_groups, SSD_ROWS, LANES), F32),
        jax.ShapeDtypeStruct((2, n_tok, LANES), F32),
    ]
    gate_specs = [
        pl.BlockSpec((2, PROJ_GROUPS, ML_ROWS, LANES), lambda i: (0, i, 0, 0)),
        pl.BlockSpec((2, tm, LANES), lambda i: (0, i, 0)),
        pl.BlockSpec((2, PROJ_GROUPS, SSD_ROWS, LANES), lambda i: (0, i, 0, 0)),
        pl.BlockSpec((2, tm, LANES), lambda i: (0, i, 0)),
    ]
    return pl.pallas_call(
        functools.partial(_inproj_kernel, n_src=len(stream), ctx_tiles=nct * per_chunk),
        out_shape=outs + gate_outs,
        grid=(n_tok // tm,),
        in_specs=_stream_specs(first, PROJ_GROUPS, dims) + [
            _mod_spec(layer, PROJ_GROUPS, dims),
            pl.BlockSpec((1, D_MODEL), lambda i: (0, 0)),
            pl.BlockSpec((CHUNK, LANES), lambda i: (i // per_chunk, 0)),
            pl.BlockSpec((CHUNK, LANES), lambda i: (i // per_chunk, 0)),
            pl.BlockSpec((2, 8, LANES), lambda i: (0, 0, 0)),
            pl.BlockSpec((4, SSD_HEADS, LANES), lambda i: (0, 0, 0)),
        ] + [_resident(w.shape, layer, mods.shape[0]) for w in w_pieces],
        out_specs=[pl.BlockSpec((tm, o.shape[1]), lambda i: (i, 0)) for o in outs] + gate_specs,
        scratch_shapes=[pltpu.VMEM((tm, D_MODEL), BF16)],
        compiler_params=_cparams(("arbitrary",)),
        name="in_projection",
    )(*stream, mods, g1, cos_t, sin_t, gbias, sbias, *w_pieces)


def _attention_blocks(qs, kvs, valid, sink, o_refs):
    lane = lax.broadcasted_iota(jnp.int32, (1, LANES), 1)
    low = lane < ATT_HEAD_DIM
    ones = jnp.ones((kvs[0].shape[0], LANES), BF16)
    zero = jnp.zeros((CHUNK, LANES), BF16)
    units = [(i, g) for i in range(len(qs)) for g in range(ATT_KV_HEADS)]

    scores, values = [], []
    for i, g in units:
        k_g = kvs[i][:, g * ATT_HEAD_DIM:(g + 1) * ATT_HEAD_DIM]
        v_g = kvs[i][:, ATT_KV + g * ATT_HEAD_DIM:ATT_KV + (g + 1) * ATT_HEAD_DIM]
        kk = jnp.concatenate([k_g, k_g], axis=1)
        values.append(jnp.concatenate([v_g, v_g, ones], axis=1))
        q_rows = []
        for pair in range(ATT_GROUP // 2):
            c0 = (g * ATT_GROUP + 2 * pair) * ATT_HEAD_DIM
            qp = qs[i][:, c0:c0 + LANES]
            q_rows += [jnp.where(low, qp, zero), jnp.where(low, zero, qp)]
        scores.append(_dot_nt(jnp.concatenate(q_rows, axis=0), kk))

    probs, sink_terms = [], []
    for (i, g), s_all in zip(units, scores):
        p_rows, t_rows = [], []
        for r in range(ATT_GROUP):
            s = s_all[r * CHUNK:(r + 1) * CHUNK]
            if valid is not None:
                s = jnp.where(valid, s, -jnp.inf)
            sk = sink[:, g * ATT_GROUP + r:g * ATT_GROUP + r + 1]
            m = jnp.maximum(jnp.max(s, axis=-1, keepdims=True), sk)
            p_rows.append(jnp.exp((s - m).astype(BF16)))
            t_rows.append(jnp.exp(sk - m))
        probs.append(jnp.concatenate(p_rows, axis=0))
        sink_terms.append(t_rows)

    outs = [_dot(p, vw) for p, vw in zip(probs, values)]

    for (i, g), o_all, t_rows in zip(units, outs, sink_terms):
        heads = []
        for r in range(ATT_GROUP):
            o = o_all[r * CHUNK:(r + 1) * CHUNK]
            heads.append(o[:, :LANES] / (o[:, LANES:] + t_rows[r]))
        for pair in range(ATT_GROUP // 2):
            c0 = (g * ATT_GROUP + 2 * pair) * ATT_HEAD_DIM
            o_refs[i][:, c0:c0 + LANES] = jnp.where(low, heads[2 * pair], heads[2 * pair + 1]).astype(o_refs[i].dtype)


def _attention_kernel(*refs, nct, nlt, ctx_queries):
    q_ref = refs[0]
    loc_refs = refs[1:4]
    ctx_refs = refs[4:4 + nct]
    sink_ref, o_ref = refs[4 + nct:]
    step = pl.program_id(1)
    sink = sink_ref[...]

    def rows_of(i):
        return slice(i * CHUNK, (i + 1) * CHUNK)

    def latent(j):
        span = 3 * CHUNK
        first = jnp.clip(j - 1, 0, nlt - 3)
        n_keys = span + nct * CHUNK
        row = lax.broadcasted_iota(jnp.int32, (CHUNK, n_keys), 0)
        col = lax.broadcasted_iota(jnp.int32, (CHUNK, n_keys), 1)
        dist = (j - first) * CHUNK + row - col
        valid = (jnp.abs(dist) <= ATT_WINDOW) | (col >= span)
        kvs = [jnp.concatenate([r[rows_of(i), :] for r in loc_refs + ctx_refs], axis=0) for i in range(ATT_BATCH)]
        _attention_blocks([q_ref[rows_of(i), :] for i in range(ATT_BATCH)], kvs, valid, sink,
                          [o_ref.at[rows_of(i), :] for i in range(ATT_BATCH)])

    if not ctx_queries:
        latent(step)
        return

    @pl.when(step < nct)
    def _():
        kvs = [jnp.concatenate([r[rows_of(i), :] for r in ctx_refs], axis=0) for i in range(ATT_BATCH)]
        _attention_blocks([q_ref[rows_of(i), :] for i in range(ATT_BATCH)], kvs, None, sink,
                          [o_ref.at[rows_of(i), :] for i in range(ATT_BATCH)])

    @pl.when(step >= nct)
    def _():
        latent(step - nct)


def _attention(q, kv, sink_row, dims, ctx_queries):
    n_b, nct, nlt = dims["B"], dims["nct"], dims["nlt"]
    c0 = 0 if ctx_queries else nct
    n_steps = nct + nlt - c0

    per_chunk = n_b // ATT_BATCH
    rows = ATT_BATCH * CHUNK

    def win(i):
        def idx(b, s):
            first = jnp.clip(s + c0 - nct - 1, 0, nlt - 3)
            return ((nct + first + i) * per_chunk + b, 0)
        return idx

    return pl.pallas_call(
        functools.partial(_attention_kernel, nct=nct, nlt=nlt, ctx_queries=ctx_queries),
        out_shape=jax.ShapeDtypeStruct((n_steps * n_b * CHUNK, ATT_Q), BF16),
        grid=(per_chunk, n_steps),
        in_specs=[pl.BlockSpec((rows, ATT_Q), lambda b, s: ((s + c0) * per_chunk + b, 0))]
        + [pl.BlockSpec((rows, 2 * ATT_KV), win(i)) for i in range(3)]
        + [pl.BlockSpec((rows, 2 * ATT_KV), functools.partial(lambda b, s, i: (i * per_chunk + b, 0), i=i))
           for i in range(nct)]
        + [pl.BlockSpec((1, LANES), lambda b, s: (0, 0))],
        out_specs=pl.BlockSpec((rows, ATT_Q), lambda b, s: (s * per_chunk + b, 0)),
        compiler_params=_cparams(("arbitrary", "arbitrary")),
        name="attention",
    )(q, *([kv] * (3 + nct)), sink_row)


def _scan_chunk(d, c, nct, nlt):
    fwd = c
    bwd = jnp.where(c < nct, nct - 1 - c, nct + nlt - 1 - (c - nct))
    return jnp.where(d == 0, fwd, bwd)


def _causal_mask(d):
    ri = lax.broadcasted_iota(jnp.int32, (CHUNK, CHUNK), 0)
    ci = lax.broadcasted_iota(jnp.int32, (CHUNK, CHUNK), 1)
    return (ci - ri) * (1 - 2 * d) <= 0


def _scan_lanes(x8, backward, combine, identity):
    lane = lax.broadcasted_iota(jnp.int32, (1, LANES), 1)
    k = 1
    while k < LANES:
        if backward:
            shifted = jnp.where(lane < LANES - k, pltpu.roll(x8, LANES - k, 1), identity)
        else:
            shifted = jnp.where(lane >= k, pltpu.roll(x8, k, 1), identity)
        x8 = combine(x8, shifted)
        k *= 2
    return x8


def _cummax_lanes(x8, backward):
    return _scan_lanes(x8, backward, jnp.maximum, -jnp.inf)


def _rows_to_columns(x8):
    pad = jnp.zeros((LANES - 8, LANES), F32)
    return jnp.concatenate([x8, pad], axis=0).T


def _lane_fill(cols, j):
    return jnp.broadcast_to(cols[:, j:j + 1], cols.shape)


ML_ROWS = 16
SSD_ROWS = 32


def _gate_prep_tile(small, gb_ref, sb_ref, mr_ref, mc_ref, sr_ref, sc_ref):
    sub8 = lax.broadcasted_iota(jnp.int32, (8, 1), 0)
    head_rows = sub8 < ML_HEADS
    for r in range(small.shape[0] // CHUNK):
        rows = slice(r * CHUNK, (r + 1) * CHUNK)
        gt = small[rows, :].T
        for dd in range(2):
            g8 = gt[8 * dd:8 * dd + 8] + gb_ref[dd]
            g8 = jnp.where(head_rows, g8, jnp.minimum(g8, 0.0) - jnp.log1p(jnp.exp(-jnp.abs(g8)))) * LOG2E
            dt8 = _softplus(gt[DT_LANE0 + 8 * dd:DT_LANE0 + 8 * dd + 8] + sb_ref[dd])
            la8 = dt8 * sb_ref[2 + dd]
            gsum = _scan_lanes(g8, dd == 1, jnp.add, 0.0)
            acs8 = _scan_lanes(la8, dd == 1, jnp.add, 0.0)

            b8 = pltpu.roll(gsum, ML_HEADS, 0)
            c8 = g8 - b8
            cm8 = _cummax_lanes(c8, backward=dd == 1)
            cm_end = jnp.broadcast_to(jnp.max(cm8, axis=1, keepdims=True), cm8.shape)
            b_end = jnp.broadcast_to(jnp.sum(g8, axis=1, keepdims=True), g8.shape)
            mr_ref[dd, r, 0:8] = jnp.where(head_rows, c8, 0.0)
            mr_ref[dd, r, 8:16] = jnp.where(head_rows, cm_end, b_end)
            mc_ref[dd, rows, :] = _rows_to_columns(jnp.where(head_rows, cm8, gsum))

            a_end = jnp.sum(la8, axis=1, keepdims=True)
            sr_ref[dd, r, 0:8] = dt8
            sr_ref[dd, r, 8:16] = acs8 * LOG2E
            sr_ref[dd, r, 16:24] = dt8 * jnp.exp(a_end - acs8)
            sr_ref[dd, r, 24:32] = jnp.broadcast_to(jnp.exp(a_end), dt8.shape)
            sc_ref[dd, rows, :] = _rows_to_columns(acs8 * LOG2E)


def _mlstm_kernel(ml_ref, mr_ref, mc_ref, gain_ref, o_ref, ct_ref, m_ref, hf_ref, *, nct, nlt):
    d = pl.program_id(1)
    c = pl.program_id(2)

    @pl.when(c == 0)
    def _():
        ct_ref[...] = jnp.zeros_like(ct_ref)
        m_ref[...] = jnp.zeros_like(m_ref)

    p = _scan_chunk(d, c, nct, nlt)
    sub8 = lax.broadcasted_iota(jnp.int32, (8, 1), 0)
    causal = _causal_mask(d)
    ones = jnp.ones((CHUNK, ML_HEAD_DIM), BF16)
    gain = gain_ref[...]

    rows_of = [slice(i * CHUNK, (i + 1) * CHUNK) for i in range(ML_BATCH)]
    units = [(i, h) for i in range(ML_BATCH) for h in range(ML_HEADS)]
    c8, m8, w8, dec8, cols = [], [], [], [], []
    for i in range(ML_BATCH):
        c8.append(mr_ref[i, 0:8])
        m8.append(m_ref[i])
        ends = mr_ref[i, 8:16]
        mx_end = jnp.maximum(ends, m8[i])
        w8.append(jnp.exp2(c8[i] - mx_end))
        dec8.append(jnp.exp2(m8[i] - mx_end))
        m_ref[i] = jnp.where(sub8 < ML_HEADS, pltpu.roll(ends, ML_HEADS, 0) + mx_end, 0.0)
        cols.append(mc_ref[rows_of[i], :])

    def piece(i, h, which):
        c0 = which * ML_INNER + h * ML_HEAD_DIM
        return ml_ref[rows_of[i], c0:c0 + ML_HEAD_DIM]

    q = {u: piece(*u, 0) for u in units}
    k_t = {u: piece(*u, 1) for u in units}
    v1 = {u: jnp.concatenate([piece(*u, 2), ones], axis=1) for u in units}
    ctn = {(i, h): ct_ref[i * ML_HEADS + h] for i, h in units}

    qk = {u: _dot(q[u], k_t[u]) for u in units}
    qc = {u: _dot(q[u], ctn[u].astype(BF16)) for u in units}

    s, k_w, cm_t = {}, {}, {}
    for i, h in units:
        cm_t[i, h] = _lane_fill(cols[i], h)
        dmat = jnp.exp2(jnp.where(causal, c8[i][h:h + 1, :] - cm_t[i, h], -jnp.inf))
        s[i, h] = (qk[i, h] * dmat).astype(BF16)
        k_w[i, h] = (k_t[i, h].astype(F32) * w8[i][h:h + 1, :]).astype(BF16)

    sv = {u: _dot(s[u], v1[u]) for u in units}
    kv = {u: _dot(k_w[u], v1[u]) for u in units}

    h_dirs = []
    for i in range(ML_BATCH):
        hs = []
        for h in range(ML_HEADS):
            m_row = m8[i][h:h + 1, :]
            mx_t = jnp.maximum(cm_t[i, h], m_row)
            intra = jnp.exp2(cm_t[i, h] - mx_t)
            prev = jnp.exp2(m_row - mx_t)
            nd = (jnp.concatenate([intra, intra], axis=1) * sv[i, h]
                  + jnp.concatenate([prev, prev], axis=1) * qc[i, h])
            floor = jnp.exp2(-(_lane_fill(cols[i], ML_HEADS + h) + mx_t))
            hs.append(nd[:, :ML_HEAD_DIM] / jnp.maximum(jnp.abs(nd[:, ML_HEAD_DIM:]), floor))
            dec = dec8[i][h:h + 1, :]
            ct_ref[i * ML_HEADS + h] = jnp.concatenate([dec, dec], axis=1) * ctn[i, h] + kv[i, h]
        h_dirs.append(jnp.concatenate(hs, axis=1))

    @pl.when(d == 0)
    def _():
        hf_ref[p] = jnp.concatenate(h_dirs, axis=0)

    @pl.when(d == 1)
    def _():
        tot = hf_ref[p] + jnp.concatenate(h_dirs, axis=0)
        for h in range(ML_HEADS):
            sl = slice(h * ML_HEAD_DIM, (h + 1) * ML_HEAD_DIM)
            o_gate = ml_ref[:, 3 * ML_INNER + h * ML_HEAD_DIM:3 * ML_INNER + (h + 1) * ML_HEAD_DIM].astype(F32)
            o_ref[:, sl] = (_rmsnorm(tot[:, sl], gain[:, sl]) * _sigmoid(o_gate)).astype(o_ref.dtype)


def _scan_block_maps(dims, batch):
    nct, nlt = dims["nct"], dims["nlt"]
    per_chunk = dims["B"] // batch
    nc = nct + nlt

    def blk(bg, d, c):
        return _scan_chunk(d, c, nct, nlt) * per_chunk + bg

    def fwd_blk(bg, d, c):
        return jnp.where(d == 0, blk(bg, 0, c), blk(bg, 0, nc - 1))

    def bwd_blk(bg, d, c):
        return jnp.where(d == 0, blk(bg, 1, 0), blk(bg, 1, c))

    return blk, fwd_blk, bwd_blk


def _mlstm(ml, gate_rows, gate_cols, gain_row, dims):
    nct, nlt = dims["nct"], dims["nlt"]
    nc = nct + nlt
    rows = ML_BATCH * CHUNK
    blk, _, bwd_blk = _scan_block_maps(dims, ML_BATCH)
    return pl.pallas_call(
        functools.partial(_mlstm_kernel, nct=nct, nlt=nlt),
        out_shape=jax.ShapeDtypeStruct((dims["n_tok"], ML_INNER), BF16),
        grid=(dims["B"] // ML_BATCH, 2, nc),
        in_specs=[
            pl.BlockSpec((rows, 4 * ML_INNER), lambda bg, d, c: (blk(bg, d, c), 0)),
            pl.BlockSpec((None, ML_BATCH, ML_ROWS, LANES), lambda bg, d, c: (d, blk(bg, d, c), 0, 0)),
            pl.BlockSpec((None, rows, LANES), lambda bg, d, c: (d, blk(bg, d, c), 0)),
            pl.BlockSpec((1, ML_INNER), lambda bg, d, c: (0, 0)),
        ],
        out_specs=pl.BlockSpec((rows, ML_INNER), lambda bg, d, c: (bwd_blk(bg, d, c), 0)),
        scratch_shapes=[
            pltpu.VMEM((ML_BATCH * ML_HEADS, ML_HEAD_DIM, 2 * ML_HEAD_DIM), F32),
            pltpu.VMEM((ML_BATCH, 8, LANES), F32),
            pltpu.VMEM((nc, rows, ML_INNER), F32),
        ],
        compiler_params=_cparams(("arbitrary", "arbitrary", "arbitrary")),
        name="mlstm",
    )(ml, gate_rows, gate_cols, gain_row)


def _ssd_kernel(*refs, nct, nlt):
    xbc_ref = refs[0]
    prev_refs = refs[1:1 + SCAN_BATCH]
    next_refs = refs[1 + SCAN_BATCH:1 + 2 * SCAN_BATCH]
    (z_ref, sr_ref, sc_ref, cw_ref, cb_ref, dsk_ref, gain_ref,
     o_ref, st_ref, xa_ref, yf_ref, ext_ref) = refs[1 + 2 * SCAN_BATCH:]
    d = pl.program_id(1)
    c = pl.program_id(2)

    @pl.when(c == 0)
    def _():
        st_ref[...] = jnp.zeros_like(st_ref)

    p = _scan_chunk(d, c, nct, nlt)
    lane = lax.broadcasted_iota(jnp.int32, (1, LANES), 1)
    low = lane < SSD_HEAD_DIM
    causal = _causal_mask(d)

    @pl.when(d == 0)
    def _():
        has_prev = jnp.where((p != 0) & (p != nct), 1.0, 0.0)
        has_next = jnp.where((p != nct - 1) & (p != nct + nlt - 1), 1.0, 0.0)
        cw = cw_ref[...]
        cbias = cb_ref[...]
        for i in range(SCAN_BATCH):
            rows = slice(i * CHUNK, (i + 1) * CHUNK)
            for lt in range(SSD_XBC // LANES):
                ln = slice(lt * LANES, (lt + 1) * LANES)
                ext_ref[i, lt, 0:HALO, :] = prev_refs[i][:, ln].astype(F32) * has_prev
                ext_ref[i, lt, HALO:HALO + CHUNK, :] = xbc_ref[rows, ln].astype(F32)
                ext_ref[i, lt, HALO + CHUNK:, :] = next_refs[i][:, ln].astype(F32) * has_next
                acc = jnp.zeros((CHUNK, LANES), F32) + cbias[:, ln]
                for tap in range(SSD_CONV):
                    off = HALO - SSD_CONV // 2 + tap
                    acc = acc + ext_ref[i, lt, off:off + CHUNK, :] * cw[tap:tap + 1, ln]
                xa_ref[p, lt, rows, :] = acc * _sigmoid(acc)

    x_tile0, b_tile0, c_tile0 = 0, SSD_INNER // LANES, SSD_INNER // LANES + SSD_GROUPS
    rows_of = [slice(i * CHUNK, (i + 1) * CHUNK) for i in range(SCAN_BATCH)]
    groups = [(i, g) for i in range(SCAN_BATCH) for g in range(SSD_GROUPS)]
    heads = [(i, h) for i in range(SCAN_BATCH) for h in range(SSD_HEADS)]
    dt8 = [sr_ref[i, 0:8] for i in range(SCAN_BATCH)]
    acs8 = [sr_ref[i, 8:16] for i in range(SCAN_BATCH)]
    dw8 = [sr_ref[i, 16:24] for i in range(SCAN_BATCH)]
    dec8 = [sr_ref[i, 24:32] for i in range(SCAN_BATCH)]
    cols = [sc_ref[rows_of[i], :] for i in range(SCAN_BATCH)]

    cb, bm_t, st, y_state = {}, {}, {}, {}
    for i, g in groups:
        bm_f = xa_ref[p, b_tile0 + g, rows_of[i], :]
        cm = xa_ref[p, c_tile0 + g, rows_of[i], :].astype(BF16)
        cb[i, g] = _dot_nt(cm, bm_f.astype(BF16))
        bm_t[i, g] = bm_f.T
        st[i, g] = st_ref[i * SSD_GROUPS + g]
        y_state[i, g] = _dot(cm, st[i, g].astype(BF16))

    mmat, bw, xm, grow = {}, {}, {}, {}
    for i, h in heads:
        g = h // SSD_HPG
        a_t = _lane_fill(cols[i], h)
        e = jnp.exp2(jnp.where(causal, a_t - acs8[i][h:h + 1, :], -jnp.inf))
        mmat[i, h] = (cb[i, g] * e * dt8[i][h:h + 1, :]).astype(BF16)
        bw[i, h] = (bm_t[i, g] * dw8[i][h:h + 1, :]).astype(BF16)
        x_pair = xa_ref[p, x_tile0 + h // 2, rows_of[i], :].astype(BF16)
        xm[i, h] = jnp.where(low if h % 2 == 0 else ~low, x_pair, jnp.zeros_like(x_pair))
        grow[i, h] = jnp.exp2(a_t)

    y_h = {u: _dot(mmat[u], xm[u]) for u in heads}
    s_h = {u: _dot(bw[u], xm[u]) for u in heads}

    y_dirs = []
    for i in range(SCAN_BATCH):
        ys = []
        for g in range(SSD_GROUPS):
            st_parts = []
            for pair in range(SSD_HPG // 2):
                h0 = g * SSD_HPG + 2 * pair
                lanes = slice(pair * LANES, (pair + 1) * LANES)
                ys.append(y_h[i, h0] + y_h[i, h0 + 1]
                          + jnp.where(low, grow[i, h0], grow[i, h0 + 1]) * y_state[i, g][:, lanes])
                dec = jnp.where(low, dec8[i][h0:h0 + 1, :], dec8[i][h0 + 1:h0 + 2, :])
                st_parts.append(dec * st[i, g][:, lanes] + s_h[i, h0] + s_h[i, h0 + 1])
            st_ref[i * SSD_GROUPS + g] = jnp.concatenate(st_parts, axis=1)
        y_dirs.append(jnp.concatenate(ys, axis=1))

    @pl.when(d == 0)
    def _():
        yf_ref[p] = jnp.concatenate(y_dirs, axis=0)

    @pl.when(d == 1)
    def _():
        xs = jnp.concatenate([xa_ref[p, x_tile0 + lt] for lt in range(SSD_INNER // LANES)], axis=1)
        y = yf_ref[p] + jnp.concatenate(y_dirs, axis=0) + dsk_ref[...] * xs
        z = z_ref[...].astype(F32)
        o_ref[...] = _rmsnorm(y * (z * _sigmoid(z)), gain_ref[...]).astype(o_ref.dtype)


def _ssd(z, xbc, gate_rows, gate_cols, conv_w, conv_b, dskip_row, gain_row, dims):
    n_b, nct, nlt = dims["B"], dims["nct"], dims["nlt"]
    nc = nct + nlt
    rows = SCAN_BATCH * CHUNK
    sub = CHUNK // HALO
    n_halo_blocks = dims["n_tok"] // HALO
    blk, fwd_blk, bwd_blk = _scan_block_maps(dims, SCAN_BATCH)

    def halo(i, side):
        def idx(bg, d, c):
            group = fwd_blk(bg, d, c) * SCAN_BATCH + i
            if side < 0:
                return (jnp.maximum((group - n_b) * sub + sub - 1, 0), 0)
            return (jnp.minimum((group + n_b) * sub, n_halo_blocks - 1), 0)
        return idx

    const = lambda bg, d, c: (0, 0)
    return pl.pallas_call(
        functools.partial(_ssd_kernel, nct=nct, nlt=nlt),
        out_shape=jax.ShapeDtypeStruct((dims["n_tok"], SSD_INNER), BF16),
        grid=(n_b // SCAN_BATCH, 2, nc),
        in_specs=[pl.BlockSpec((rows, SSD_XBC), lambda bg, d, c: (fwd_blk(bg, d, c), 0))]
        + [pl.BlockSpec((HALO, SSD_XBC), halo(i, -1)) for i in range(SCAN_BATCH)]
        + [pl.BlockSpec((HALO, SSD_XBC), halo(i, +1)) for i in range(SCAN_BATCH)]
        + [
            pl.BlockSpec((rows, SSD_INNER), lambda bg, d, c: (bwd_blk(bg, d, c), 0)),
            pl.BlockSpec((None, SCAN_BATCH, SSD_ROWS, LANES), lambda bg, d, c: (d, blk(bg, d, c), 0, 0)),
            pl.BlockSpec((None, rows, LANES), lambda bg, d, c: (d, blk(bg, d, c), 0)),
            pl.BlockSpec((8, SSD_XBC), const),
            pl.BlockSpec((1, SSD_XBC), const),
            pl.BlockSpec((1, SSD_INNER), const),
            pl.BlockSpec((1, SSD_INNER), const),
        ],
        out_specs=pl.BlockSpec((rows, SSD_INNER), lambda bg, d, c: (bwd_blk(bg, d, c), 0)),
        scratch_shapes=[
            pltpu.VMEM((SCAN_BATCH * SSD_GROUPS, SSD_STATE, SSD_HPG * SSD_HEAD_DIM), F32),
            pltpu.VMEM((nc, SSD_XBC // LANES, rows, LANES), F32),
            pltpu.VMEM((nc, rows, SSD_INNER), F32),
            pltpu.VMEM((SCAN_BATCH, SSD_XBC // LANES, CHUNK + 2 * HALO, LANES), F32),
        ],
        compiler_params=_cparams(("arbitrary", "arbitrary", "arbitrary")),
        name="ssd",
    )(xbc, *([xbc] * (2 * SCAN_BATCH)), z, gate_rows, gate_cols, conv_w, conv_b, dskip_row, gain_row)


def _merge_ffn_kernel(*refs, n_src, ctx_tiles, final_norm):
    x_refs = refs[:n_src]
    (att_ref, ml_ref, ss_ref, gt_ref, mod_ref, g2_ref, gf_ref,
     wa_ref, wm_ref, ws_ref, wo_ref, wup_ref, wdn_ref, o_ref, x1_ref, h_ref) = refs[n_src:]
    is_ctx_tile = pl.program_id(0) < ctx_tiles
    y = None
    for i, (src, w) in enumerate(((att_ref, wa_ref), (ml_ref, wm_ref), (ss_ref, ws_ref))):
        gate = _sigmoid(gt_ref[:, i * D_MODEL:(i + 1) * D_MODEL].astype(F32))
        term = gate * _dot(src[...], w[...])
        y = term if y is None else y + term
    yo = _dot(y.astype(BF16), wo_ref[...])
    g2 = g2_ref[...]
    for k in range(FFN_GROUPS):
        rows = slice(k * CHUNK, (k + 1) * CHUNK)
        mod = mod_ref[k]
        x1 = _read_stream(x_refs, k, is_ctx_tile) + mod[2:3] * yo[rows]
        x1_ref[rows, :] = x1
        h_ref[rows, :] = (_rmsnorm(x1, g2) * (1.0 + mod[4:5]) + mod[3:4]).astype(BF16)
    h = h_ref[...]
    acc = None
    for c0, width in FFN_CHUNKS:
        gate = _dot(h, wup_ref[:, c0:c0 + width])
        up = _dot(h, wup_ref[:, D_FF + c0:D_FF + c0 + width])
        act = (gate * _sigmoid(gate) * up).astype(BF16)
        part = _dot(act, wdn_ref[c0:c0 + width, :])
        acc = part if acc is None else acc + part
    for k in range(FFN_GROUPS):
        rows = slice(k * CHUNK, (k + 1) * CHUNK)
        x2 = x1_ref[rows, :] + mod_ref[k][5:6] * acc[rows]
        if final_norm:
            x2 = _rmsnorm(x2, gf_ref[...])
        o_ref[k] = x2


def _merge_ffn(stream, first, att, ml, ss, gates, mods, layer, g2, gf, wts, dims, last):
    n_b, nct, nlt = dims["B"], dims["nct"], dims["nlt"]
    tm = FFN_GROUPS * CHUNK
    per_chunk = n_b // FFN_GROUPS
    ctx_tiles = nct * per_chunk
    t0 = ctx_tiles if last else 0
    n_tiles = (nct + nlt) * per_chunk - t0
    assert not (first and last)

    row = lambda i: (i + t0, 0)
    att_row = (lambda i: (i, 0)) if last else row
    if last:
        out_shape = jax.ShapeDtypeStruct((n_b, nlt, CHUNK, D_MODEL), F32)
        out_spec = pl.BlockSpec((FFN_GROUPS, None, CHUNK, D_MODEL), lambda i: (i % per_chunk, i // per_chunk, 0, 0))
        stream_specs = [pl.BlockSpec((FFN_GROUPS, CHUNK, D_MODEL), lambda i: (i + t0, 0, 0))]
    else:
        out_shape = jax.ShapeDtypeStruct(((nct + nlt) * n_b, CHUNK, D_MODEL), F32)
        out_spec = pl.BlockSpec((FFN_GROUPS, CHUNK, D_MODEL), lambda i: (i, 0, 0))
        stream_specs = _stream_specs(first, FFN_GROUPS, dims)
    return pl.pallas_call(
        functools.partial(_merge_ffn_kernel, n_src=len(stream), ctx_tiles=ctx_tiles - t0, final_norm=last),
        out_shape=out_shape,
        grid=(n_tiles,),
        in_specs=stream_specs + [
            pl.BlockSpec((tm, ATT_Q), att_row),
            pl.BlockSpec((tm, ML_INNER), row),
            pl.BlockSpec((tm, SSD_INNER), row),
            pl.BlockSpec((tm, 3 * D_MODEL), row),
            _mod_spec(layer, FFN_GROUPS, dims, tile0=t0),
            pl.BlockSpec((1, D_MODEL), lambda i: (0, 0)),
            pl.BlockSpec((1, D_MODEL), lambda i: (0, 0)),
        ] + [_resident(w.shape, layer, mods.shape[0]) for w in wts],
        out_specs=out_spec,
        scratch_shapes=[pltpu.VMEM((tm, D_MODEL), F32), pltpu.VMEM((tm, D_MODEL), BF16)],
        compiler_params=_cparams(("arbitrary",)),
        name="merge_ffn",
    )(*stream, att, ml, ss, gates, mods, g2, gf, *wts)


def _rope_tables(seq, nct):
    pos = np.arange(seq)
    row = (pos // GRID_W).astype(np.float32)
    col = (pos % GRID_W).astype(np.float32)
    inv = (np.float32(ROPE_BASE) ** (-np.arange(ROPE_FREQS, dtype=np.float32) / np.float32(ROPE_FREQS))).astype(np.float32)
    ang_r = (row[:, None] * inv).astype(np.float32)
    ang_c = (col[:, None] * inv).astype(np.float32)
    cos_h = np.concatenate([np.cos(ang_r), np.cos(ang_r), np.cos(ang_c), np.cos(ang_c)], axis=1)
    sin_h = np.concatenate([-np.sin(ang_r), np.sin(ang_r), -np.sin(ang_c), np.sin(ang_c)], axis=1)
    cos_t = np.concatenate([np.ones((nct * CHUNK, LANES)), np.tile(cos_h, (1, 2))], axis=0)
    sin_t = np.concatenate([np.zeros((nct * CHUNK, LANES)), np.tile(sin_h, (1, 2))], axis=0)
    return jnp.asarray(cos_t, F32), jnp.asarray(sin_t, F32)


def _split_w_in(w):
    pad = jnp.zeros(w.shape[:2] + (LANES - 4 * ML_HEADS - 2 * SSD_HEADS,), w.dtype)
    small = jnp.concatenate([w[..., IN_MLG0:IN_Z0], w[..., IN_DT0:IN_G0], pad], axis=-1)
    pieces = [w[..., :W_MIX], w[..., IN_Z0:IN_DT0], w[..., IN_G0:], small]
    return [_stack_rows(piece).astype(BF16) for piece in pieces]


def _lane_row(vals):
    return jnp.zeros((1, LANES), F32).at[0, :vals.shape[0]].set(vals.astype(F32))


def kernel(x, c, ctx, c_ctx, w_mod, b_mod, g_norm1, w_in, att_sink, ml_i_bias, ml_f_bias, ml_head_gain,
           ssd_conv_w, ssd_conv_b, ssd_dt_bias, ssd_a_log, ssd_d, ssd_norm_gain,
           w_att_out, w_ml_out, w_ssd_out, w_o, g_norm2, w_up, w_down, g_final):
    n_b, seq, d_model = x.shape
    lc = ctx.shape[1]
    depth = w_mod.shape[0]
    nct, nlt = lc // CHUNK, seq // CHUNK
    dims = dict(B=n_b, nct=nct, nlt=nlt, n_tok=(nct + nlt) * n_b * CHUNK)
    assert d_model == D_MODEL and seq % CHUNK == 0 and lc % CHUNK == 0 and nlt >= 3 and depth >= 2
    assert n_b % PROJ_GROUPS == 0 and n_b % FFN_GROUPS == 0 and n_b % SCAN_BATCH == 0 and n_b % ATT_BATCH == 0
    assert n_b % ML_BATCH == 0
    assert n_b + PROJ_GROUPS <= MOD_ROWS

    cc = jnp.zeros((MOD_ROWS, d_model), F32).at[:n_b].set(c).at[n_b:n_b + PROJ_GROUPS].set(c_ctx)
    mods = _modulation(cc, w_mod, b_mod).reshape(depth, MOD_ROWS, N_MOD, d_model)
    cos_t, sin_t = _rope_tables(seq, nct)
    w_pieces = _split_w_in(w_in)
    wts = tuple(_stack_rows(w).astype(BF16) for w in (w_att_out, w_ml_out, w_ssd_out, w_o, w_up, w_down))

    stream = [ctx.reshape(n_b, nct, CHUNK, d_model), x.reshape(n_b, nlt, CHUNK, d_model)]
    for l in range(depth):
        first, last = l == 0, l == depth - 1
        gbias = jnp.broadcast_to(jnp.concatenate([ml_i_bias[l], ml_f_bias[l]], axis=1).astype(F32)[:, :, None],
                                 (2, 2 * ML_HEADS, LANES))
        a_neg = -jnp.exp(ssd_a_log[l].astype(F32))
        sbias = jnp.broadcast_to(jnp.concatenate([ssd_dt_bias[l].astype(F32), a_neg], axis=0)[:, :, None],
                                 (4, SSD_HEADS, LANES))
        q, kv, ml, z, xbc, gates, ml_rows, ml_cols, ssd_rows, ssd_cols = _in_projection(
            stream, first, mods, l, g_norm1[l].reshape(1, -1), cos_t, sin_t, w_pieces, gbias, sbias, dims)

        att = _attention(q, kv, _lane_row(att_sink[l]), dims, ctx_queries=not last)

        mlo = _mlstm(ml, ml_rows, ml_cols, ml_head_gain[l].reshape(1, -1), dims)

        conv_w = jnp.zeros((8, SSD_XBC), F32).at[:SSD_CONV].set(ssd_conv_w[l])
        sso = _ssd(z, xbc, ssd_rows, ssd_cols, conv_w, ssd_conv_b[l].reshape(1, -1),
                   jnp.repeat(ssd_d[l].astype(F32), SSD_HEAD_DIM).reshape(1, -1),
                   ssd_norm_gain[l].reshape(1, -1), dims)

        out = _merge_ffn(stream, first, att, mlo, sso, gates, mods, l, g_norm2[l].reshape(1, -1),
                         g_final.reshape(1, -1), wts, dims, last)
        stream = [out]
    return out.reshape(n_b, seq, d_model)
```

```python
import functools

import numpy as np
import jax
import jax.numpy as jnp
from jax import lax
from jax.experimental import pallas as pl
from jax.experimental.pallas import tpu as pltpu

F32 = jnp.float32
BF16 = jnp.bfloat16

D_MODEL = 1024
EPS = 1e-6
N_MOD = 6
GRID_W = 64
ROPE_BASE = 10000.0

ATT_HEADS = 8
ATT_KV_HEADS = 2
ATT_GROUP = ATT_HEADS // ATT_KV_HEADS
ATT_HEAD_DIM = 64
ATT_WINDOW = 128
ATT_Q = ATT_HEADS * ATT_HEAD_DIM
ATT_KV = ATT_KV_HEADS * ATT_HEAD_DIM
ROPE_FREQS = ATT_HEAD_DIM // 4

ML_HEADS = 4
ML_HEAD_DIM = 128
ML_INNER = ML_HEADS * ML_HEAD_DIM

SSD_HEADS = 8
SSD_HEAD_DIM = 64
SSD_GROUPS = 2
SSD_HPG = SSD_HEADS // SSD_GROUPS
SSD_STATE = 128
SSD_CONV = 5
SSD_INNER = SSD_HEADS * SSD_HEAD_DIM
SSD_XBC = SSD_INNER + 2 * SSD_GROUPS * SSD_STATE

D_FF = -((-8 * D_MODEL) // (3 * 256)) * 256
MXU_DIM = 256
FFN_CHUNKS = ((0, 5 * MXU_DIM), (5 * MXU_DIM, D_FF - 5 * MXU_DIM))

CHUNK = 128
LANES = 128
HALO = 8
PROJ_GROUPS = 4
FFN_GROUPS = 4
SCAN_BATCH = 2
ML_BATCH = 4
ATT_BATCH = 8
MOD_ROWS = 24
VMEM_LIMIT = 56 * 1024 * 1024
LOG2E = 1.4426950408889634

W_Q0 = 0
W_KV0 = ATT_Q
W_ML0 = W_KV0 + 2 * ATT_KV
W_MIX = W_ML0 + 4 * ML_INNER
IN_MLG0 = W_MIX
IN_Z0 = IN_MLG0 + 4 * ML_HEADS
IN_DT0 = IN_Z0 + SSD_INNER + SSD_XBC
IN_G0 = IN_DT0 + 2 * SSD_HEADS
DT_LANE0 = 4 * ML_HEADS


def _cparams(sem):
    return pltpu.CompilerParams(dimension_semantics=sem, vmem_limit_bytes=VMEM_LIMIT)


def _resident(stacked_shape, layer, depth):
    rows, cols = stacked_shape
    return pl.BlockSpec((rows // depth, cols), lambda *_: (layer, 0), pipeline_mode=pl.Buffered(1))


def _stack_rows(w):
    return w.reshape(w.shape[0] * w.shape[1], w.shape[2])


def _sigmoid(x):
    return 1.0 / (1.0 + jnp.exp(-x))


def _softplus(x):
    return jnp.maximum(x, 0.0) + jnp.log1p(jnp.exp(-jnp.abs(x)))


def _dot(a, b):
    return jnp.dot(a, b, preferred_element_type=F32)


def _dot_nt(a, b):
    return lax.dot_general(a, b, (((1,), (1,)), ((), ())), preferred_element_type=F32)


def _rmsnorm(x, gain):
    return x * lax.rsqrt(jnp.mean(x * x, axis=-1, keepdims=True) + EPS) * gain


def _mod_kernel(c_ref, w_ref, b_ref, o_ref):
    c = c_ref[...]
    a = (c * _sigmoid(c)).astype(BF16)
    o_ref[...] = _dot(a, w_ref[...].astype(BF16)) + b_ref[...]


def _modulation(cc, w_mod, b_mod):
    depth, d, n = w_mod.shape
    tn = 3072
    return pl.pallas_call(
        _mod_kernel,
        out_shape=jax.ShapeDtypeStruct((depth, MOD_ROWS, n), F32),
        grid=(depth, n // tn),
        in_specs=[
            pl.BlockSpec((MOD_ROWS, d), lambda l, j: (0, 0)),
            pl.BlockSpec((None, d, tn), lambda l, j: (l, 0, j)),
            pl.BlockSpec((None, 1, tn), lambda l, j: (l, 0, j)),
        ],
        out_specs=pl.BlockSpec((None, MOD_ROWS, tn), lambda l, j: (l, 0, j)),
        compiler_params=_cparams(("arbitrary", "arbitrary")),
        name="modulation",
    )(cc, w_mod, b_mod.reshape(depth, 1, n))


def _stream_specs(first, groups, dims):
    n_b, nct, nlt = dims["B"], dims["nct"], dims["nlt"]
    per_chunk = n_b // groups
    if not first:
        return [pl.BlockSpec((groups, CHUNK, D_MODEL), lambda i: (i, 0, 0))]

    def ctx_idx(i):
        p = jnp.minimum(i // per_chunk, nct - 1)
        return (jnp.where(i < nct * per_chunk, i % per_chunk, per_chunk - 1), p, 0, 0)

    def lat_idx(i):
        p = jnp.maximum(i // per_chunk - nct, 0)
        return (jnp.where(i < nct * per_chunk, 0, i % per_chunk), p, 0, 0)

    return [pl.BlockSpec((groups, None, CHUNK, D_MODEL), ctx_idx),
            pl.BlockSpec((groups, None, CHUNK, D_MODEL), lat_idx)]


def _mod_spec(layer, groups, dims, tile0=0):
    n_b, nct = dims["B"], dims["nct"]
    per_chunk = n_b // groups

    def idx(i):
        i = i + tile0
        return (layer, jnp.where(i < nct * per_chunk, per_chunk, i % per_chunk), 0, 0)

    return pl.BlockSpec((None, groups, N_MOD, D_MODEL), idx)


def _read_stream(refs, k, is_ctx_tile):
    if len(refs) == 1:
        return refs[0][k]
    return jnp.where(is_ctx_tile, refs[0][k], refs[1][k])


def _rope(x, cos, sin_signed, first_half):
    partner = jnp.where(first_half, pltpu.roll(x, LANES - ROPE_FREQS, 1), pltpu.roll(x, ROPE_FREQS, 1))
    return x * cos + partner * sin_signed


def _inproj_kernel(*refs, n_src, ctx_tiles):
    x_refs = refs[:n_src]
    (mod_ref, g_ref, cos_ref, sin_ref, gb_ref, sb_ref, w_mix_ref, w_ssd_ref, w_merge_ref, w_small_ref,
     oq_ref, okv_ref, oml_ref, oz_ref, oxbc_ref, og_ref,
     mr_ref, mc_ref, sr_ref, sc_ref, h_ref) = refs[n_src:]
    is_ctx_tile = pl.program_id(0) < ctx_tiles
    gain = g_ref[...]
    for k in range(PROJ_GROUPS):
        mod = mod_ref[k]
        y = _rmsnorm(_read_stream(x_refs, k, is_ctx_tile), gain)
        h_ref[k * CHUNK:(k + 1) * CHUNK, :] = (y * (1.0 + mod[1:2]) + mod[0:1]).astype(BF16)
    h = h_ref[...]

    cos = jnp.concatenate([cos_ref[...]] * PROJ_GROUPS, axis=0)
    sin = jnp.concatenate([sin_ref[...]] * PROJ_GROUPS, axis=0)
    lane = lax.broadcasted_iota(jnp.int32, (1, LANES), 1)
    first_half = (lane % (2 * ROPE_FREQS)) < ROPE_FREQS

    def proj(c0, width, w_ref=w_mix_ref):
        return _dot(h, w_ref[:, c0:c0 + width])

    _gate_prep_tile(proj(0, LANES, w_small_ref), gb_ref, sb_ref, mr_ref, mc_ref, sr_ref, sc_ref)

    q = proj(W_Q0, ATT_Q)
    for s in range(ATT_Q // LANES):
        qs = _rope(q[:, s * LANES:(s + 1) * LANES], cos, sin, first_half)
        oq_ref[:, s * LANES:(s + 1) * LANES] = (qs * ATT_HEAD_DIM ** -0.5).astype(oq_ref.dtype)
    kv = proj(W_KV0, 2 * ATT_KV)
    okv_ref[:, 0:ATT_KV] = _rope(kv[:, 0:ATT_KV], cos, sin, first_half).astype(okv_ref.dtype)
    okv_ref[:, ATT_KV:] = kv[:, ATT_KV:].astype(okv_ref.dtype)

    oml_ref[:, 0:ML_INNER] = (proj(W_ML0, ML_INNER) * ML_HEAD_DIM ** -0.5).astype(oml_ref.dtype)
    kproj = proj(W_ML0 + ML_INNER, ML_INNER)
    for r in range(PROJ_GROUPS):
        for hd in range(ML_HEADS):
            blk = kproj[r * CHUNK:(r + 1) * CHUNK, hd * ML_HEAD_DIM:(hd + 1) * ML_HEAD_DIM]
            oml_ref[r * CHUNK:(r + 1) * CHUNK,
                    ML_INNER + hd * ML_HEAD_DIM:ML_INNER + (hd + 1) * ML_HEAD_DIM] = blk.T.astype(oml_ref.dtype)
    for s in range(2, 4):
        oml_ref[:, s * ML_INNER:(s + 1) * ML_INNER] = proj(W_ML0 + s * ML_INNER, ML_INNER).astype(oml_ref.dtype)

    oz_ref[...] = proj(0, SSD_INNER, w_ssd_ref).astype(oz_ref.dtype)
    for s in range(SSD_XBC // 512):
        oxbc_ref[:, s * 512:(s + 1) * 512] = proj(SSD_INNER + s * 512, 512, w_ssd_ref).astype(oxbc_ref.dtype)
    for s in range(3 * D_MODEL // 512):
        og_ref[:, s * 512:(s + 1) * 512] = proj(s * 512, 512, w_merge_ref).astype(og_ref.dtype)


def _in_projection(stream, first, mods, layer, g1, cos_t, sin_t, w_pieces, gbias, sbias, dims):
    n_b, nct = dims["B"], dims["nct"]
    n_tok = dims["n_tok"]
    n_groups = n_tok // CHUNK
    tm = PROJ_GROUPS * CHUNK
    per_chunk = n_b // PROJ_GROUPS
    outs = [
        jax.ShapeDtypeStruct((n_tok, ATT_Q), BF16),
        jax.ShapeDtypeStruct((n_tok, 2 * ATT_KV), BF16),
        jax.ShapeDtypeStruct((n_tok, 4 * ML_INNER), BF16),
        jax.ShapeDtypeStruct((n_tok, SSD_INNER), F32),
        jax.ShapeDtypeStruct((n_tok, SSD_XBC), F32),
        jax.ShapeDtypeStruct((n_tok, 3 * D_MODEL), BF16),
    ]
    gate_outs = [
        jax.ShapeDtypeStruct((2, n_groups, ML_ROWS, LANES), F32),
        jax.ShapeDtypeStruct((2, n_tok, LANES), F32),
        jax.ShapeDtypeStruct((2, n_groups, SSD_ROWS, LANES), F32),
        jax.ShapeDtypeStruct((2, n_tok, LANES), F32),
    ]
    gate_specs = [
        pl.BlockSpec((2, PROJ_GROUPS, ML_ROWS, LANES), lambda i: (0, i, 0, 0)),
        pl.BlockSpec((2, tm, LANES), lambda i: (0, i, 0)),
        pl.BlockSpec((2, PROJ_GROUPS, SSD_ROWS, LANES), lambda i: (0, i, 0, 0)),
        pl.BlockSpec((2, tm, LANES), lambda i: (0, i, 0)),
    ]
    return pl.pallas_call(
        functools.partial(_inproj_kernel, n_src=len(stream), ctx_tiles=nct * per_chunk),
        out_shape=outs + gate_outs,
        grid=(n_tok // tm,),
        in_specs=_stream_specs(first, PROJ_GROUPS, dims) + [
            _mod_spec(layer, PROJ_GROUPS, dims),
            pl.BlockSpec((1, D_MODEL), lambda i: (0, 0)),
            pl.BlockSpec((CHUNK, LANES), lambda i: (i // per_chunk, 0)),
            pl.BlockSpec((CHUNK, LANES), lambda i: (i // per_chunk, 0)),
            pl.BlockSpec((2, 8, LANES), lambda i: (0, 0, 0)),
            pl.BlockSpec((4, SSD_HEADS, LANES), lambda i: (0, 0, 0)),
        ] + [_resident(w.shape, layer, mods.shape[0]) for w in w_pieces],
        out_specs=[pl.BlockSpec((tm, o.shape[1]), lambda i: (i, 0)) for o in outs] + gate_specs,
        scratch_shapes=[pltpu.VMEM((tm, D_MODEL), BF16)],
        compiler_params=_cparams(("arbitrary",)),
        name="in_projection",
    )(*stream, mods, g1, cos_t, sin_t, gbias, sbias, *w_pieces)


def _attention_blocks(qs, kvs, valid, sink, o_refs):
    lane = lax.broadcasted_iota(jnp.int32, (1, LANES), 1)
    low = lane < ATT_HEAD_DIM
    ones = jnp.ones((kvs[0].shape[0], LANES), BF16)
    zero = jnp.zeros((CHUNK, LANES), BF16)
    units = [(i, g) for i in range(len(qs)) for g in range(ATT_KV_HEADS)]

    scores, values = [], []
    for i, g in units:
        k_g = kvs[i][:, g * ATT_HEAD_DIM:(g + 1) * ATT_HEAD_DIM]
        v_g = kvs[i][:, ATT_KV + g * ATT_HEAD_DIM:ATT_KV + (g + 1) * ATT_HEAD_DIM]
        kk = jnp.concatenate([k_g, k_g], axis=1)
        values.append(jnp.concatenate([v_g, v_g, ones], axis=1))
        q_rows = []
        for pair in range(ATT_GROUP // 2):
            c0 = (g * ATT_GROUP + 2 * pair) * ATT_HEAD_DIM
            qp = qs[i][:, c0:c0 + LANES]
            q_rows += [jnp.where(low, qp, zero), jnp.where(low, zero, qp)]
        scores.append(_dot_nt(jnp.concatenate(q_rows, axis=0), kk))

    probs, sink_terms = [], []
    for (i, g), s_all in zip(units, scores):
        p_rows, t_rows = [], []
        for r in range(ATT_GROUP):
            s = s_all[r * CHUNK:(r + 1) * CHUNK]
            if valid is not None:
                s = jnp.where(valid, s, -jnp.inf)
            sk = sink[:, g * ATT_GROUP + r:g * ATT_GROUP + r + 1]
            m = jnp.maximum(jnp.max(s, axis=-1, keepdims=True), sk)
            p_rows.append(jnp.exp((s - m).astype(BF16)))
            t_rows.append(jnp.exp(sk - m))
        probs.append(jnp.concatenate(p_rows, axis=0))
        sink_terms.append(t_rows)

    outs = [_dot(p, vw) for p, vw in zip(probs, values)]

    for (i, g), o_all, t_rows in zip(units, outs, sink_terms):
        heads = []
        for r in range(ATT_GROUP):
            o = o_all[r * CHUNK:(r + 1) * CHUNK]
            heads.append(o[:, :LANES] / (o[:, LANES:] + t_rows[r]))
        for pair in range(ATT_GROUP // 2):
            c0 = (g * ATT_GROUP + 2 * pair) * ATT_HEAD_DIM
            o_refs[i][:, c0:c0 + LANES] = jnp.where(low, heads[2 * pair], heads[2 * pair + 1]).astype(o_refs[i].dtype)


def _attention_kernel(*refs, nct, nlt, ctx_queries):
    q_ref = refs[0]
    loc_refs = refs[1:4]
    ctx_refs = refs[4:4 + nct]
    sink_ref, o_ref = refs[4 + nct:]
    step = pl.program_id(1)
    sink = sink_ref[...]

    def rows_of(i):
        return slice(i * CHUNK, (i + 1) * CHUNK)

    def latent(j):
        span = 3 * CHUNK
        first = jnp.clip(j - 1, 0, nlt - 3)
        n_keys = span + nct * CHUNK
        row = lax.broadcasted_iota(jnp.int32, (CHUNK, n_keys), 0)
        col = lax.broadcasted_iota(jnp.int32, (CHUNK, n_keys), 1)
        dist = (j - first) * CHUNK + row - col
        valid = (jnp.abs(dist) <= ATT_WINDOW) | (col >= span)
        kvs = [jnp.concatenate([r[rows_of(i), :] for r in loc_refs + ctx_refs], axis=0) for i in range(ATT_BATCH)]
        _attention_blocks([q_ref[rows_of(i), :] for i in range(ATT_BATCH)], kvs, valid, sink,
                          [o_ref.at[rows_of(i), :] for i in range(ATT_BATCH)])

    if not ctx_queries:
        latent(step)
        return

    @pl.when(step < nct)
    def _():
        kvs = [jnp.concatenate([r[rows_of(i), :] for r in ctx_refs], axis=0) for i in range(ATT_BATCH)]
        _attention_blocks([q_ref[rows_of(i), :] for i in range(ATT_BATCH)], kvs, None, sink,
                          [o_ref.at[rows_of(i), :] for i in range(ATT_BATCH)])

    @pl.when(step >= nct)
    def _():
        latent(step - nct)


def _attention(q, kv, sink_row, dims, ctx_queries):
    n_b, nct, nlt = dims["B"], dims["nct"], dims["nlt"]
    c0 = 0 if ctx_queries else nct
    n_steps = nct + nlt - c0

    per_chunk = n_b // ATT_BATCH
    rows = ATT_BATCH * CHUNK

    def win(i):
        def idx(b, s):
            first = jnp.clip(s + c0 - nct - 1, 0, nlt - 3)
            return ((nct + first + i) * per_chunk + b, 0)
        return idx

    return pl.pallas_call(
        functools.partial(_attention_kernel, nct=nct, nlt=nlt, ctx_queries=ctx_queries),
        out_shape=jax.ShapeDtypeStruct((n_steps * n_b * CHUNK, ATT_Q), BF16),
        grid=(per_chunk, n_steps),
        in_specs=[pl.BlockSpec((rows, ATT_Q), lambda b, s: ((s + c0) * per_chunk + b, 0))]
        + [pl.BlockSpec((rows, 2 * ATT_KV), win(i)) for i in range(3)]
        + [pl.BlockSpec((rows, 2 * ATT_KV), functools.partial(lambda b, s, i: (i * per_chunk + b, 0), i=i))
           for i in range(nct)]
        + [pl.BlockSpec((1, LANES), lambda b, s: (0, 0))],
        out_specs=pl.BlockSpec((rows, ATT_Q), lambda b, s: (s * per_chunk + b, 0)),
        compiler_params=_cparams(("arbitrary", "arbitrary")),
        name="attention",
    )(q, *([kv] * (3 + nct)), sink_row)


def _scan_chunk(d, c, nct, nlt):
    fwd = c
    bwd = jnp.where(c < nct, nct - 1 - c, nct + nlt - 1 - (c - nct))
    return jnp.where(d == 0, fwd, bwd)


def _causal_mask(d):
    ri = lax.broadcasted_iota(jnp.int32, (CHUNK, CHUNK), 0)
    ci = lax.broadcasted_iota(jnp.int32, (CHUNK, CHUNK), 1)
    return (ci - ri) * (1 - 2 * d) <= 0


def _scan_lanes(x8, backward, combine, identity):
    lane = lax.broadcasted_iota(jnp.int32, (1, LANES), 1)
    k = 1
    while k < LANES:
        if backward:
            shifted = jnp.where(lane < LANES - k, pltpu.roll(x8, LANES - k, 1), identity)
        else:
            shifted = jnp.where(lane >= k, pltpu.roll(x8, k, 1), identity)
        x8 = combine(x8, shifted)
        k *= 2
    return x8


def _cummax_lanes(x8, backward):
    return _scan_lanes(x8, backward, jnp.maximum, -jnp.inf)


def _rows_to_columns(x8):
    pad = jnp.zeros((LANES - 8, LANES), F32)
    return jnp.concatenate([x8, pad], axis=0).T


def _lane_fill(cols, j):
    return jnp.broadcast_to(cols[:, j:j + 1], cols.shape)


ML_ROWS = 16
SSD_ROWS = 32


def _gate_prep_tile(small, gb_ref, sb_ref, mr_ref, mc_ref, sr_ref, sc_ref):
    sub8 = lax.broadcasted_iota(jnp.int32, (8, 1), 0)
    head_rows = sub8 < ML_HEADS
    for r in range(small.shape[0] // CHUNK):
        rows = slice(r * CHUNK, (r + 1) * CHUNK)
        gt = small[rows, :].T
        for dd in range(2):
            g8 = gt[8 * dd:8 * dd + 8] + gb_ref[dd]
            g8 = jnp.where(head_rows, g8, jnp.minimum(g8, 0.0) - jnp.log1p(jnp.exp(-jnp.abs(g8)))) * LOG2E
            dt8 = _softplus(gt[DT_LANE0 + 8 * dd:DT_LANE0 + 8 * dd + 8] + sb_ref[dd])
            la8 = dt8 * sb_ref[2 + dd]
            gsum = _scan_lanes(g8, dd == 1, jnp.add, 0.0)
            acs8 = _scan_lanes(la8, dd == 1, jnp.add, 0.0)

            b8 = pltpu.roll(gsum, ML_HEADS, 0)
            c8 = g8 - b8
            cm8 = _cummax_lanes(c8, backward=dd == 1)
            cm_end = jnp.broadcast_to(jnp.max(cm8, axis=1, keepdims=True), cm8.shape)
            b_end = jnp.broadcast_to(jnp.sum(g8, axis=1, keepdims=True), g8.shape)
            mr_ref[dd, r, 0:8] = jnp.where(head_rows, c8, 0.0)
            mr_ref[dd, r, 8:16] = jnp.where(head_rows, cm_end, b_end)
            mc_ref[dd, rows, :] = _rows_to_columns(jnp.where(head_rows, cm8, gsum))

            a_end = jnp.sum(la8, axis=1, keepdims=True)
            sr_ref[dd, r, 0:8] = dt8
            sr_ref[dd, r, 8:16] = acs8 * LOG2E
            sr_ref[dd, r, 16:24] = dt8 * jnp.exp(a_end - acs8)
            sr_ref[dd, r, 24:32] = jnp.broadcast_to(jnp.exp(a_end), dt8.shape)
            sc_ref[dd, rows, :] = _rows_to_columns(acs8 * LOG2E)


def _mlstm_kernel(ml_ref, mr_ref, mc_ref, gain_ref, o_ref, ct_ref, m_ref, hf_ref, *, nct, nlt):
    d = pl.program_id(1)
    c = pl.program_id(2)

    @pl.when(c == 0)
    def _():
        ct_ref[...] = jnp.zeros_like(ct_ref)
        m_ref[...] = jnp.zeros_like(m_ref)

    p = _scan_chunk(d, c, nct, nlt)
    sub8 = lax.broadcasted_iota(jnp.int32, (8, 1), 0)
    causal = _causal_mask(d)
    ones = jnp.ones((CHUNK, ML_HEAD_DIM), BF16)
    gain = gain_ref[...]

    rows_of = [slice(i * CHUNK, (i + 1) * CHUNK) for i in range(ML_BATCH)]
    units = [(i, h) for i in range(ML_BATCH) for h in range(ML_HEADS)]
    c8, m8, w8, dec8, cols = [], [], [], [], []
    for i in range(ML_BATCH):
        c8.append(mr_ref[i, 0:8])
        m8.append(m_ref[i])
        ends = mr_ref[i, 8:16]
        mx_end = jnp.maximum(ends, m8[i])
        w8.append(jnp.exp2(c8[i] - mx_end))
        dec8.append(jnp.exp2(m8[i] - mx_end))
        m_ref[i] = jnp.where(sub8 < ML_HEADS, pltpu.roll(ends, ML_HEADS, 0) + mx_end, 0.0)
        cols.append(mc_ref[rows_of[i], :])

    def piece(i, h, which):
        c0 = which * ML_INNER + h * ML_HEAD_DIM
        return ml_ref[rows_of[i], c0:c0 + ML_HEAD_DIM]

    q = {u: piece(*u, 0) for u in units}
    k_t = {u: piece(*u, 1) for u in units}
    v1 = {u: jnp.concatenate([piece(*u, 2), ones], axis=1) for u in units}
    ctn = {(i, h): ct_ref[i * ML_HEADS + h] for i, h in units}

    qk = {u: _dot(q[u], k_t[u]) for u in units}
    qc = {u: _dot(q[u], ctn[u].astype(BF16)) for u in units}

    s, k_w, cm_t = {}, {}, {}
    for i, h in units:
        cm_t[i, h] = _lane_fill(cols[i], h)
        dmat = jnp.exp2(jnp.where(causal, c8[i][h:h + 1, :] - cm_t[i, h], -jnp.inf))
        s[i, h] = (qk[i, h] * dmat).astype(BF16)
        k_w[i, h] = (k_t[i, h].astype(F32) * w8[i][h:h + 1, :]).astype(BF16)

    sv = {u: _dot(s[u], v1[u]) for u in units}
    kv = {u: _dot(k_w[u], v1[u]) for u in units}

    h_dirs = []
    for i in range(ML_BATCH):
        hs = []
        for h in range(ML_HEADS):
            m_row = m8[i][h:h + 1, :]
            mx_t = jnp.maximum(cm_t[i, h], m_row)
            intra = jnp.exp2(cm_t[i, h] - mx_t)
            prev = jnp.exp2(m_row - mx_t)
            nd = (jnp.concatenate([intra, intra], axis=1) * sv[i, h]
                  + jnp.concatenate([prev, prev], axis=1) * qc[i, h])
            floor = jnp.exp2(-(_lane_fill(cols[i], ML_HEADS + h) + mx_t))
            hs.append(nd[:, :ML_HEAD_DIM] / jnp.maximum(jnp.abs(nd[:, ML_HEAD_DIM:]), floor))
            dec = dec8[i][h:h + 1, :]
            ct_ref[i * ML_HEADS + h] = jnp.concatenate([dec, dec], axis=1) * ctn[i, h] + kv[i, h]
        h_dirs.append(jnp.concatenate(hs, axis=1))

    @pl.when(d == 0)
    def _():
        hf_ref[p] = jnp.concatenate(h_dirs, axis=0)

    @pl.when(d == 1)
    def _():
        tot = hf_ref[p] + jnp.concatenate(h_dirs, axis=0)
        for h in range(ML_HEADS):
            sl = slice(h * ML_HEAD_DIM, (h + 1) * ML_HEAD_DIM)
            o_gate = ml_ref[:, 3 * ML_INNER + h * ML_HEAD_DIM:3 * ML_INNER + (h + 1) * ML_HEAD_DIM].astype(F32)
            o_ref[:, sl] = (_rmsnorm(tot[:, sl], gain[:, sl]) * _sigmoid(o_gate)).astype(o_ref.dtype)


def _scan_block_maps(dims, batch):
    nct, nlt = dims["nct"], dims["nlt"]
    per_chunk = dims["B"] // batch
    nc = nct + nlt

    def blk(bg, d, c):
        return _scan_chunk(d, c, nct, nlt) * per_chunk + bg

    def fwd_blk(bg, d, c):
        return jnp.where(d == 0, blk(bg, 0, c), blk(bg, 0, nc - 1))

    def bwd_blk(bg, d, c):
        return jnp.where(d == 0, blk(bg, 1, 0), blk(bg, 1, c))

    return blk, fwd_blk, bwd_blk


def _mlstm(ml, gate_rows, gate_cols, gain_row, dims):
    nct, nlt = dims["nct"], dims["nlt"]
    nc = nct + nlt
    rows = ML_BATCH * CHUNK
    blk, _, bwd_blk = _scan_block_maps(dims, ML_BATCH)
    return pl.pallas_call(
        functools.partial(_mlstm_kernel, nct=nct, nlt=nlt),
        out_shape=jax.ShapeDtypeStruct((dims["n_tok"], ML_INNER), BF16),
        grid=(dims["B"] // ML_BATCH, 2, nc),
        in_specs=[
            pl.BlockSpec((rows, 4 * ML_INNER), lambda bg, d, c: (blk(bg, d, c), 0)),
            pl.BlockSpec((None, ML_BATCH, ML_ROWS, LANES), lambda bg, d, c: (d, blk(bg, d, c), 0, 0)),
            pl.BlockSpec((None, rows, LANES), lambda bg, d, c: (d, blk(bg, d, c), 0)),
            pl.BlockSpec((1, ML_INNER), lambda bg, d, c: (0, 0)),
        ],
        out_specs=pl.BlockSpec((rows, ML_INNER), lambda bg, d, c: (bwd_blk(bg, d, c), 0)),
        scratch_shapes=[
            pltpu.VMEM((ML_BATCH * ML_HEADS, ML_HEAD_DIM, 2 * ML_HEAD_DIM), F32),
            pltpu.VMEM((ML_BATCH, 8, LANES), F32),
            pltpu.VMEM((nc, rows, ML_INNER), F32),
        ],
        compiler_params=_cparams(("arbitrary", "arbitrary", "arbitrary")),
        name="mlstm",
    )(ml, gate_rows, gate_cols, gain_row)


def _ssd_kernel(*refs, nct, nlt):
    xbc_ref = refs[0]
    prev_refs = refs[1:1 + SCAN_BATCH]
    next_refs = refs[1 + SCAN_BATCH:1 + 2 * SCAN_BATCH]
    (z_ref, sr_ref, sc_ref, cw_ref, cb_ref, dsk_ref, gain_ref,
     o_ref, st_ref, xa_ref, yf_ref, ext_ref) = refs[1 + 2 * SCAN_BATCH:]
    d = pl.program_id(1)
    c = pl.program_id(2)

    @pl.when(c == 0)
    def _():
        st_ref[...] = jnp.zeros_like(st_ref)

    p = _scan_chunk(d, c, nct, nlt)
    lane = lax.broadcasted_iota(jnp.int32, (1, LANES), 1)
    low = lane < SSD_HEAD_DIM
    causal = _causal_mask(d)

    @pl.when(d == 0)
    def _():
        has_prev = jnp.where((p != 0) & (p != nct), 1.0, 0.0)
        has_next = jnp.where((p != nct - 1) & (p != nct + nlt - 1), 1.0, 0.0)
        cw = cw_ref[...]
        cbias = cb_ref[...]
        for i in range(SCAN_BATCH):
            rows = slice(i * CHUNK, (i + 1) * CHUNK)
            for lt in range(SSD_XBC // LANES):
                ln = slice(lt * LANES, (lt + 1) * LANES)
                ext_ref[i, lt, 0:HALO, :] = prev_refs[i][:, ln].astype(F32) * has_prev
                ext_ref[i, lt, HALO:HALO + CHUNK, :] = xbc_ref[rows, ln].astype(F32)
                ext_ref[i, lt, HALO + CHUNK:, :] = next_refs[i][:, ln].astype(F32) * has_next
                acc = jnp.zeros((CHUNK, LANES), F32) + cbias[:, ln]
                for tap in range(SSD_CONV):
                    off = HALO - SSD_CONV // 2 + tap
                    acc = acc + ext_ref[i, lt, off:off + CHUNK, :] * cw[tap:tap + 1, ln]
                xa_ref[p, lt, rows, :] = acc * _sigmoid(acc)

    x_tile0, b_tile0, c_tile0 = 0, SSD_INNER // LANES, SSD_INNER // LANES + SSD_GROUPS
    rows_of = [slice(i * CHUNK, (i + 1) * CHUNK) for i in range(SCAN_BATCH)]
    groups = [(i, g) for i in range(SCAN_BATCH) for g in range(SSD_GROUPS)]
    heads = [(i, h) for i in range(SCAN_BATCH) for h in range(SSD_HEADS)]
    dt8 = [sr_ref[i, 0:8] for i in range(SCAN_BATCH)]
    acs8 = [sr_ref[i, 8:16] for i in range(SCAN_BATCH)]
    dw8 = [sr_ref[i, 16:24] for i in range(SCAN_BATCH)]
    dec8 = [sr_ref[i, 24:32] for i in range(SCAN_BATCH)]
    cols = [sc_ref[rows_of[i], :] for i in range(SCAN_BATCH)]

    cb, bm_t, st, y_state = {}, {}, {}, {}
    for i, g in groups:
        bm_f = xa_ref[p, b_tile0 + g, rows_of[i], :]
        cm = xa_ref[p, c_tile0 + g, rows_of[i], :].astype(BF16)
        cb[i, g] = _dot_nt(cm, bm_f.astype(BF16))
        bm_t[i, g] = bm_f.T
        st[i, g] = st_ref[i * SSD_GROUPS + g]
        y_state[i, g] = _dot(cm, st[i, g].astype(BF16))

    mmat, bw, xm, grow = {}, {}, {}, {}
    for i, h in heads:
        g = h // SSD_HPG
        a_t = _lane_fill(cols[i], h)
        e = jnp.exp2(jnp.where(causal, a_t - acs8[i][h:h + 1, :], -jnp.inf))
        mmat[i, h] = (cb[i, g] * e * dt8[i][h:h + 1, :]).astype(BF16)
        bw[i, h] = (bm_t[i, g] * dw8[i][h:h + 1, :]).astype(BF16)
        x_pair = xa_ref[p, x_tile0 + h // 2, rows_of[i], :].astype(BF16)
        xm[i, h] = jnp.where(low if h % 2 == 0 else ~low, x_pair, jnp.zeros_like(x_pair))
        grow[i, h] = jnp.exp2(a_t)

    y_h = {u: _dot(mmat[u], xm[u]) for u in heads}
    s_h = {u: _dot(bw[u], xm[u]) for u in heads}

    y_dirs = []
    for i in range(SCAN_BATCH):
        ys = []
        for g in range(SSD_GROUPS):
            st_parts = []
            for pair in range(SSD_HPG // 2):
                h0 = g * SSD_HPG + 2 * pair
                lanes = slice(pair * LANES, (pair + 1) * LANES)
                ys.append(y_h[i, h0] + y_h[i, h0 + 1]
                          + jnp.where(low, grow[i, h0], grow[i, h0 + 1]) * y_state[i, g][:, lanes])
                dec = jnp.where(low, dec8[i][h0:h0 + 1, :], dec8[i][h0 + 1:h0 + 2, :])
                st_parts.append(dec * st[i, g][:, lanes] + s_h[i, h0] + s_h[i, h0 + 1])
            st_ref[i * SSD_GROUPS + g] = jnp.concatenate(st_parts, axis=1)
        y_dirs.append(jnp.concatenate(ys, axis=1))

    @pl.when(d == 0)
    def _():
        yf_ref[p] = jnp.concatenate(y_dirs, axis=0)

    @pl.when(d == 1)
    def _():
        xs = jnp.concatenate([xa_ref[p, x_tile0 + lt] for lt in range(SSD_INNER // LANES)], axis=1)
        y = yf_ref[p] + jnp.concatenate(y_dirs, axis=0) + dsk_ref[...] * xs
        z = z_ref[...].astype(F32)
        o_ref[...] = _rmsnorm(y * (z * _sigmoid(z)), gain_ref[...]).astype(o_ref.dtype)


def _ssd(z, xbc, gate_rows, gate_cols, conv_w, conv_b, dskip_row, gain_row, dims):
    n_b, nct, nlt = dims["B"], dims["nct"], dims["nlt"]
    nc = nct + nlt
    rows = SCAN_BATCH * CHUNK
    sub = CHUNK // HALO
    n_halo_blocks = dims["n_tok"] // HALO
    blk, fwd_blk, bwd_blk = _scan_block_maps(dims, SCAN_BATCH)

    def halo(i, side):
        def idx(bg, d, c):
            group = fwd_blk(bg, d, c) * SCAN_BATCH + i
            if side < 0:
                return (jnp.maximum((group - n_b) * sub + sub - 1, 0), 0)
            return (jnp.minimum((group + n_b) * sub, n_halo_blocks - 1), 0)
        return idx

    const = lambda bg, d, c: (0, 0)
    return pl.pallas_call(
        functools.partial(_ssd_kernel, nct=nct, nlt=nlt),
        out_shape=jax.ShapeDtypeStruct((dims["n_tok"], SSD_INNER), BF16),
        grid=(n_b // SCAN_BATCH, 2, nc),
        in_specs=[pl.BlockSpec((rows, SSD_XBC), lambda bg, d, c: (fwd_blk(bg, d, c), 0))]
        + [pl.BlockSpec((HALO, SSD_XBC), halo(i, -1)) for i in range(SCAN_BATCH)]
        + [pl.BlockSpec((HALO, SSD_XBC), halo(i, +1)) for i in range(SCAN_BATCH)]
        + [
            pl.BlockSpec((rows, SSD_INNER), lambda bg, d, c: (bwd_blk(bg, d, c), 0)),
            pl.BlockSpec((None, SCAN_BATCH, SSD_ROWS, LANES), lambda bg, d, c: (d, blk(bg, d, c), 0, 0)),
            pl.BlockSpec((None, rows, LANES), lambda bg, d, c: (d, blk(bg, d, c), 0)),
            pl.BlockSpec((8, SSD_XBC), const),
            pl.BlockSpec((1, SSD_XBC), const),
            pl.BlockSpec((1, SSD_INNER), const),
            pl.BlockSpec((1, SSD_INNER), const),
        ],
        out_specs=pl.BlockSpec((rows, SSD_INNER), lambda bg, d, c: (bwd_blk(bg, d, c), 0)),
        scratch_shapes=[
            pltpu.VMEM((SCAN_BATCH * SSD_GROUPS, SSD_STATE, SSD_HPG * SSD_HEAD_DIM), F32),
            pltpu.VMEM((nc, SSD_XBC // LANES, rows, LANES), F32),
            pltpu.VMEM((nc, rows, SSD_INNER), F32),
            pltpu.VMEM((SCAN_BATCH, SSD_XBC // LANES, CHUNK + 2 * HALO, LANES), F32),
        ],
        compiler_params=_cparams(("arbitrary", "arbitrary", "arbitrary")),
        name="ssd",
    )(xbc, *([xbc] * (2 * SCAN_BATCH)), z, gate_rows, gate_cols, conv_w, conv_b, dskip_row, gain_row)


def _merge_ffn_kernel(*refs, n_src, ctx_tiles, final_norm):
    x_refs = refs[:n_src]
    (att_ref, ml_ref, ss_ref, gt_ref, mod_ref, g2_ref, gf_ref,
     wa_ref, wm_ref, ws_ref, wo_ref, wup_ref, wdn_ref, o_ref, x1_ref, h_ref) = refs[n_src:]
    is_ctx_tile = pl.program_id(0) < ctx_tiles
    y = None
    for i, (src, w) in enumerate(((att_ref, wa_ref), (ml_ref, wm_ref), (ss_ref, ws_ref))):
        gate = _sigmoid(gt_ref[:, i * D_MODEL:(i + 1) * D_MODEL].astype(F32))
        term = gate * _dot(src[...], w[...])
        y = term if y is None else y + term
    yo = _dot(y.astype(BF16), wo_ref[...])
    g2 = g2_ref[...]
    for k in range(FFN_GROUPS):
        rows = slice(k * CHUNK, (k + 1) * CHUNK)
        mod = mod_ref[k]
        x1 = _read_stream(x_refs, k, is_ctx_tile) + mod[2:3] * yo[rows]
        x1_ref[rows, :] = x1
        h_ref[rows, :] = (_rmsnorm(x1, g2) * (1.0 + mod[4:5]) + mod[3:4]).astype(BF16)
    h = h_ref[...]
    acc = None
    for c0, width in FFN_CHUNKS:
        gate = _dot(h, wup_ref[:, c0:c0 + width])
        up = _dot(h, wup_ref[:, D_FF + c0:D_FF + c0 + width])
        act = (gate * _sigmoid(gate) * up).astype(BF16)
        part = _dot(act, wdn_ref[c0:c0 + width, :])
        acc = part if acc is None else acc + part
    for k in range(FFN_GROUPS):
        rows = slice(k * CHUNK, (k + 1) * CHUNK)
        x2 = x1_ref[rows, :] + mod_ref[k][5:6] * acc[rows]
        if final_norm:
            x2 = _rmsnorm(x2, gf_ref[...])
        o_ref[k] = x2


def _merge_ffn(stream, first, att, ml, ss, gates, mods, layer, g2, gf, wts, dims, last):
    n_b, nct, nlt = dims["B"], dims["nct"], dims["nlt"]
    tm = FFN_GROUPS * CHUNK
    per_chunk = n_b // FFN_GROUPS
    ctx_tiles = nct * per_chunk
    t0 = ctx_tiles if last else 0
    n_tiles = (nct + nlt) * per_chunk - t0
    assert not (first and last)

    row = lambda i: (i + t0, 0)
    att_row = (lambda i: (i, 0)) if last else row
    if last:
        out_shape = jax.ShapeDtypeStruct((n_b, nlt, CHUNK, D_MODEL), F32)
        out_spec = pl.BlockSpec((FFN_GROUPS, None, CHUNK, D_MODEL), lambda i: (i % per_chunk, i // per_chunk, 0, 0))
        stream_specs = [pl.BlockSpec((FFN_GROUPS, CHUNK, D_MODEL), lambda i: (i + t0, 0, 0))]
    else:
        out_shape = jax.ShapeDtypeStruct(((nct + nlt) * n_b, CHUNK, D_MODEL), F32)
        out_spec = pl.BlockSpec((FFN_GROUPS, CHUNK, D_MODEL), lambda i: (i, 0, 0))
        stream_specs = _stream_specs(first, FFN_GROUPS, dims)
    return pl.pallas_call(
        functools.partial(_merge_ffn_kernel, n_src=len(stream), ctx_tiles=ctx_tiles - t0, final_norm=last),
        out_shape=out_shape,
        grid=(n_tiles,),
        in_specs=stream_specs + [
            pl.BlockSpec((tm, ATT_Q), att_row),
            pl.BlockSpec((tm, ML_INNER), row),
            pl.BlockSpec((tm, SSD_INNER), row),
            pl.BlockSpec((tm, 3 * D_MODEL), row),
            _mod_spec(layer, FFN_GROUPS, dims, tile0=t0),
            pl.BlockSpec((1, D_MODEL), lambda i: (0, 0)),
            pl.BlockSpec((1, D_MODEL), lambda i: (0, 0)),
        ] + [_resident(w.shape, layer, mods.shape[0]) for w in wts],
        out_specs=out_spec,
        scratch_shapes=[pltpu.VMEM((tm, D_MODEL), F32), pltpu.VMEM((tm, D_MODEL), BF16)],
        compiler_params=_cparams(("arbitrary",)),
        name="merge_ffn",
    )(*stream, att, ml, ss, gates, mods, g2, gf, *wts)


def _rope_tables(seq, nct):
    pos = np.arange(seq)
    row = (pos // GRID_W).astype(np.float32)
    col = (pos % GRID_W).astype(np.float32)
    inv = (np.float32(ROPE_BASE) ** (-np.arange(ROPE_FREQS, dtype=np.float32) / np.float32(ROPE_FREQS))).astype(np.float32)
    ang_r = (row[:, None] * inv).astype(np.float32)
    ang_c = (col[:, None] * inv).astype(np.float32)
    cos_h = np.concatenate([np.cos(ang_r), np.cos(ang_r), np.cos(ang_c), np.cos(ang_c)], axis=1)
    sin_h = np.concatenate([-np.sin(ang_r), np.sin(ang_r), -np.sin(ang_c), np.sin(ang_c)], axis=1)
    cos_t = np.concatenate([np.ones((nct * CHUNK, LANES)), np.tile(cos_h, (1, 2))], axis=0)
    sin_t = np.concatenate([np.zeros((nct * CHUNK, LANES)), np.tile(sin_h, (1, 2))], axis=0)
    return jnp.asarray(cos_t, F32), jnp.asarray(sin_t, F32)


def _split_w_in(w):
    pad = jnp.zeros(w.shape[:2] + (LANES - 4 * ML_HEADS - 2 * SSD_HEADS,), w.dtype)
    small = jnp.concatenate([w[..., IN_MLG0:IN_Z0], w[..., IN_DT0:IN_G0], pad], axis=-1)
    pieces = [w[..., :W_MIX], w[..., IN_Z0:IN_DT0], w[..., IN_G0:], small]
    return [_stack_rows(piece).astype(BF16) for piece in pieces]


def _lane_row(vals):
    return jnp.zeros((1, LANES), F32).at[0, :vals.shape[0]].set(vals.astype(F32))


def kernel(x, c, ctx, c_ctx, w_mod, b_mod, g_norm1, w_in, att_sink, ml_i_bias, ml_f_bias, ml_head_gain,
           ssd_conv_w, ssd_conv_b, ssd_dt_bias, ssd_a_log, ssd_d, ssd_norm_gain,
           w_att_out, w_ml_out, w_ssd_out, w_o, g_norm2, w_up, w_down, g_final):
    n_b, seq, d_model = x.shape
    lc = ctx.shape[1]
    depth = w_mod.shape[0]
    nct, nlt = lc // CHUNK, seq // CHUNK
    dims = dict(B=n_b, nct=nct, nlt=nlt, n_tok=(nct + nlt) * n_b * CHUNK)
    assert d_model == D_MODEL and seq % CHUNK == 0 and lc % CHUNK == 0 and nlt >= 3 and depth >= 2
    assert n_b % PROJ_GROUPS == 0 and n_b % FFN_GROUPS == 0 and n_b % SCAN_BATCH == 0 and n_b % ATT_BATCH == 0
    assert n_b % ML_BATCH == 0
    assert n_b + PROJ_GROUPS <= MOD_ROWS

    cc = jnp.zeros((MOD_ROWS, d_model), F32).at[:n_b].set(c).at[n_b:n_b + PROJ_GROUPS].set(c_ctx)
    mods = _modulation(cc, w_mod, b_mod).reshape(depth, MOD_ROWS, N_MOD, d_model)
    cos_t, sin_t = _rope_tables(seq, nct)
    w_pieces = _split_w_in(w_in)
    wts = tuple(_stack_rows(w).astype(BF16) for w in (w_att_out, w_ml_out, w_ssd_out, w_o, w_up, w_down))

    stream = [ctx.reshape(n_b, nct, CHUNK, d_model), x.reshape(n_b, nlt, CHUNK, d_model)]
    for l in range(depth):
        first, last = l == 0, l == depth - 1
        gbias = jnp.broadcast_to(jnp.concatenate([ml_i_bias[l], ml_f_bias[l]], axis=1).astype(F32)[:, :, None],
                                 (2, 2 * ML_HEADS, LANES))
        a_neg = -jnp.exp(ssd_a_log[l].astype(F32))
        sbias = jnp.broadcast_to(jnp.concatenate([ssd_dt_bias[l].astype(F32), a_neg], axis=0)[:, :, None],
                                 (4, SSD_HEADS, LANES))
        q, kv, ml, z, xbc, gates, ml_rows, ml_cols, ssd_rows, ssd_cols = _in_projection(
            stream, first, mods, l, g_norm1[l].reshape(1, -1), cos_t, sin_t, w_pieces, gbias, sbias, dims)

        att = _attention(q, kv, _lane_row(att_sink[l]), dims, ctx_queries=not last)

        mlo = _mlstm(ml, ml_rows, ml_cols, ml_head_gain[l].reshape(1, -1), dims)

        conv_w = jnp.zeros((8, SSD_XBC), F32).at[:SSD_CONV].set(ssd_conv_w[l])
        sso = _ssd(z, xbc, ssd_rows, ssd_cols, conv_w, ssd_conv_b[l].reshape(1, -1),
                   jnp.repeat(ssd_d[l].astype(F32), SSD_HEAD_DIM).reshape(1, -1),
                   ssd_norm_gain[l].reshape(1, -1), dims)

        out = _merge_ffn(stream, first, att, mlo, sso, gates, mods, l, g_norm2[l].reshape(1, -1),
                         g_final.reshape(1, -1), wts, dims, last)
        stream = [out]
    return out.reshape(n_b, seq, d_model)
```
